```python
import math
import jax
import jax.numpy as jnp
from jax import lax
import numpy as np

D_MODEL = 4096
BATCH = 4
SEQ = 2048
DEPTH = 4
DEC_BATCH = 8
DEC_SEQ = 8
PAST_LEN = 8192
PAGE_SIZE = 128

N_AB = (DEPTH + 1) // 2
N_C = DEPTH // 2
D_A = D_MODEL // 2
A_HEADS = 16
A_BLK = D_A // A_HEADS
A_CONV = 4
LRU_C = 8.0
B_HEADS = 16
HEAD_DIM = 128
KV_HEADS = 4
HPG = B_HEADS // KV_HEADS
D_B = B_HEADS * HEAD_DIM
D_KV = KV_HEADS * HEAD_DIM
CMP_LEN = 32
CMP_STRIDE = 16
SUB_PER_CMP = CMP_LEN // CMP_STRIDE
SEL_BLOCK = 64
SUB_PER_SEL = SEL_BLOCK // CMP_STRIDE
SEL_TOP = 16
WINDOW = 512
Q_CHUNK = 32
WIN_QBLOCK = 128
D_C = D_MODEL
C_CONV = 3
D_FF = -(-8 * D_MODEL // (3 * 256)) * 256
D_IN_AB = 2 * D_A + D_B + 6 * D_KV + 3 * B_HEADS
RMS_EPS = 1e-6
NEG_INF = -1e30
FORCE_SCORE = 1e4

kernel_name = 'hawk_nsa_shortconv_hybrid_step'


def _rmsnorm(x, g):
    xf = x.astype(jnp.float32)
    y = xf * lax.rsqrt(jnp.mean(xf * xf, axis=-1, keepdims=True) + RMS_EPS)
    return (y * g.astype(jnp.float32)).astype(x.dtype)


def _swiglu(x, wg, wu, wd):
    return (jax.nn.silu(x @ wg) * (x @ wu)) @ wd


def _alibi_slopes():
    h = jnp.arange(1, B_HEADS + 1, dtype=jnp.float32)
    return jnp.exp2(-8.0 * h / B_HEADS).reshape(KV_HEADS, HPG)


def _alibi_probs(s, dist, ok, slopes):
    s = jnp.where(ok, s - slopes[:, :, None] * dist, NEG_INF)
    return jax.nn.softmax(s, axis=-1)


def _causal_dwconv(x, buf, w):
    K = w.shape[0]
    T = x.shape[1]
    xx = jnp.concatenate([buf.astype(x.dtype), x], axis=1)
    y = xx[:, 0:T] * w[0]
    for k in range(1, K):
        y = y + xx[:, k:k + T] * w[k]
    return y, xx[:, xx.shape[1] - (K - 1):]


def _rglru(x, h0, w_a, b_a, w_x, b_x, lam):
    Bn, T, _ = x.shape
    xb = x.reshape(Bn, T, A_HEADS, A_BLK)
    r = jax.nn.sigmoid((jnp.einsum('bthi,hij->bthj', xb, w_a).reshape(Bn, T, D_A) + b_a).astype(jnp.float32))
    ig = jax.nn.sigmoid((jnp.einsum('bthi,hij->bthj', xb, w_x).reshape(Bn, T, D_A) + b_x).astype(jnp.float32))
    log_a = -LRU_C * r * jax.nn.softplus(-lam.astype(jnp.float32))
    a = jnp.exp(log_a)
    u = jnp.sqrt(-jnp.expm1(2.0 * log_a)) * ig * x.astype(jnp.float32)

    def step(h, au):
        h = au[0] * h + au[1]
        return h, h

    hT, hs = lax.scan(step, h0.astype(jnp.float32), (jnp.swapaxes(a, 0, 1), jnp.swapaxes(u, 0, 1)))
    return jnp.swapaxes(hs, 0, 1).astype(x.dtype), hT.astype(h0.dtype)


def _compress(kv, pe, w1, b1, w2):
    Tk = kv.shape[1]
    n_cmp = (Tk - CMP_LEN) // CMP_STRIDE + 1
    idx = jnp.arange(n_cmp)[:, None] * CMP_STRIDE + jnp.arange(CMP_LEN)[None, :]
    blk = kv[:, idx] + pe[:, None, :]
    hid = jax.nn.gelu(jnp.einsum('bnlgd,lde->bnge', blk, w1) + b1)
    return jnp.einsum('bnge,ef->bngf', hid, w2)


def _nsa_global(q, kc, vc, ks, vs, q_pos, slopes):
    Bn, Tq = q.shape[:2]
    Tk = ks.shape[1]
    n_cmp = kc.shape[1]
    n_sel = -(-Tk // SEL_BLOCK)
    top = min(SEL_TOP, n_sel)
    pad = n_sel * SEL_BLOCK - Tk
    def blocks(t):
        t = jnp.pad(t, ((0, 0), (0, pad), (0, 0), (0, 0)))
        return t.reshape(Bn, n_sel, SEL_BLOCK, KV_HEADS, HEAD_DIM).transpose(0, 3, 1, 2, 4)
    ksb, vsb = blocks(ks), blocks(vs)
    cmp_end = jnp.arange(n_cmp) * CMP_STRIDE + CMP_LEN - 1
    blk_ids = jnp.arange(n_sel)
    in_blk = jnp.arange(SEL_BLOCK)
    scale = HEAD_DIM ** -0.5
    chunk = Q_CHUNK if Tq % Q_CHUNK == 0 else Tq
    nq = Tq // chunk
    qc = jnp.swapaxes(q.reshape(Bn, nq, chunk, KV_HEADS, HPG, HEAD_DIM), 0, 1)
    pc = q_pos.reshape(nq, chunk)
    gather = jax.vmap(jax.vmap(lambda kb, ib: kb[ib]))

    def body(args):
        qb, tb = args
        C = qb.shape[1]
        dist_c = (tb[:, None] - cmp_end[None, :]).astype(jnp.float32)
        ok_c = (dist_c >= 0)[None, :, None, None, :]
        s_c = jnp.einsum('bcgpd,bngd->bcgpn', qb, kc).astype(jnp.float32) * scale
        p_c = _alibi_probs(s_c, dist_c[None, :, None, None, :], ok_c, slopes) * ok_c
        o_c = jnp.einsum('bcgpn,bngd->bcgpd', p_c.astype(vc.dtype), vc)
        imp = p_c.sum(axis=3)
        imp = sum(jnp.pad(imp, ((0, 0), (0, 0), (0, 0), (n, SUB_PER_CMP - 1 - n))) for n in range(SUB_PER_CMP)) / SUB_PER_CMP
        imp = jnp.pad(imp, ((0, 0), (0, 0), (0, 0), (0, n_sel * SUB_PER_SEL - imp.shape[-1])))
        imp = imp.reshape(Bn, C, KV_HEADS, n_sel, SUB_PER_SEL).sum(-1)
        cur = tb // SEL_BLOCK
        forced = (blk_ids[None] == 0) | (blk_ids[None] == cur[:, None]) | (blk_ids[None] == cur[:, None] - 1)
        valid = blk_ids[None] * SEL_BLOCK <= tb[:, None]
        score = jnp.where(forced[None, :, None, :], FORCE_SCORE, jnp.where(valid[None, :, None, :], imp, -1.0))
        _, idx = lax.top_k(score, top)
        idx_t = idx.transpose(0, 2, 1, 3)
        kg = gather(ksb, idx_t)
        vg = gather(vsb, idx_t)
        kpos = idx_t[..., None] * SEL_BLOCK + in_blk
        dist_s = (tb[None, None, :, None, None] - kpos).astype(jnp.float32)
        dist_s = dist_s.transpose(0, 2, 1, 3, 4).reshape(Bn, C, KV_HEADS, 1, top * SEL_BLOCK)
        s_s = jnp.einsum('bcgpd,bgcksd->bcgpks', qb, kg).astype(jnp.float32) * scale
        s_s = s_s.reshape(Bn, C, KV_HEADS, HPG, top * SEL_BLOCK)
        p_s = _alibi_probs(s_s, dist_s, dist_s >= 0, slopes).reshape(Bn, C, KV_HEADS, HPG, top, SEL_BLOCK)
        o_s = jnp.einsum('bcgpks,bgcksd->bcgpd', p_s.astype(vg.dtype), vg)
        return o_c, o_s

    o_c, o_s = lax.map(body, (qc, pc))
    shp = (Bn, Tq, KV_HEADS, HPG, HEAD_DIM)
    return jnp.swapaxes(o_c, 0, 1).reshape(shp), jnp.swapaxes(o_s, 0, 1).reshape(shp)


def _window_prompt(q, k, v, slopes):
    Bn, T = q.shape[:2]
    nb = T // WIN_QBLOCK
    span = WIN_QBLOCK + WINDOW
    padw = ((0, 0), (WINDOW, 0), (0, 0), (0, 0))
    idx = jnp.arange(nb)[:, None] * WIN_QBLOCK + jnp.arange(span)[None, :]
    kb = jnp.pad(k, padw)[:, idx]
    vb = jnp.pad(v, padw)[:, idx]
    qpos = jnp.arange(nb)[:, None] * WIN_QBLOCK + jnp.arange(WIN_QBLOCK)[None, :]
    kpos = idx - WINDOW
    dist = qpos[:, :, None] - kpos[:, None, :]
    ok = (dist >= 0) & (dist < WINDOW) & (kpos[:, None, :] >= 0)
    qb = q.reshape(Bn, nb, WIN_QBLOCK, KV_HEADS, HPG, HEAD_DIM)
    s = jnp.einsum('bnqgpd,bnsgd->bnqgps', qb, kb).astype(jnp.float32) * (HEAD_DIM ** -0.5)
    p = _alibi_probs(s, dist.astype(jnp.float32)[None, :, :, None, None, :], ok[None, :, :, None, None, :], slopes)
    o = jnp.einsum('bnqgps,bnsgd->bnqgpd', p.astype(vb.dtype), vb)
    return o.reshape(Bn, T, KV_HEADS, HPG, HEAD_DIM)


def _window_decode(q, k, v, q_pos, k_pos, slopes):
    dist = q_pos[:, None] - k_pos[None, :]
    ok = (dist >= 0) & (dist < WINDOW)
    s = jnp.einsum('btgpd,bsgd->btgps', q, k).astype(jnp.float32) * (HEAD_DIM ** -0.5)
    p = _alibi_probs(s, dist.astype(jnp.float32)[None, :, None, None, :], ok[None, :, None, None, :], slopes)
    return jnp.einsum('btgps,bsgd->btgpd', p.astype(v.dtype), v)


def _ab_mixer(hn, q_pos, lru_h0, lru_buf, kv_cmp_past, kv_sel_past, win_buf,
              w_in, conv_w, conv_b, ga_w, ga_b, gx_w, gx_b, lam, cmp_pe, cmp_w1, cmp_b1, cmp_w2, w_out, slopes):
    Bn, T, _ = hn.shape
    z = hn @ w_in
    sizes = (D_A, D_A, D_B, 2 * D_KV, 2 * D_KV, 2 * D_KV, 3 * B_HEADS)
    cuts = [int(c) for c in np.cumsum(sizes)[:-1]]
    xa, ga, q, kvc, kvs, kvw, gt = jnp.split(z, cuts, axis=-1)
    kv_shape = (Bn, T, 2, KV_HEADS, HEAD_DIM)
    kvc, kvs, kvw = kvc.reshape(kv_shape), kvs.reshape(kv_shape), kvw.reshape(kv_shape)
    xc, new_lru_buf = _causal_dwconv(xa, lru_buf, conv_w)
    y_lru, new_h = _rglru(xc + conv_b, lru_h0, ga_w, ga_b, gx_w, gx_b, lam)
    y_a = y_lru * jax.nn.gelu(ga)
    q = q.reshape(Bn, T, KV_HEADS, HPG, HEAD_DIM)
    kvc_all = kvc if kv_cmp_past is None else jnp.concatenate([kv_cmp_past.astype(kvc.dtype), kvc], axis=1)
    kvs_all = kvs if kv_sel_past is None else jnp.concatenate([kv_sel_past.astype(kvs.dtype), kvs], axis=1)
    kc = _compress(kvc_all[:, :, 0], cmp_pe[0], cmp_w1[0], cmp_b1[0], cmp_w2[0])
    vc = _compress(kvc_all[:, :, 1], cmp_pe[1], cmp_w1[1], cmp_b1[1], cmp_w2[1])
    o_c, o_s = _nsa_global(q, kc, vc, kvs_all[:, :, 0], kvs_all[:, :, 1], q_pos, slopes)
    if win_buf is None:
        o_w = _window_prompt(q, kvw[:, :, 0], kvw[:, :, 1], slopes)
        new_win = kvw[:, T - min(WINDOW, T):]
    else:
        kvw_all = jnp.concatenate([win_buf.astype(kvw.dtype), kvw], axis=1)
        Lw = kvw_all.shape[1]
        k_pos = q_pos[0] - win_buf.shape[1] + jnp.arange(Lw, dtype=jnp.int32)
        o_w = _window_decode(q, kvw_all[:, :, 0], kvw_all[:, :, 1], q_pos, k_pos, slopes)
        new_win = kvw_all[:, Lw - min(WINDOW, Lw):]
    g = jax.nn.sigmoid(gt.astype(jnp.float32)).reshape(Bn, T, 3, KV_HEADS, HPG, 1)
    o_b = (g[:, :, 0] * o_c + g[:, :, 1] * o_s + g[:, :, 2] * o_w).astype(hn.dtype).reshape(Bn, T, D_B)
    out = jnp.concatenate([y_a, o_b], axis=-1) @ w_out
    return out, kvc, kvs, new_win, new_h, new_lru_buf


def _c_mixer(hn, buf, w_in, conv_w, w_out):
    b_g, c_g, v = jnp.split(hn @ w_in, 3, axis=-1)
    y, new_buf = _causal_dwconv(c_g * v, buf, conv_w)
    return (b_g * y) @ w_out, new_buf


def setup_inputs(seed: int = 0) -> dict:
    key = jax.random.key(seed)
    k = jax.random.split(key, 32)
    f32 = jnp.float32

    def nrm(kk, shape, scale):
        return jax.random.normal(kk, shape, f32) * scale

    n_pages = PAST_LEN // PAGE_SIZE
    n_used = DEC_BATCH * n_pages
    n_pool = (5 * n_used + 3) // 4
    w_buf = min(WINDOW, PAST_LEN)
    page_table = jax.random.permutation(k[0], n_pool)[:n_used].reshape(DEC_BATCH, n_pages).astype(jnp.int32)
    a0 = jax.random.uniform(k[1], (N_AB, D_A), f32, 0.9, 0.999)
    return {
        'x_prompt': nrm(k[2], (BATCH, SEQ, D_MODEL), 1.0),
        'x_sample': nrm(k[3], (DEC_BATCH, DEC_SEQ, D_MODEL), 1.0),
        'cache_kv_cmp': nrm(k[4], (N_AB, n_pool, PAGE_SIZE, 2, KV_HEADS, HEAD_DIM), 1.0),
        'cache_kv_sel': nrm(k[5], (N_AB, n_pool, PAGE_SIZE, 2, KV_HEADS, HEAD_DIM), 1.0),
        'state_win_kv': nrm(k[6], (N_AB, DEC_BATCH, w_buf, 2, KV_HEADS, HEAD_DIM), 1.0),
        'state_lru_h': nrm(k[7], (N_AB, DEC_BATCH, D_A), 0.5),
        'state_lru_conv': nrm(k[8], (N_AB, DEC_BATCH, A_CONV - 1, D_A), 1.0),
        'state_sconv': nrm(k[9], (N_C, DEC_BATCH, C_CONV - 1, D_C), 1.0),
        'page_table': page_table,
        'norm_mix': 1.0 + nrm(k[10], (DEPTH, D_MODEL), 0.02),
        'norm_ffn': 1.0 + nrm(k[11], (DEPTH, D_MODEL), 0.02),
        'norm_final': 1.0 + nrm(k[12], (D_MODEL,), 0.02),
        'ab_w_in': nrm(k[13], (N_AB, D_MODEL, D_IN_AB), D_MODEL ** -0.5),
        'ab_conv_w': nrm(k[14], (N_AB, A_CONV, D_A), A_CONV ** -0.5),
        'ab_conv_b': nrm(k[15], (N_AB, D_A), 0.02),
        'ab_gate_a_w': nrm(k[16], (N_AB, A_HEADS, A_BLK, A_BLK), A_BLK ** -0.5),
        'ab_gate_a_b': nrm(k[17], (N_AB, D_A), 0.02),
        'ab_gate_x_w': nrm(k[18], (N_AB, A_HEADS, A_BLK, A_BLK), A_BLK ** -0.5),
        'ab_gate_x_b': nrm(k[19], (N_AB, D_A), 0.02),
        'ab_lru_lambda': jnp.log(a0) - jnp.log1p(-a0),
        'ab_cmp_pe': nrm(k[20], (N_AB, 2, CMP_LEN, HEAD_DIM), 0.1),
        'ab_cmp_w1': nrm(k[21], (N_AB, 2, CMP_LEN, HEAD_DIM, HEAD_DIM), (CMP_LEN * HEAD_DIM) ** -0.5),
        'ab_cmp_b1': nrm(k[22], (N_AB, 2, HEAD_DIM), 0.02),
        'ab_cmp_w2': nrm(k[23], (N_AB, 2, HEAD_DIM, HEAD_DIM), HEAD_DIM ** -0.5),
        'ab_w_out': nrm(k[24], (N_AB, D_A + D_B, D_MODEL), (D_A + D_B) ** -0.5),
        'c_w_in': nrm(k[25], (N_C, D_MODEL, 3 * D_C), D_MODEL ** -0.5),
        'c_conv_w': nrm(k[26], (N_C, C_CONV, D_C), C_CONV ** -0.5),
        'c_w_out': nrm(k[27], (N_C, D_C, D_MODEL), D_C ** -0.5),
        'ffn_w_gate': nrm(k[28], (DEPTH, D_MODEL, D_FF), D_MODEL ** -0.5),
        'ffn_w_up': nrm(k[29], (DEPTH, D_MODEL, D_FF), D_MODEL ** -0.5),
        'ffn_w_down': nrm(k[30], (DEPTH, D_FF, D_MODEL), D_FF ** -0.5),
    }


def reference(x_prompt, x_sample, cache_kv_cmp, cache_kv_sel, state_win_kv, state_lru_h, state_lru_conv, state_sconv,
              page_table, norm_mix, norm_ffn, norm_final, ab_w_in, ab_conv_w, ab_conv_b, ab_gate_a_w, ab_gate_a_b,
              ab_gate_x_w, ab_gate_x_b, ab_lru_lambda, ab_cmp_pe, ab_cmp_w1, ab_cmp_b1, ab_cmp_w2, ab_w_out,
              c_w_in, c_conv_w, c_w_out, ffn_w_gate, ffn_w_up, ffn_w_down):
    slopes = _alibi_slopes()
    Bp, T = x_prompt.shape[:2]
    Bs, Ts = x_sample.shape[:2]
    pos_p = jnp.arange(T, dtype=jnp.int32)
    pos_s = PAST_LEN + jnp.arange(Ts, dtype=jnp.int32)
    hp, hs = x_prompt, x_sample
    p_ab = [[], [], [], [], []]
    s_ab = [[], [], [], [], []]
    p_c, s_c = [], []
    for layer in range(DEPTH):
        hpn = _rmsnorm(hp, norm_mix[layer])
        hsn = _rmsnorm(hs, norm_mix[layer])
        if layer % 2 == 0:
            i = layer // 2
            wts = (ab_w_in[i], ab_conv_w[i], ab_conv_b[i], ab_gate_a_w[i], ab_gate_a_b[i], ab_gate_x_w[i],
                   ab_gate_x_b[i], ab_lru_lambda[i], ab_cmp_pe[i], ab_cmp_w1[i], ab_cmp_b1[i], ab_cmp_w2[i],
                   ab_w_out[i], slopes)
            mp, *sp = _ab_mixer(hpn, pos_p, jnp.zeros((Bp, D_A), hp.dtype), jnp.zeros((Bp, A_CONV - 1, D_A), hp.dtype),
                                None, None, None, *wts)
            past_cmp = cache_kv_cmp[i][page_table].reshape(Bs, -1, 2, KV_HEADS, HEAD_DIM)
            past_sel = cache_kv_sel[i][page_table].reshape(Bs, -1, 2, KV_HEADS, HEAD_DIM)
            ms, *ss = _ab_mixer(hsn, pos_s, state_lru_h[i], state_lru_conv[i], past_cmp, past_sel,
                                state_win_kv[i], *wts)
            for j in range(5):
                p_ab[j].append(sp[j])
                s_ab[j].append(ss[j])
        else:
            i = layer // 2
            mp, bp = _c_mixer(hpn, jnp.zeros((Bp, C_CONV - 1, D_C), hp.dtype), c_w_in[i], c_conv_w[i], c_w_out[i])
            ms, bs = _c_mixer(hsn, state_sconv[i], c_w_in[i], c_conv_w[i], c_w_out[i])
            p_c.append(bp)
            s_c.append(bs)
        hp = hp + mp
        hs = hs + ms
        hp = hp + _swiglu(_rmsnorm(hp, norm_ffn[layer]), ffn_w_gate[layer], ffn_w_up[layer], ffn_w_down[layer])
        hs = hs + _swiglu(_rmsnorm(hs, norm_ffn[layer]), ffn_w_gate[layer], ffn_w_up[layer], ffn_w_down[layer])
    y_prompt = _rmsnorm(hp, norm_final)
    y_sample = _rmsnorm(hs, norm_final)
    return (y_prompt, y_sample,
            jnp.stack(p_ab[0]), jnp.stack(p_ab[1]), jnp.stack(p_ab[2]), jnp.stack(p_ab[3]), jnp.stack(p_ab[4]), jnp.stack(p_c),
            jnp.stack(s_ab[0]), jnp.stack(s_ab[1]), jnp.stack(s_ab[2]), jnp.stack(s_ab[3]), jnp.stack(s_ab[4]), jnp.stack(s_c))
```

```python
import functools
import math

import jax
import jax.numpy as jnp
import numpy as np
from jax import lax
from jax.experimental import pallas as pl
from jax.experimental.pallas import tpu as pltpu

D_MODEL = 4096
DEPTH = 4
PAST_LEN = 8192
PAGE_SIZE = 128
N_AB = (DEPTH + 1) // 2
N_C = DEPTH // 2
D_A = D_MODEL // 2
A_HEADS = 16
A_BLK = D_A // A_HEADS
A_CONV = 4
LRU_C = 8.0
B_HEADS = 16
HEAD_DIM = 128
KV_HEADS = 4
HPG = B_HEADS // KV_HEADS
D_B = B_HEADS * HEAD_DIM
D_KV = KV_HEADS * HEAD_DIM
CMP_LEN = 32
CMP_STRIDE = 16
SUB_PER_CMP = CMP_LEN // CMP_STRIDE
SEL_BLOCK = 64
SUB_PER_SEL = SEL_BLOCK // CMP_STRIDE
SEL_TOP = 16
WINDOW = 512
Q_CHUNK = 32
WIN_QBLOCK = 128
D_C = D_MODEL
C_CONV = 3
D_FF = -(-8 * D_MODEL // (3 * 256)) * 256
D_IN_AB = 2 * D_A + D_B + 6 * D_KV + 3 * B_HEADS
D_IN_MAIN = 2 * D_A + D_B + 6 * D_KV
RMS_EPS = 1e-6
NEG_INF = -1e30
FORCE_SCORE = 1e4

V7X_VMEM_LIMIT_BYTES = 56 * 1024 * 1024
LANE = 128


def _row_tile(m):
    for t in range(1024, 15, -16):
        if m % t == 0:
            return t
    raise ValueError(m)


def _col_tile(n, cap=512):
    for t in range(cap, LANE - 1, -LANE):
        if n % t == 0:
            return t
    raise ValueError(n)


def _params(sem):
    return pltpu.CompilerParams(dimension_semantics=sem, vmem_limit_bytes=V7X_VMEM_LIMIT_BYTES)


def _rmsnorm_kernel(x_ref, g_ref, o_ref):
    x = x_ref[...]
    y = x * lax.rsqrt(jnp.mean(x * x, axis=-1, keepdims=True) + RMS_EPS)
    o_ref[...] = (y * g_ref[...]).astype(o_ref.dtype)


def _rmsnorm(x, g, out_dtype):
    m, d = x.shape
    tm = _row_tile(m) // 2 if _row_tile(m) % 32 == 0 else _row_tile(m)
    return pl.pallas_call(
        _rmsnorm_kernel,
        out_shape=jax.ShapeDtypeStruct((m, d), out_dtype),
        grid=(m // tm,),
        in_specs=[pl.BlockSpec((tm, d), lambda i: (i, 0)), pl.BlockSpec((1, d), lambda i: (0, 0))],
        out_specs=pl.BlockSpec((tm, d), lambda i: (i, 0)),
        compiler_params=_params(("parallel",)),
        name="rmsnorm",
    )(x, g.reshape(1, d))


def _mm_kernel(x_ref, w_ref, o_ref):
    o_ref[...] = jnp.dot(x_ref[...], w_ref[...], preferred_element_type=jnp.float32).astype(o_ref.dtype)


def _matmul(x, w, out_dtype=jnp.float32, name="matmul"):
    m, k = x.shape
    n = w.shape[1]
    tm, tn = _row_tile(m), _col_tile(n)
    return pl.pallas_call(
        _mm_kernel,
        out_shape=jax.ShapeDtypeStruct((m, n), out_dtype),
        grid=(m // tm, n // tn),
        in_specs=[pl.BlockSpec((tm, k), lambda i, j: (i, 0)), pl.BlockSpec((k, tn), lambda i, j: (0, j))],
        out_specs=pl.BlockSpec((tm, tn), lambda i, j: (i, j)),
        compiler_params=_params(("parallel", "arbitrary")),
        name=name,
    )(x, w)


def _mm_res_kernel(x_ref, w_ref, r_ref, o_ref):
    o_ref[...] = r_ref[...] + jnp.dot(x_ref[...], w_ref[...], preferred_element_type=jnp.float32)


def _matmul_residual(x, w, r, name="matmul_res"):
    m, k = x.shape
    n = w.shape[1]
    tm, tn = _row_tile(m), _col_tile(n)
    return pl.pallas_call(
        _mm_res_kernel,
        out_shape=jax.ShapeDtypeStruct((m, n), jnp.float32),
        grid=(m // tm, n // tn),
        in_specs=[pl.BlockSpec((tm, k), lambda i, j: (i, 0)), pl.BlockSpec((k, tn), lambda i, j: (0, j)),
                  pl.BlockSpec((tm, tn), lambda i, j: (i, j))],
        out_specs=pl.BlockSpec((tm, tn), lambda i, j: (i, j)),
        compiler_params=_params(("parallel", "arbitrary")),
        name=name,
    )(x, w, r)


def _ffn_up_kernel(x_ref, wg_ref, wu_ref, o_ref):
    x = x_ref[...]
    g = jnp.dot(x, wg_ref[...], preferred_element_type=jnp.float32)
    u = jnp.dot(x, wu_ref[...], preferred_element_type=jnp.float32)
    o_ref[...] = (g * jax.nn.sigmoid(g) * u).astype(o_ref.dtype)


def _ffn_up(x, wg, wu):
    m, k = x.shape
    n = wg.shape[1]
    tm, tn = _row_tile(m), _col_tile(n)
    return pl.pallas_call(
        _ffn_up_kernel,
        out_shape=jax.ShapeDtypeStruct((m, n), jnp.bfloat16),
        grid=(m // tm, n // tn),
        in_specs=[pl.BlockSpec((tm, k), lambda i, j: (i, 0)), pl.BlockSpec((k, tn), lambda i, j: (0, j)),
                  pl.BlockSpec((k, tn), lambda i, j: (0, j))],
        out_specs=pl.BlockSpec((tm, tn), lambda i, j: (i, j)),
        compiler_params=_params(("parallel", "arbitrary")),
        name="ffn_up",
    )(x, wg, wu)


def _mm_acc_res_kernel(x_ref, w_ref, r_ref, o_ref, acc_ref):
    kk = pl.program_id(2)

    @pl.when(kk == 0)
    def _():
        acc_ref[...] = r_ref[...]

    acc_ref[...] += jnp.dot(x_ref[...], w_ref[...], preferred_element_type=jnp.float32)

    @pl.when(kk == pl.num_programs(2) - 1)
    def _():
        o_ref[...] = acc_ref[...]


def _ffn_down(h, wd, r):
    m, k = h.shape
    n = wd.shape[1]
    tm, tn = _row_tile(m), _col_tile(n)
    nk = 2
    tk = k // nk
    assert tk * nk == k and tk % LANE == 0
    return pl.pallas_call(
        _mm_acc_res_kernel,
        out_shape=jax.ShapeDtypeStruct((m, n), jnp.float32),
        grid=(m // tm, n // tn, nk),
        in_specs=[pl.BlockSpec((tm, tk), lambda i, j, kk: (i, kk)), pl.BlockSpec((tk, tn), lambda i, j, kk: (kk, j)),
                  pl.BlockSpec((tm, tn), lambda i, j, kk: (i, j))],
        out_specs=pl.BlockSpec((tm, tn), lambda i, j, kk: (i, j)),
        scratch_shapes=[pltpu.VMEM((tm, tn), jnp.float32)],
        compiler_params=_params(("parallel", "arbitrary", "arbitrary")),
        name="ffn_down",
    )(h, wd, r)


def _alibi_slopes():
    h = jnp.arange(1, B_HEADS + 1, dtype=jnp.float32)
    return jnp.exp2(-8.0 * h / B_HEADS).reshape(KV_HEADS, HPG)


def _alibi_probs(s, dist, ok, slopes):
    s = jnp.where(ok, s - slopes[:, :, None] * dist, NEG_INF)
    return jax.nn.softmax(s, axis=-1)


def _causal_dwconv(x, buf, w):
    K = w.shape[0]
    T = x.shape[1]
    xx = jnp.concatenate([buf.astype(x.dtype), x], axis=1)
    y = xx[:, 0:T] * w[0]
    for k in range(1, K):
        y = y + xx[:, k:k + T] * w[k]
    return y, xx[:, xx.shape[1] - (K - 1):]


def _rglru(x, h0, w_a, b_a, w_x, b_x, lam):
    Bn, T, _ = x.shape
    xb = x.reshape(Bn, T, A_HEADS, A_BLK)
    r = jax.nn.sigmoid((jnp.einsum('bthi,hij->bthj', xb, w_a).reshape(Bn, T, D_A) + b_a).astype(jnp.float32))
    ig = jax.nn.sigmoid((jnp.einsum('bthi,hij->bthj', xb, w_x).reshape(Bn, T, D_A) + b_x).astype(jnp.float32))
    log_a = -LRU_C * r * jax.nn.softplus(-lam.astype(jnp.float32))
    a = jnp.exp(log_a)
    u = jnp.sqrt(-jnp.expm1(2.0 * log_a)) * ig * x.astype(jnp.float32)

    def step(h, au):
        h = au[0] * h + au[1]
        return h, h

    hT, hs = lax.scan(step, h0.astype(jnp.float32), (jnp.swapaxes(a, 0, 1), jnp.swapaxes(u, 0, 1)))
    return jnp.swapaxes(hs, 0, 1).astype(x.dtype), hT.astype(h0.dtype)


def _compress(kv, pe, w1, b1, w2):
    Tk = kv.shape[1]
    n_cmp = (Tk - CMP_LEN) // CMP_STRIDE + 1
    idx = jnp.arange(n_cmp)[:, None] * CMP_STRIDE + jnp.arange(CMP_LEN)[None, :]
    blk = kv[:, idx] + pe[:, None, :]
    hid = jax.nn.gelu(jnp.einsum('bnlgd,lde->bnge', blk, w1) + b1)
    return jnp.einsum('bnge,ef->bngf', hid, w2)


def _nsa_global(q, kc, vc, ks, vs, q_pos, slopes):
    Bn, Tq = q.shape[:2]
    Tk = ks.shape[1]
    n_cmp = kc.shape[1]
    n_sel = -(-Tk // SEL_BLOCK)
    top = min(SEL_TOP, n_sel)
    pad = n_sel * SEL_BLOCK - Tk

    def blocks(t):
        t = jnp.pad(t, ((0, 0), (0, pad), (0, 0), (0, 0)))
        return t.reshape(Bn, n_sel, SEL_BLOCK, KV_HEADS, HEAD_DIM).transpose(0, 3, 1, 2, 4)
    ksb, vsb = blocks(ks), blocks(vs)
    cmp_end = jnp.arange(n_cmp) * CMP_STRIDE + CMP_LEN - 1
    blk_ids = jnp.arange(n_sel)
    in_blk = jnp.arange(SEL_BLOCK)
    scale = HEAD_DIM ** -0.5
    chunk = Q_CHUNK if Tq % Q_CHUNK == 0 else Tq
    nq = Tq // chunk
    qc = jnp.swapaxes(q.reshape(Bn, nq, chunk, KV_HEADS, HPG, HEAD_DIM), 0, 1)
    pc = q_pos.reshape(nq, chunk)
    gather = jax.vmap(jax.vmap(lambda kb, ib: kb[ib]))

    def body(args):
        qb, tb = args
        C = qb.shape[1]
        dist_c = (tb[:, None] - cmp_end[None, :]).astype(jnp.float32)
        ok_c = (dist_c >= 0)[None, :, None, None, :]
        s_c = jnp.einsum('bcgpd,bngd->bcgpn', qb, kc).astype(jnp.float32) * scale
        p_c = _alibi_probs(s_c, dist_c[None, :, None, None, :], ok_c, slopes) * ok_c
        o_c = jnp.einsum('bcgpn,bngd->bcgpd', p_c.astype(vc.dtype), vc)
        imp = p_c.sum(axis=3)
        imp = sum(jnp.pad(imp, ((0, 0), (0, 0), (0, 0), (n, SUB_PER_CMP - 1 - n))) for n in range(SUB_PER_CMP)) / SUB_PER_CMP
        imp = jnp.pad(imp, ((0, 0), (0, 0), (0, 0), (0, n_sel * SUB_PER_SEL - imp.shape[-1])))
        imp = imp.reshape(Bn, C, KV_HEADS, n_sel, SUB_PER_SEL).sum(-1)
        cur = tb // SEL_BLOCK
        forced = (blk_ids[None] == 0) | (blk_ids[None] == cur[:, None]) | (blk_ids[None] == cur[:, None] - 1)
        valid = blk_ids[None] * SEL_BLOCK <= tb[:, None]
        score = jnp.where(forced[None, :, None, :], FORCE_SCORE, jnp.where(valid[None, :, None, :], imp, -1.0))
        _, idx = lax.top_k(score, top)
        idx_t = idx.transpose(0, 2, 1, 3)
        kg = gather(ksb, idx_t)
        vg = gather(vsb, idx_t)
        kpos = idx_t[..., None] * SEL_BLOCK + in_blk
        dist_s = (tb[None, None, :, None, None] - kpos).astype(jnp.float32)
        dist_s = dist_s.transpose(0, 2, 1, 3, 4).reshape(Bn, C, KV_HEADS, 1, top * SEL_BLOCK)
        s_s = jnp.einsum('bcgpd,bgcksd->bcgpks', qb, kg).astype(jnp.float32) * scale
        s_s = s_s.reshape(Bn, C, KV_HEADS, HPG, top * SEL_BLOCK)
        p_s = _alibi_probs(s_s, dist_s, dist_s >= 0, slopes).reshape(Bn, C, KV_HEADS, HPG, top, SEL_BLOCK)
        o_s = jnp.einsum('bcgpks,bgcksd->bcgpd', p_s.astype(vg.dtype), vg)
        return o_c, o_s

    o_c, o_s = lax.map(body, (qc, pc))
    shp = (Bn, Tq, KV_HEADS, HPG, HEAD_DIM)
    return jnp.swapaxes(o_c, 0, 1).reshape(shp), jnp.swapaxes(o_s, 0, 1).reshape(shp)


def _window_prompt(q, k, v, slopes):
    Bn, T = q.shape[:2]
    nb = T // WIN_QBLOCK
    span = WIN_QBLOCK + WINDOW
    padw = ((0, 0), (WINDOW, 0), (0, 0), (0, 0))
    idx = jnp.arange(nb)[:, None] * WIN_QBLOCK + jnp.arange(span)[None, :]
    kb = jnp.pad(k, padw)[:, idx]
    vb = jnp.pad(v, padw)[:, idx]
    qpos = jnp.arange(nb)[:, None] * WIN_QBLOCK + jnp.arange(WIN_QBLOCK)[None, :]
    kpos = idx - WINDOW
    dist = qpos[:, :, None] - kpos[:, None, :]
    ok = (dist >= 0) & (dist < WINDOW) & (kpos[:, None, :] >= 0)
    qb = q.reshape(Bn, nb, WIN_QBLOCK, KV_HEADS, HPG, HEAD_DIM)
    s = jnp.einsum('bnqgpd,bnsgd->bnqgps', qb, kb).astype(jnp.float32) * (HEAD_DIM ** -0.5)
    p = _alibi_probs(s, dist.astype(jnp.float32)[None, :, :, None, None, :], ok[None, :, :, None, None, :], slopes)
    o = jnp.einsum('bnqgps,bnsgd->bnqgpd', p.astype(vb.dtype), vb)
    return o.reshape(Bn, T, KV_HEADS, HPG, HEAD_DIM)


def _window_decode(q, k, v, q_pos, k_pos, slopes):
    dist = q_pos[:, None] - k_pos[None, :]
    ok = (dist >= 0) & (dist < WINDOW)
    s = jnp.einsum('btgpd,bsgd->btgps', q, k).astype(jnp.float32) * (HEAD_DIM ** -0.5)
    p = _alibi_probs(s, dist.astype(jnp.float32)[None, :, None, None, :], ok[None, :, None, None, :], slopes)
    return jnp.einsum('btgps,bsgd->btgpd', p.astype(v.dtype), v)


def _ab_mixer_core(z, gt, q_pos, lru_h0, lru_buf, kv_cmp_past, kv_sel_past, win_buf,
                   conv_w, conv_b, ga_w, ga_b, gx_w, gx_b, lam, cmp_pe, cmp_w1, cmp_b1, cmp_w2, slopes):
    Bn, T, _ = z.shape
    sizes = (D_A, D_A, D_B, 2 * D_KV, 2 * D_KV)
    cuts = [int(c) for c in np.cumsum(sizes)]
    xa, ga, q, kvc, kvs, kvw = jnp.split(z, cuts, axis=-1)
    kv_shape = (Bn, T, 2, KV_HEADS, HEAD_DIM)
    kvc, kvs, kvw = kvc.reshape(kv_shape), kvs.reshape(kv_shape), kvw.reshape(kv_shape)
    xc, new_lru_buf = _causal_dwconv(xa, lru_buf, conv_w)
    y_lru, new_h = _rglru(xc + conv_b, lru_h0, ga_w, ga_b, gx_w, gx_b, lam)
    y_a = y_lru * jax.nn.gelu(ga)
    q = q.reshape(Bn, T, KV_HEADS, HPG, HEAD_DIM)
    kvc_all = kvc if kv_cmp_past is None else jnp.concatenate([kv_cmp_past.astype(kvc.dtype), kvc], axis=1)
    kvs_all = kvs if kv_sel_past is None else jnp.concatenate([kv_sel_past.astype(kvs.dtype), kvs], axis=1)
    kc = _compress(kvc_all[:, :, 0], cmp_pe[0], cmp_w1[0], cmp_b1[0], cmp_w2[0])
    vc = _compress(kvc_all[:, :, 1], cmp_pe[1], cmp_w1[1], cmp_b1[1], cmp_w2[1])
    o_c, o_s = _nsa_global(q, kc, vc, kvs_all[:, :, 0], kvs_all[:, :, 1], q_pos, slopes)
    if win_buf is None:
        o_w = _window_prompt(q, kvw[:, :, 0], kvw[:, :, 1], slopes)
        new_win = kvw[:, T - min(WINDOW, T):]
    else:
        kvw_all = jnp.concatenate([win_buf.astype(kvw.dtype), kvw], axis=1)
        Lw = kvw_all.shape[1]
        k_pos = q_pos[0] - win_buf.shape[1] + jnp.arange(Lw, dtype=jnp.int32)
        o_w = _window_decode(q, kvw_all[:, :, 0], kvw_all[:, :, 1], q_pos, k_pos, slopes)
        new_win = kvw_all[:, Lw - min(WINDOW, Lw):]
    g = jax.nn.sigmoid(gt.astype(jnp.float32)).reshape(Bn, T, 3, KV_HEADS, HPG, 1)
    o_b = (g[:, :, 0] * o_c + g[:, :, 1] * o_s + g[:, :, 2] * o_w).astype(z.dtype).reshape(Bn, T, D_B)
    mix = jnp.concatenate([y_a, o_b], axis=-1)
    return mix, kvc, kvs, new_win, new_h, new_lru_buf


def kernel(x_prompt, x_sample, cache_kv_cmp, cache_kv_sel, state_win_kv, state_lru_h, state_lru_conv, state_sconv,
           page_table, norm_mix, norm_ffn, norm_final, ab_w_in, ab_conv_w, ab_conv_b, ab_gate_a_w, ab_gate_a_b,
           ab_gate_x_w, ab_gate_x_b, ab_lru_lambda, ab_cmp_pe, ab_cmp_w1, ab_cmp_b1, ab_cmp_w2, ab_w_out,
           c_w_in, c_conv_w, c_w_out, ffn_w_gate, ffn_w_up, ffn_w_down):
    bf16 = jnp.bfloat16
    slopes = _alibi_slopes()
    Bp, T = x_prompt.shape[:2]
    Bs, Ts = x_sample.shape[:2]
    mp, ms = Bp * T, Bs * Ts
    pos_p = jnp.arange(T, dtype=jnp.int32)
    pos_s = PAST_LEN + jnp.arange(Ts, dtype=jnp.int32)
    h = jnp.concatenate([x_prompt.reshape(mp, D_MODEL), x_sample.reshape(ms, D_MODEL)], axis=0)
    p_ab = [[], [], [], [], []]
    s_ab = [[], [], [], [], []]
    p_c, s_c = [], []
    for layer in range(DEPTH):
        hn = _rmsnorm(h, norm_mix[layer], bf16)
        i = layer // 2
        if layer % 2 == 0:
            w_in = ab_w_in[i].astype(bf16)
            z = _matmul(hn, w_in[:, :D_IN_MAIN], name="ab_in_proj")
            w_gt = jnp.pad(w_in[:, D_IN_MAIN:], ((0, 0), (0, LANE - 3 * B_HEADS)))
            gt = _matmul(hn, w_gt, name="ab_gate_proj")[:, :3 * B_HEADS]
            wts = (ab_conv_w[i], ab_conv_b[i], ab_gate_a_w[i], ab_gate_a_b[i], ab_gate_x_w[i],
                   ab_gate_x_b[i], ab_lru_lambda[i], ab_cmp_pe[i], ab_cmp_w1[i], ab_cmp_b1[i], ab_cmp_w2[i], slopes)
            mix_p, *sp = _ab_mixer_core(z[:mp].reshape(Bp, T, -1), gt[:mp].reshape(Bp, T, -1), pos_p,
                                        jnp.zeros((Bp, D_A), h.dtype), jnp.zeros((Bp, A_CONV - 1, D_A), h.dtype),
                                        None, None, None, *wts)
            past_cmp = cache_kv_cmp[i][page_table].reshape(Bs, -1, 2, KV_HEADS, HEAD_DIM)
            past_sel = cache_kv_sel[i][page_table].reshape(Bs, -1, 2, KV_HEADS, HEAD_DIM)
            mix_s, *ss = _ab_mixer_core(z[mp:].reshape(Bs, Ts, -1), gt[mp:].reshape(Bs, Ts, -1), pos_s,
                                        state_lru_h[i], state_lru_conv[i], past_cmp, past_sel, state_win_kv[i], *wts)
            for j in range(5):
                p_ab[j].append(sp[j])
                s_ab[j].append(ss[j])
            mix = jnp.concatenate([mix_p.reshape(mp, -1), mix_s.reshape(ms, -1)], axis=0).astype(bf16)
            h = _matmul_residual(mix, ab_w_out[i].astype(bf16), h, name="ab_out_proj")
        else:
            zc = _matmul(hn, c_w_in[i].astype(bf16), name="c_in_proj")
            b_g, c_g, v = jnp.split(zc, 3, axis=-1)
            cv = c_g * v
            yp, bp = _causal_dwconv(cv[:mp].reshape(Bp, T, D_C), jnp.zeros((Bp, C_CONV - 1, D_C), h.dtype), c_conv_w[i])
            ys, bs = _causal_dwconv(cv[mp:].reshape(Bs, Ts, D_C), state_sconv[i], c_conv_w[i])
            p_c.append(bp)
            s_c.append(bs)
            y = jnp.concatenate([yp.reshape(mp, D_C), ys.reshape(ms, D_C)], axis=0)
            h = _matmul_residual((b_g * y).astype(bf16), c_w_out[i].astype(bf16), h, name="c_out_proj")
        hf = _rmsnorm(h, norm_ffn[layer], bf16)
        hid = _ffn_up(hf, ffn_w_gate[layer].astype(bf16), ffn_w_up[layer].astype(bf16))
        h = _ffn_down(hid, ffn_w_down[layer].astype(bf16), h)
    y = _rmsnorm(h, norm_final, jnp.float32)
    y_prompt = y[:mp].reshape(Bp, T, D_MODEL)
    y_sample = y[mp:].reshape(Bs, Ts, D_MODEL)
    return (y_prompt, y_sample,
            jnp.stack(p_ab[0]), jnp.stack(p_ab[1]), jnp.stack(p_ab[2]), jnp.stack(p_ab[3]), jnp.stack(p_ab[4]), jnp.stack(p_c),
            jnp.stack(s_ab[0]), jnp.stack(s_ab[1]), jnp.stack(s_ab[2]), jnp.stack(s_ab[3]), jnp.stack(s_ab[4]), jnp.stack(s_c))
```

```python
import functools

import jax
import jax.numpy as jnp
import numpy as np
from jax import lax
from jax.experimental import pallas as pl
from jax.experimental.pallas import tpu as pltpu

D_MODEL = 4096
DEPTH = 4
PAGE_SIZE = 128
N_AB = (DEPTH + 1) // 2
N_C = DEPTH // 2
D_A = D_MODEL // 2
A_HEADS = 16
A_BLK = D_A // A_HEADS
A_CONV = 4
LRU_C = 8.0
B_HEADS = 16
HEAD_DIM = 128
KV_HEADS = 4
HPG = B_HEADS // KV_HEADS
D_B = B_HEADS * HEAD_DIM
D_KV = KV_HEADS * HEAD_DIM
CMP_LEN = 32
CMP_STRIDE = 16
SUB_PER_CMP = CMP_LEN // CMP_STRIDE
SEL_BLOCK = 64
SUB_PER_SEL = SEL_BLOCK // CMP_STRIDE
SEL_TOP = 16
WINDOW = 512
D_C = D_MODEL
C_CONV = 3
D_FF = -(-8 * D_MODEL // (3 * 256)) * 256
D_IN_MAIN = 2 * D_A + D_B + 6 * D_KV
N_GATES = 3 * B_HEADS
RMS_EPS = 1e-6
NEG_INF = -1e30
FORCE_SCORE = 1e4
QK_SCALE = HEAD_DIM ** -0.5

COL_XA = 0
COL_GA = D_A // 128
COL_Q = 2 * D_A // 128
COL_KVC = (2 * D_A + D_B) // 128
COL_KVS = COL_KVC + 2 * D_KV // 128
COL_KVW = COL_KVS + 2 * D_KV // 128

V7X_VMEM_LIMIT_BYTES = 56 * 1024 * 1024
LANE = 128
ATT_TQ = 128
ATT_CK = 512
F32 = jnp.float32
BF16 = jnp.bfloat16


def _row_tile(m):
    for t in range(1024, 15, -16):
        if m % t == 0:
            return t
    raise ValueError(m)


def _col_tile(n, cap=512):
    for t in range(cap, LANE - 1, -LANE):
        if n % t == 0:
            return t
    raise ValueError(n)


def _round_up(x, m):
    return -(-x // m) * m


def _params(sem):
    return pltpu.CompilerParams(dimension_semantics=sem, vmem_limit_bytes=V7X_VMEM_LIMIT_BYTES)


def _dot(a, b):
    return jnp.dot(a, b, preferred_element_type=F32)


def _dot_nt(a, b):
    return lax.dot_general(a, b, (((1,), (1,)), ((), ())), preferred_element_type=F32)


def _dot_f32_exact(x, w_bf16):
    hi = x.astype(BF16)
    r1 = x - hi.astype(F32)
    mid = r1.astype(BF16)
    lo = (r1 - mid.astype(F32)).astype(BF16)
    return _dot(hi, w_bf16) + (_dot(mid, w_bf16) + _dot(lo, w_bf16))


def _gelu_tanh(x):
    return 0.5 * x * (1.0 + jnp.tanh(0.7978845608028654 * (x + 0.044715 * (x * x * x))))


def _lanes(x128, w):
    return x128 if w == LANE else jnp.concatenate([x128] * (w // LANE), axis=1)


def _rmsnorm_kernel(x_ref, g_ref, o_ref):
    x = x_ref[...]
    y = x * lax.rsqrt(jnp.mean(x * x, axis=-1, keepdims=True) + RMS_EPS)
    o_ref[...] = (y * g_ref[...]).astype(o_ref.dtype)


def _rmsnorm(x, g, out_dtype):
    m, d = x.shape
    tm = _row_tile(m)
    return pl.pallas_call(
        _rmsnorm_kernel,
        out_shape=jax.ShapeDtypeStruct((m, d), out_dtype),
        grid=(m // tm,),
        in_specs=[pl.BlockSpec((tm, d), lambda i: (i, 0)), pl.BlockSpec((1, d), lambda i: (0, 0))],
        out_specs=pl.BlockSpec((tm, d), lambda i: (i, 0)),
        compiler_params=_params(("parallel",)),
        name="rmsnorm",
    )(x, g.reshape(1, d))


def _mm_kernel(x_ref, w_ref, o_ref):
    o_ref[...] = _dot(x_ref[...], w_ref[...]).astype(o_ref.dtype)


def _matmul(x, w, out_dtype=F32, name="matmul"):
    m, k = x.shape
    n = w.shape[1]
    tm, tn = _row_tile(m), _col_tile(n)
    return pl.pallas_call(
        _mm_kernel,
        out_shape=jax.ShapeDtypeStruct((m, n), out_dtype),
        grid=(m // tm, n // tn),
        in_specs=[pl.BlockSpec((tm, k), lambda i, j: (i, 0)), pl.BlockSpec((k, tn), lambda i, j: (0, j))],
        out_specs=pl.BlockSpec((tm, tn), lambda i, j: (i, j)),
        compiler_params=_params(("parallel", "arbitrary")),
        name=name,
    )(x, w)


def _mm_res_kernel(x_ref, w_ref, r_ref, o_ref):
    o_ref[...] = r_ref[...] + _dot(x_ref[...], w_ref[...])


def _matmul_residual(x, w, r, name="matmul_res"):
    m, k = x.shape
    n = w.shape[1]
    tm, tn = _row_tile(m), _col_tile(n)
    return pl.pallas_call(
        _mm_res_kernel,
        out_shape=jax.ShapeDtypeStruct((m, n), F32),
        grid=(m // tm, n // tn),
        in_specs=[pl.BlockSpec((tm, k), lambda i, j: (i, 0)), pl.BlockSpec((k, tn), lambda i, j: (0, j)),
                  pl.BlockSpec((tm, tn), lambda i, j: (i, j))],
        out_specs=pl.BlockSpec((tm, tn), lambda i, j: (i, j)),
        compiler_params=_params(("parallel", "arbitrary")),
        name=name,
    )(x, w, r)


def _mm2_res_kernel(x1_ref, x2_ref, w1_ref, w2_ref, r_ref, o_ref):
    o_ref[...] = r_ref[...] + (_dot(x1_ref[...], w1_ref[...]) + _dot(x2_ref[...], w2_ref[...]))


def _matmul2_residual(x1, x2, w, r, name="matmul2_res"):
    m, k1 = x1.shape
    k2 = x2.shape[1]
    n = w.shape[1]
    tm, tn = _row_tile(m), _col_tile(n)
    assert k1 % 8 == 0 and k1 == k2
    return pl.pallas_call(
        _mm2_res_kernel,
        out_shape=jax.ShapeDtypeStruct((m, n), F32),
        grid=(m // tm, n // tn),
        in_specs=[pl.BlockSpec((tm, k1), lambda i, j: (i, 0)), pl.BlockSpec((tm, k2), lambda i, j: (i, 0)),
                  pl.BlockSpec((k1, tn), lambda i, j: (0, j)), pl.BlockSpec((k2, tn), lambda i, j: (1, j)),
                  pl.BlockSpec((tm, tn), lambda i, j: (i, j))],
        out_specs=pl.BlockSpec((tm, tn), lambda i, j: (i, j)),
        compiler_params=_params(("parallel", "arbitrary")),
        name=name,
    )(x1, x2, w, w, r)


def _ffn_up_kernel(x_ref, wg_ref, wu_ref, o_ref):
    x = x_ref[...]
    g = _dot(x, wg_ref[...])
    u = _dot(x, wu_ref[...])
    o_ref[...] = (g * jax.nn.sigmoid(g) * u).astype(o_ref.dtype)


def _ffn_up(x, wg, wu):
    m, k = x.shape
    n = wg.shape[1]
    tm, tn = _row_tile(m), _col_tile(n)
    return pl.pallas_call(
        _ffn_up_kernel,
        out_shape=jax.ShapeDtypeStruct((m, n), BF16),
        grid=(m // tm, n // tn),
        in_specs=[pl.BlockSpec((tm, k), lambda i, j: (i, 0)), pl.BlockSpec((k, tn), lambda i, j: (0, j)),
                  pl.BlockSpec((k, tn), lambda i, j: (0, j))],
        out_specs=pl.BlockSpec((tm, tn), lambda i, j: (i, j)),
        compiler_params=_params(("parallel", "arbitrary")),
        name="ffn_up",
    )(x, wg, wu)


def _mm_acc_res_kernel(x_ref, w_ref, r_ref, o_ref, acc_ref):
    kk = pl.program_id(2)

    @pl.when(kk == 0)
    def _():
        acc_ref[...] = r_ref[...]

    acc_ref[...] += _dot(x_ref[...], w_ref[...])

    @pl.when(kk == pl.num_programs(2) - 1)
    def _():
        o_ref[...] = acc_ref[...]


def _ffn_down(h, wd, r):
    m, k = h.shape
    n = wd.shape[1]
    tm, tn = _row_tile(m), _col_tile(n)
    nk = 2
    tk = k // nk
    assert tk * nk == k and tk % LANE == 0
    return pl.pallas_call(
        _mm_acc_res_kernel,
        out_shape=jax.ShapeDtypeStruct((m, n), F32),
        grid=(m // tm, n // tn, nk),
        in_specs=[pl.BlockSpec((tm, tk), lambda i, j, kk: (i, kk)), pl.BlockSpec((tk, tn), lambda i, j, kk: (kk, j)),
                  pl.BlockSpec((tm, tn), lambda i, j, kk: (i, j))],
        out_specs=pl.BlockSpec((tm, tn), lambda i, j, kk: (i, j)),
        scratch_shapes=[pltpu.VMEM((tm, tn), F32)],
        compiler_params=_params(("parallel", "arbitrary", "arbitrary")),
        name="ffn_down",
    )(h, wd, r)


def _lru_kernel(xa_ref, ga_ref, h0_ref, buf_ref, cw_ref, cb_ref, wa_ref, ba_ref, wx_ref, bx_ref, lam_ref,
                y_ref, ht_ref, nbuf_ref):
    t_len = xa_ref.shape[0]
    x = xa_ref[...]
    buf = buf_ref[...]
    cw = cw_ref[...]
    row = lax.broadcasted_iota(jnp.int32, x.shape, 0)
    nb = A_CONV - 1
    xc = x * cw[nb:nb + 1]
    for s in range(1, A_CONV):
        xs = pltpu.roll(x, s, 0)
        for r in range(s):
            xs = jnp.where(row == r, buf[nb + r - s:nb + r - s + 1], xs)
        xc = xc + xs * cw[nb - s:nb - s + 1]
    xc = xc + cb_ref[...]
    xcb = xc.astype(BF16)
    r_g = jax.nn.sigmoid(_dot(xcb, wa_ref[...].astype(BF16)) + ba_ref[...])
    i_g = jax.nn.sigmoid(_dot(xcb, wx_ref[...].astype(BF16)) + bx_ref[...])
    nl = -lam_ref[...]
    softplus = jnp.maximum(nl, 0.0) + jnp.log1p(jnp.exp(-jnp.abs(nl)))
    log_a = -LRU_C * r_g * softplus
    a = jnp.exp(log_a)
    u = jnp.sqrt(-jnp.tanh(log_a) * (a * a + 1.0)) * i_g * xc
    d = 1
    while d < t_len:
        keep = row >= d
        a_prev = pltpu.roll(a, d, 0)
        u_prev = pltpu.roll(u, d, 0)
        u = jnp.where(keep, a * u_prev + u, u)
        a = jnp.where(keep, a * a_prev, a)
        d *= 2
    h = a * h0_ref[...] + u
    y_ref[...] = (h * _gelu_tanh(ga_ref[...])).astype(y_ref.dtype)
    ht_ref[...] = h[t_len - 1:t_len]
    nbuf_ref[...] = x[t_len - nb:t_len]


def _lru_mixer(z, row0, n_seq, t_len, h0, buf, conv_w, conv_b, wa, ba, wx, bx, lam, out_dtype):
    assert row0 % t_len == 0 and t_len >= A_CONV - 1 and t_len % 8 == 0
    rb0 = row0 // t_len
    nb = A_CONV - 1
    vec = lambda v: v.reshape(1, D_A)
    cspec = pl.BlockSpec((1, A_BLK), lambda s, h: (0, h))
    wspec = pl.BlockSpec((None, A_BLK, A_BLK), lambda s, h: (h, 0, 0))
    return pl.pallas_call(
        _lru_kernel,
        out_shape=(jax.ShapeDtypeStruct((n_seq * t_len, D_A), out_dtype),
                   jax.ShapeDtypeStruct((n_seq, 1, D_A), F32),
                   jax.ShapeDtypeStruct((n_seq, nb, D_A), F32)),
        grid=(n_seq, A_HEADS),
        in_specs=[pl.BlockSpec((t_len, A_BLK), lambda s, h: (rb0 + s, COL_XA + h)),
                  pl.BlockSpec((t_len, A_BLK), lambda s, h: (rb0 + s, COL_GA + h)),
                  pl.BlockSpec((None, 1, A_BLK), lambda s, h: (s, 0, h)),
                  pl.BlockSpec((None, nb, A_BLK), lambda s, h: (s, 0, h)),
                  pl.BlockSpec((A_CONV, A_BLK), lambda s, h: (0, h)),
                  cspec, wspec, cspec, wspec, cspec, cspec],
        out_specs=(pl.BlockSpec((t_len, A_BLK), lambda s, h: (s, h)),
                   pl.BlockSpec((None, 1, A_BLK), lambda s, h: (s, 0, h)),
                   pl.BlockSpec((None, nb, A_BLK), lambda s, h: (s, 0, h))),
        compiler_params=_params(("parallel", "parallel")),
        name="rglru_mixer",
    )(z, z, h0.reshape(n_seq, 1, D_A), buf, conv_w, vec(conv_b), wa, vec(ba), wx, vec(bx), vec(lam))


def _compress_kernel(x_ref, pe_ref, w1_ref, b1_ref, w2_ref, o_ref):
    n_sub = o_ref.shape[0]
    pe = pe_ref[...]
    y0 = jnp.zeros((n_sub, HEAD_DIM), F32)
    y1 = jnp.zeros((n_sub, HEAD_DIM), F32)
    for j in range(CMP_STRIDE):
        xj = x_ref[pl.ds(j, n_sub, stride=CMP_STRIDE), :]
        y0 = y0 + _dot((xj + pe[j:j + 1]).astype(BF16), w1_ref[j])
        y1 = y1 + _dot((xj + pe[CMP_STRIDE + j:CMP_STRIDE + j + 1]).astype(BF16), w1_ref[CMP_STRIDE + j])
    hid = y0 + pltpu.roll(y1, n_sub - 1, 0)
    hid = _gelu_tanh(hid + b1_ref[...])
    o_ref[...] = _dot(hid.astype(BF16), w2_ref[...]).astype(o_ref.dtype)


def _compress(x2d, row_block0, col_block0, n_seq, t_len, pe, w1, b1, w2):
    assert SUB_PER_CMP == 2 and t_len % (8 * CMP_STRIDE) == 0
    n_sub = t_len // CMP_STRIDE
    nt = 2 * KV_HEADS
    return pl.pallas_call(
        _compress_kernel,
        out_shape=jax.ShapeDtypeStruct((n_seq, nt, n_sub, HEAD_DIM), BF16),
        grid=(n_seq, nt),
        in_specs=[pl.BlockSpec((t_len, HEAD_DIM), lambda s, c: (row_block0 + s, col_block0 + c)),
                  pl.BlockSpec((None, CMP_LEN, HEAD_DIM), lambda s, c: (c // KV_HEADS, 0, 0)),
                  pl.BlockSpec((None, CMP_LEN, HEAD_DIM, HEAD_DIM), lambda s, c: (c // KV_HEADS, 0, 0, 0)),
                  pl.BlockSpec((None, 1, HEAD_DIM), lambda s, c: (c // KV_HEADS, 0, 0)),
                  pl.BlockSpec((None, HEAD_DIM, HEAD_DIM), lambda s, c: (c // KV_HEADS, 0, 0))],
        out_specs=pl.BlockSpec((None, None, n_sub, HEAD_DIM), lambda s, c: (s, c, 0, 0)),
        compiler_params=_params(("parallel", "arbitrary")),
        name="nsa_compress",
    )(x2d, pe, w1.astype(BF16), b1.reshape(2, 1, HEAD_DIM), w2.astype(BF16))


def _page_gather_kernel(pt_ref, c_ref, o_ref):
    o_ref[...] = c_ref[...]


def _page_gather(cache, layer, page_table):
    nb, n_pages = page_table.shape
    w = cache.shape[-1]
    return pl.pallas_call(
        _page_gather_kernel,
        out_shape=jax.ShapeDtypeStruct((nb * n_pages * PAGE_SIZE, w), cache.dtype),
        grid_spec=pltpu.PrefetchScalarGridSpec(
            num_scalar_prefetch=1, grid=(nb, n_pages),
            in_specs=[pl.BlockSpec((None, None, PAGE_SIZE, w), lambda b, p, pt: (layer, pt[b, p], 0, 0))],
            out_specs=pl.BlockSpec((PAGE_SIZE, w), lambda b, p, pt: (b * n_pages + p, 0))),
        compiler_params=_params(("parallel", "arbitrary")),
        name="page_gather",
    )(page_table, cache)


def _stack_heads(q, g0):
    return jnp.concatenate([q[:, (g0 + p) * HEAD_DIM:(g0 + p + 1) * HEAD_DIM] for p in range(HPG)], axis=0).astype(BF16)


def _masked_softmax(s, ok):
    sb = jnp.where(ok, s, NEG_INF)
    m = jnp.max(sb, axis=-1, keepdims=True)
    e = jnp.exp(sb - m)
    return e / jnp.sum(e, axis=-1, keepdims=True)


def _cmp_branch(q4, kc, vc, t_rows, slope, n_cmp):
    ncp = kc.shape[0]
    s = _dot_nt(q4, kc) * QK_SCALE
    col = lax.broadcasted_iota(jnp.int32, s.shape, 1)
    dist_i = t_rows - (col * CMP_STRIDE + (CMP_LEN - 1))
    ok = (dist_i >= 0) & (col < n_cmp)
    p = _masked_softmax(s - _lanes(slope, ncp) * dist_i.astype(F32), ok)
    p = jnp.where(ok, p, 0.0)
    return _dot(p.astype(BF16), vc), p


def _select_blocks(imp, pool, t_q, n_sel):
    imp_sel = _dot_f32_exact(imp, pool)
    jcol = lax.broadcasted_iota(jnp.int32, imp_sel.shape, 1)
    cur = lax.shift_right_logical(t_q, int(np.log2(SEL_BLOCK)))
    forced = (jcol == 0) | (jcol == cur) | (jcol == cur - 1)
    valid = jcol * SEL_BLOCK <= t_q
    score = jnp.where(forced, FORCE_SCORE, jnp.where(valid, imp_sel, -1.0))
    score = jnp.where(jcol < n_sel, score, -2.0)
    rank = jnp.zeros(score.shape, F32)
    for i in range(n_sel):
        ci = score[:, i:i + 1]
        before = (ci > score) | ((ci == score) & (jcol > i))
        rank = rank + jnp.where(before, 1.0, 0.0)
    top = min(SEL_TOP, n_sel)
    return jnp.where(rank < float(top), 1.0, 0.0)


def _pool_matrix(n_cmp, ncp, n_sel, nsp):
    assert n_cmp + SUB_PER_CMP - 1 <= n_sel * SUB_PER_SEL
    m = np.zeros((ncp, nsp), np.float32)
    for c in range(n_cmp):
        for n in range(SUB_PER_CMP):
            m[c, (c + n) // SUB_PER_SEL] += 1.0 / SUB_PER_CMP
    return jnp.asarray(m, BF16)


def _slope_rows(t_rep):
    h = np.arange(1, B_HEADS + 1, dtype=np.float64)
    s = np.exp2(-8.0 * h / B_HEADS).astype(np.float32).reshape(KV_HEADS, HPG, 1, 1)
    return jnp.asarray(np.broadcast_to(s, (KV_HEADS, HPG, t_rep, LANE)).reshape(KV_HEADS, HPG * t_rep, LANE))


def _expand_matrix(shape, blk0):
    j = lax.broadcasted_iota(jnp.int32, shape, 0)
    k = lax.broadcasted_iota(jnp.int32, shape, 1)
    return jnp.where(j == blk0 + lax.shift_right_logical(k, int(np.log2(SEL_BLOCK))), 1.0, 0.0).astype(BF16)


def _gate_combine(gs, p, tq, o_c, o_s, o_w):
    rows = slice(p * tq, (p + 1) * tq)
    return (gs[:, p:p + 1] * o_c[rows] + gs[:, HPG + p:HPG + p + 1] * o_s[rows]
            + gs[:, 2 * HPG + p:2 * HPG + p + 1] * o_w[rows])


def _attn_prompt_kernel(q_ref, kc_ref, vc_ref, ks_ref, vs_ref, kw_ref, vw_ref, gt_ref, slope_ref, pool_ref,
                        o_ref, ksb, vsb, kwb, vwb, *, n_cmp, n_sel):
    qi = pl.program_id(2)
    tq = q_ref.shape[0]
    t_total = ks_ref.shape[0]
    rr = HPG * tq

    @pl.when(qi == 0)
    def _():
        ksb[...] = ks_ref[...].astype(BF16)
        vsb[...] = vs_ref[...].astype(BF16)
        kwb[...] = kw_ref[...].astype(BF16)
        vwb[...] = vw_ref[...].astype(BF16)

    t0 = qi * tq
    q4 = _stack_heads(q_ref[...], 0)
    slope = slope_ref[...]

    def t_rows(width):
        r = lax.broadcasted_iota(jnp.int32, (rr, width), 0)
        return t0 + (r & (tq - 1))

    ncp = kc_ref.shape[0]
    o_c, p_c = _cmp_branch(q4, kc_ref[...], vc_ref[...], t_rows(ncp), slope, n_cmp)
    imp = p_c[0:tq]
    for p in range(1, HPG):
        imp = imp + p_c[p * tq:(p + 1) * tq]
    nsp = pool_ref.shape[1]
    t_q = t0 + lax.broadcasted_iota(jnp.int32, (tq, nsp), 0)
    sel = _select_blocks(imp, pool_ref[...], t_q, n_sel).astype(BF16)
    sel4 = jnp.concatenate([sel] * HPG, axis=0)

    ck = ATT_CK
    tr_ck = t_rows(ck)
    slope_ck = _lanes(slope, ck)
    col_ck = lax.broadcasted_iota(jnp.int32, (rr, ck), 1)

    def sel_chunk(c, carry):
        m, l, acc = carry
        k0 = pl.multiple_of(c * ck, ck)
        s = _dot_nt(q4, ksb[pl.ds(k0, ck), :]) * QK_SCALE
        dist_i = tr_ck - (k0 + col_ck)
        selx = _dot(sel4, _expand_matrix((nsp, ck), c * (ck // SEL_BLOCK)))
        ok = (dist_i >= 0) & (selx > 0.5)
        sb = jnp.where(ok, s - slope_ck * dist_i.astype(F32), NEG_INF)
        m_new = jnp.maximum(m, jnp.max(sb, axis=-1, keepdims=True))
        alpha = jnp.exp(m - m_new)
        e = jnp.exp(sb - m_new)
        l = alpha * l + jnp.sum(e, axis=-1, keepdims=True)
        acc = alpha * acc + _dot(e.astype(BF16), vsb[pl.ds(k0, ck), :])
        return m_new, l, acc

    n_chunks = (t0 + tq + ck - 1) // ck
    m0 = jnp.full((rr, 1), NEG_INF, F32)
    _, l_s, acc_s = lax.fori_loop(0, n_chunks, sel_chunk, (m0, jnp.zeros((rr, 1), F32), jnp.zeros((rr, HEAD_DIM), F32)))
    o_s = acc_s / l_s

    wk = WINDOW + tq
    w0 = pl.multiple_of(jnp.maximum(t0 - WINDOW, 0), tq)
    s = _dot_nt(q4, kwb[pl.ds(w0, wk), :]) * QK_SCALE
    dist_i = t_rows(wk) - (w0 + lax.broadcasted_iota(jnp.int32, (rr, wk), 1))
    ok = (dist_i >= 0) & (dist_i < WINDOW)
    p_w = _masked_softmax(s - _lanes(slope, wk) * dist_i.astype(F32), ok)
    o_w = _dot(p_w.astype(BF16), vwb[pl.ds(w0, wk), :])

    gs = jax.nn.sigmoid(gt_ref[...])
    for p in range(HPG):
        o_ref[:, p * HEAD_DIM:(p + 1) * HEAD_DIM] = _gate_combine(gs, p, tq, o_c, o_s, o_w).astype(o_ref.dtype)


def _attn_prompt(z, gt, kvc_cmp, n_seq, t_len):
    tq = ATT_TQ
    assert t_len % ATT_CK == 0 and t_len >= WINDOW + tq and tq & (tq - 1) == 0
    n_cmp = (t_len - CMP_LEN) // CMP_STRIDE + 1
    n_sel = -(-t_len // SEL_BLOCK)
    ncp = kvc_cmp.shape[2]
    nsp = _round_up(n_sel, LANE)
    nq = t_len // tq
    rr = HPG * tq
    kvspec = lambda col0: pl.BlockSpec((t_len, HEAD_DIM), lambda b, g, i: (b, col0 + g))
    return pl.pallas_call(
        functools.partial(_attn_prompt_kernel, n_cmp=n_cmp, n_sel=n_sel),
        out_shape=jax.ShapeDtypeStruct((n_seq * t_len, D_B), BF16),
        grid=(n_seq, KV_HEADS, nq),
        in_specs=[pl.BlockSpec((tq, HPG * HEAD_DIM), lambda b, g, i: (b * nq + i, COL_Q // HPG + g)),
                  pl.BlockSpec((None, None, ncp, HEAD_DIM), lambda b, g, i: (b, g, 0, 0)),
                  pl.BlockSpec((None, None, ncp, HEAD_DIM), lambda b, g, i: (b, KV_HEADS + g, 0, 0)),
                  kvspec(COL_KVS), kvspec(COL_KVS + KV_HEADS), kvspec(COL_KVW), kvspec(COL_KVW + KV_HEADS),
                  pl.BlockSpec((tq, LANE), lambda b, g, i: (b * nq + i, g)),
                  pl.BlockSpec((None, rr, LANE), lambda b, g, i: (g, 0, 0)),
                  pl.BlockSpec((ncp, nsp), lambda b, g, i: (0, 0))],
        out_specs=pl.BlockSpec((tq, HPG * HEAD_DIM), lambda b, g, i: (b * nq + i, g)),
        scratch_shapes=[pltpu.VMEM((t_len, HEAD_DIM), BF16)] * 4,
        compiler_params=_params(("parallel", "parallel", "arbitrary")),
        name="nsa_prompt_attention",
    )(z, kvc_cmp, kvc_cmp, z, z, z, z, gt, _slope_rows(tq), _pool_matrix(n_cmp, ncp, n_sel, nsp))


def _attn_dec_select_kernel(q_ref, kc_ref, vc_ref, slope_ref, pool_ref, oc_ref, sel_ref, *, n_cmp, n_sel, past_len):
    ts = q_ref.shape[0]
    rr = HPG * ts
    ncp = kc_ref.shape[0]
    q4 = _stack_heads(q_ref[...], 0)
    t_rows = past_len + (lax.broadcasted_iota(jnp.int32, (rr, ncp), 0) & (ts - 1))
    o_c, p_c = _cmp_branch(q4, kc_ref[...], vc_ref[...], t_rows, slope_ref[...], n_cmp)
    imp = p_c[0:ts]
    for p in range(1, HPG):
        imp = imp + p_c[p * ts:(p + 1) * ts]
    nsp = pool_ref.shape[1]
    t_q = past_len + lax.broadcasted_iota(jnp.int32, (ts, nsp), 0)
    oc_ref[...] = o_c
    sel_ref[...] = _select_blocks(imp, pool_ref[...], t_q, n_sel)


def _attn_dec_select(z, row0, kvc_cmp, n_seq, ts, past_len, n_cmp, n_sel):
    assert row0 % ts == 0 and ts & (ts - 1) == 0 and ts % 8 == 0
    rb0 = row0 // ts
    ncp = kvc_cmp.shape[2]
    nsp = _round_up(n_sel, LANE)
    rr = HPG * ts
    return pl.pallas_call(
        functools.partial(_attn_dec_select_kernel, n_cmp=n_cmp, n_sel=n_sel, past_len=past_len),
        out_shape=(jax.ShapeDtypeStruct((n_seq, KV_HEADS, rr, HEAD_DIM), F32),
                   jax.ShapeDtypeStruct((n_seq, KV_HEADS, ts, nsp), F32)),
        grid=(n_seq, KV_HEADS),
        in_specs=[pl.BlockSpec((ts, HPG * HEAD_DIM), lambda b, g: (rb0 + b, COL_Q // HPG + g)),
                  pl.BlockSpec((None, None, ncp, HEAD_DIM), lambda b, g: (b, g, 0, 0)),
                  pl.BlockSpec((None, None, ncp, HEAD_DIM), lambda b, g: (b, KV_HEADS + g, 0, 0)),
                  pl.BlockSpec((None, rr, LANE), lambda b, g: (g, 0, 0)),
                  pl.BlockSpec((ncp, nsp), lambda b, g: (0, 0))],
        out_specs=(pl.BlockSpec((None, None, rr, HEAD_DIM), lambda b, g: (b, g, 0, 0)),
                   pl.BlockSpec((None, None, ts, nsp), lambda b, g: (b, g, 0, 0))),
        compiler_params=_params(("parallel", "parallel")),
        name="nsa_decode_select",
    )(z, kvc_cmp, kvc_cmp, _slope_rows(ts), _pool_matrix(n_cmp, ncp, n_sel, nsp))


def _attn_dec_kernel(pt_ref, q_ref, sel_ref, page_ref, kvs_ref, kvw_ref, win_ref, oc_ref, gt_ref, slope_ref,
                     o_ref, nwin_ref, m_sc, l_sc, acc_sc, *, past_len, n_pages):
    pg = pl.program_id(1)
    ts = q_ref.shape[0]
    rr = HPG * ts
    nsp = sel_ref.shape[-1]
    pad_rows = LANE - ts
    row = lax.broadcasted_iota(jnp.int32, (rr, LANE), 0)
    col = lax.broadcasted_iota(jnp.int32, (rr, LANE), 1)
    t_rows = past_len + (row & (ts - 1))

    @pl.when(pg == 0)
    def _():
        m_sc[...] = jnp.full(m_sc.shape, NEG_INF, F32)
        l_sc[...] = jnp.zeros(l_sc.shape, F32)
        acc_sc[...] = jnp.zeros(acc_sc.shape, F32)

    def sel_update(g, q4, kk, vv, k0, expand, extra_ok):
        s = _dot_nt(q4, kk) * QK_SCALE
        dist_i = t_rows - (k0 + col)
        sel4 = jnp.concatenate([sel_ref[g].astype(BF16)] * HPG, axis=0)
        ok = (dist_i >= 0) & (_dot(sel4, expand) > 0.5) & extra_ok
        sb = jnp.where(ok, s - slope_ref[g] * dist_i.astype(F32), NEG_INF)
        m_old = m_sc[g]
        m_new = jnp.maximum(m_old, jnp.max(sb, axis=-1, keepdims=True))
        alpha = jnp.exp(m_old - m_new)
        e = jnp.exp(sb - m_new)
        l_sc[g] = alpha * l_sc[g] + jnp.sum(e, axis=-1, keepdims=True)
        acc_sc[g] = alpha * acc_sc[g] + _dot(e.astype(BF16), vv)
        m_sc[g] = m_new

    @pl.when(pg < n_pages)
    def _():
        k0 = pg * PAGE_SIZE
        expand = _expand_matrix((nsp, PAGE_SIZE), pg * (PAGE_SIZE // SEL_BLOCK))
        for g in range(KV_HEADS):
            q4 = _stack_heads(q_ref[...], g * HPG)
            kk = page_ref[:, g * HEAD_DIM:(g + 1) * HEAD_DIM].astype(BF16)
            vv = page_ref[:, D_KV + g * HEAD_DIM:D_KV + (g + 1) * HEAD_DIM].astype(BF16)
            sel_update(g, q4, kk, vv, k0, expand, True)

    @pl.when(pg == n_pages)
    def _():
        zpad = jnp.zeros((pad_rows, HEAD_DIM), F32)
        gs = jax.nn.sigmoid(gt_ref[...])
        jj = lax.broadcasted_iota(jnp.int32, (nsp, LANE), 0)
        kk_i = lax.broadcasted_iota(jnp.int32, (nsp, LANE), 1)
        new_blk = lax.shift_right_logical(past_len + kk_i, int(np.log2(SEL_BLOCK)))
        expand_new = jnp.where(jj == new_blk, 1.0, 0.0).astype(BF16)
        is_new = col < ts
        wlen = win_ref.shape[0]
        wrow = lax.broadcasted_iota(jnp.int32, (rr, wlen), 0)
        wcol = lax.broadcasted_iota(jnp.int32, (rr, wlen), 1)
        dist_w = (past_len + (wrow & (ts - 1))) - (past_len - wlen + wcol)
        ok_w = (dist_w >= 0) & (dist_w < WINDOW)
        dist_n = t_rows - (past_len + col)
        ok_n = (dist_n >= 0) & (dist_n < WINDOW) & is_new
        for g in range(KV_HEADS):
            q4 = _stack_heads(q_ref[...], g * HPG)
            ksl = slice(g * HEAD_DIM, (g + 1) * HEAD_DIM)
            vsl = slice(D_KV + g * HEAD_DIM, D_KV + (g + 1) * HEAD_DIM)
            kn = jnp.concatenate([kvs_ref[:, ksl], zpad], axis=0).astype(BF16)
            vn = jnp.concatenate([kvs_ref[:, vsl], zpad], axis=0).astype(BF16)
            sel_update(g, q4, kn, vn, past_len, expand_new, is_new)
            o_s = acc_sc[g] / l_sc[g]
            slope = slope_ref[g]
            s1 = _dot_nt(q4, win_ref[:, ksl].astype(BF16)) * QK_SCALE
            s1 = jnp.where(ok_w, s1 - _lanes(slope, wlen) * dist_w.astype(F32), NEG_INF)
            kwn = jnp.concatenate([kvw_ref[:, ksl], zpad], axis=0).astype(BF16)
            vwn = jnp.concatenate([kvw_ref[:, vsl], zpad], axis=0).astype(BF16)
            s2 = _dot_nt(q4, kwn) * QK_SCALE
            s2 = jnp.where(ok_n, s2 - slope * dist_n.astype(F32), NEG_INF)
            m = jnp.maximum(jnp.max(s1, axis=-1, keepdims=True), jnp.max(s2, axis=-1, keepdims=True))
            e1 = jnp.exp(s1 - m)
            e2 = jnp.exp(s2 - m)
            inv = 1.0 / (jnp.sum(e1, axis=-1, keepdims=True) + jnp.sum(e2, axis=-1, keepdims=True))
            o_w = _dot((e1 * inv).astype(BF16), win_ref[:, vsl].astype(BF16)) + _dot((e2 * inv).astype(BF16), vwn)
            o_c = oc_ref[g]
            for p in range(HPG):
                c0 = (g * HPG + p) * HEAD_DIM
                o_ref[:, c0:c0 + HEAD_DIM] = _gate_combine(gs[:, g * LANE:(g + 1) * LANE], p, ts, o_c, o_s, o_w)
        nwin_ref[0:wlen - ts, :] = win_ref[ts:wlen, :]
        nwin_ref[wlen - ts:wlen, :] = kvw_ref[...]


def _attn_decode(z, gt, row0, n_seq, ts, page_table, cache_sel, layer, win_state, o_c, sel, past_len):
    n_pages = page_table.shape[1]
    rb0 = row0 // ts
    rr = HPG * ts
    nsp = sel.shape[-1]
    wlen = win_state.shape[2]
    w = 2 * D_KV
    assert wlen == WINDOW and past_len % SEL_BLOCK == 0
    last = n_pages - 1
    return pl.pallas_call(
        functools.partial(_attn_dec_kernel, past_len=past_len, n_pages=n_pages),
        out_shape=(jax.ShapeDtypeStruct((n_seq * ts, D_B), F32),
                   jax.ShapeDtypeStruct((n_seq, wlen, w), F32)),
        grid_spec=pltpu.PrefetchScalarGridSpec(
            num_scalar_prefetch=1, grid=(n_seq, n_pages + 1),
            in_specs=[pl.BlockSpec((ts, D_B), lambda b, p, pt: (rb0 + b, COL_Q * LANE // D_B)),
                      pl.BlockSpec((None, KV_HEADS, ts, nsp), lambda b, p, pt: (b, 0, 0, 0)),
                      pl.BlockSpec((None, None, PAGE_SIZE, w), lambda b, p, pt: (layer, pt[b, jnp.minimum(p, last)], 0, 0)),
                      pl.BlockSpec((ts, w), lambda b, p, pt: (rb0 + b, COL_KVS * LANE // w)),
                      pl.BlockSpec((ts, w), lambda b, p, pt: (rb0 + b, COL_KVW * LANE // w)),
                      pl.BlockSpec((None, None, wlen, w), lambda b, p, pt: (layer, b, 0, 0)),
                      pl.BlockSpec((None, KV_HEADS, rr, HEAD_DIM), lambda b, p, pt: (b, 0, 0, 0)),
                      pl.BlockSpec((ts, KV_HEADS * LANE), lambda b, p, pt: (rb0 + b, 0)),
                      pl.BlockSpec((KV_HEADS, rr, LANE), lambda b, p, pt: (0, 0, 0))],
            out_specs=(pl.BlockSpec((ts, D_B), lambda b, p, pt: (b, 0)),
                       pl.BlockSpec((None, wlen, w), lambda b, p, pt: (b, 0, 0))),
            scratch_shapes=[pltpu.VMEM((KV_HEADS, rr, 1), F32), pltpu.VMEM((KV_HEADS, rr, 1), F32),
                            pltpu.VMEM((KV_HEADS, rr, HEAD_DIM), F32)]),
        compiler_params=_params(("parallel", "arbitrary")),
        name="nsa_decode_attention",
    )(page_table, z, sel, cache_sel, z, z, win_state, o_c, gt, _slope_rows(ts))


def _sconv_kernel(bg_ref, cg_ref, v_ref, buf_ref, cw_ref, y_ref, nbuf_ref):
    t_len = cg_ref.shape[0]
    nb = C_CONV - 1
    cv = cg_ref[...] * v_ref[...]
    buf = buf_ref[...]
    cw = cw_ref[...]
    row = lax.broadcasted_iota(jnp.int32, cv.shape, 0)
    y = cv * cw[nb:nb + 1]
    for s in range(1, C_CONV):
        xs = pltpu.roll(cv, s, 0)
        for r in range(s):
            xs = jnp.where(row == r, buf[nb + r - s:nb + r - s + 1], xs)
        y = y + xs * cw[nb - s:nb - s + 1]
    y_ref[...] = (bg_ref[...] * y).astype(y_ref.dtype)
    nbuf_ref[...] = cv[t_len - nb:t_len]


def _sconv_mixer(zc, row0, n_seq, t_len, buf, conv_w, out_dtype):
    assert row0 % t_len == 0 and t_len >= C_CONV - 1 and t_len % 8 == 0
    rb0 = row0 // t_len
    nb = C_CONV - 1
    tc = 256
    nc = D_C // tc
    zspec = lambda third: pl.BlockSpec((t_len, tc), lambda s, c: (rb0 + s, third * nc + c))
    return pl.pallas_call(
        _sconv_kernel,
        out_shape=(jax.ShapeDtypeStruct((n_seq * t_len, D_C), out_dtype),
                   jax.ShapeDtypeStruct((n_seq, nb, D_C), F32)),
        grid=(n_seq, nc),
        in_specs=[zspec(0), zspec(1), zspec(2),
                  pl.BlockSpec((None, nb, tc), lambda s, c: (s, 0, c)),
                  pl.BlockSpec((C_CONV, tc), lambda s, c: (0, c))],
        out_specs=(pl.BlockSpec((t_len, tc), lambda s, c: (s, c)),
                   pl.BlockSpec((None, nb, tc), lambda s, c: (s, 0, c))),
        compiler_params=_params(("parallel", "parallel")),
        name="sconv_mixer",
    )(zc, zc, zc, buf, conv_w)


def _gate_weight(w_in):
    k = w_in.shape[0]
    wg = w_in[:, D_IN_MAIN:].reshape(k, 3, KV_HEADS, HPG).transpose(0, 2, 1, 3).reshape(k, KV_HEADS, 3 * HPG)
    return jnp.pad(wg, ((0, 0), (0, 0), (0, LANE - 3 * HPG))).reshape(k, KV_HEADS * LANE)


def _ab_layer(h, hn, i, mp, bp, t_len, bs, ts, past_len, w, cache_kv_cmp, cache_kv_sel, state_win_kv, state_lru_h,
              state_lru_conv, page_table):
    w_in = w["ab_w_in"][i].astype(BF16)
    z = _matmul(hn, w_in[:, :D_IN_MAIN], name="ab_in_proj")
    gt = _matmul(hn, _gate_weight(w_in), name="ab_gate_proj")
    lru_w = (w["ab_conv_w"][i], w["ab_conv_b"][i], w["ab_gate_a_w"][i], w["ab_gate_a_b"][i], w["ab_gate_x_w"][i],
             w["ab_gate_x_b"][i], w["ab_lru_lambda"][i])
    cmp_w = (w["ab_cmp_pe"][i], w["ab_cmp_w1"][i], w["ab_cmp_b1"][i], w["ab_cmp_w2"][i])
    n_pages = page_table.shape[1]
    tk = past_len + ts
    n_cmp_s = (tk - CMP_LEN) // CMP_STRIDE + 1
    n_sel_s = -(-tk // SEL_BLOCK)
    assert n_cmp_s == (past_len - CMP_LEN) // CMP_STRIDE + 1

    ya_p, hT_p, nbuf_p = _lru_mixer(z, 0, bp, t_len, jnp.zeros((bp, D_A), F32), jnp.zeros((bp, A_CONV - 1, D_A), F32),
                                    *lru_w, out_dtype=BF16)
    kvc_p = _compress(z, 0, COL_KVC, bp, t_len, *cmp_w)
    ob_p = _attn_prompt(z, gt, kvc_p, bp, t_len)

    ya_s, hT_s, nbuf_s = _lru_mixer(z, mp, bs, ts, state_lru_h[i], state_lru_conv[i], *lru_w, out_dtype=F32)
    past_cmp = _page_gather(cache_kv_cmp.reshape(N_AB, -1, PAGE_SIZE, 2 * D_KV), i, page_table)
    kvc_s = _compress(past_cmp, 0, 0, bs, past_len, *cmp_w)
    oc_s, sel_s = _attn_dec_select(z, mp, kvc_s, bs, ts, past_len, n_cmp_s, n_sel_s)
    ob_s, nwin_s = _attn_decode(z, gt, mp, bs, ts, page_table, cache_kv_sel.reshape(N_AB, -1, PAGE_SIZE, 2 * D_KV), i,
                                state_win_kv.reshape(N_AB, bs, -1, 2 * D_KV), oc_s, sel_s, past_len)

    ya = jnp.concatenate([ya_p, ya_s.astype(BF16)], axis=0)
    ob = jnp.concatenate([ob_p, ob_s.astype(BF16)], axis=0)
    h = _matmul2_residual(ya, ob, w["ab_w_out"][i].astype(BF16), h, name="ab_out_proj")

    kv5 = lambda a, nb_, tt: a.reshape(nb_, tt, 2, KV_HEADS, HEAD_DIM)
    zp = z[:mp].reshape(bp, t_len, -1)
    zs = z[mp:].reshape(bs, ts, -1)
    seg = lambda a, c0: a[:, :, c0 * LANE:c0 * LANE + 2 * D_KV]
    wl = min(WINDOW, t_len)
    outs_p = (kv5(seg(zp, COL_KVC), bp, t_len), kv5(seg(zp, COL_KVS), bp, t_len),
              kv5(seg(zp, COL_KVW)[:, t_len - wl:], bp, wl), hT_p.reshape(bp, D_A), nbuf_p)
    outs_s = (kv5(seg(zs, COL_KVC), bs, ts), kv5(seg(zs, COL_KVS), bs, ts),
              kv5(nwin_s, bs, nwin_s.shape[1]), hT_s.reshape(bs, D_A), nbuf_s)
    return h, outs_p, outs_s


def _c_layer(h, hn, i, mp, bp, t_len, bs, ts, w, state_sconv):
    zc = _matmul(hn, w["c_w_in"][i].astype(BF16), name="c_in_proj")
    y_p, nbuf_p = _sconv_mixer(zc, 0, bp, t_len, jnp.zeros((bp, C_CONV - 1, D_C), F32), w["c_conv_w"][i], BF16)
    y_s, nbuf_s = _sconv_mixer(zc, mp, bs, ts, state_sconv[i], w["c_conv_w"][i], F32)
    y = jnp.concatenate([y_p, y_s.astype(BF16)], axis=0)
    h = _matmul_residual(y, w["c_w_out"][i].astype(BF16), h, name="c_out_proj")
    return h, nbuf_p, nbuf_s


def kernel(x_prompt, x_sample, cache_kv_cmp, cache_kv_sel, state_win_kv, state_lru_h, state_lru_conv, state_sconv,
           page_table, norm_mix, norm_ffn, norm_final, ab_w_in, ab_conv_w, ab_conv_b, ab_gate_a_w, ab_gate_a_b,
           ab_gate_x_w, ab_gate_x_b, ab_lru_lambda, ab_cmp_pe, ab_cmp_w1, ab_cmp_b1, ab_cmp_w2, ab_w_out,
           c_w_in, c_conv_w, c_w_out, ffn_w_gate, ffn_w_up, ffn_w_down):
    w = dict(ab_w_in=ab_w_in, ab_conv_w=ab_conv_w, ab_conv_b=ab_conv_b, ab_gate_a_w=ab_gate_a_w,
             ab_gate_a_b=ab_gate_a_b, ab_gate_x_w=ab_gate_x_w, ab_gate_x_b=ab_gate_x_b, ab_lru_lambda=ab_lru_lambda,
             ab_cmp_pe=ab_cmp_pe, ab_cmp_w1=ab_cmp_w1, ab_cmp_b1=ab_cmp_b1, ab_cmp_w2=ab_cmp_w2, ab_w_out=ab_w_out,
             c_w_in=c_w_in, c_conv_w=c_conv_w, c_w_out=c_w_out)
    bp, t_len = x_prompt.shape[:2]
    bs, ts = x_sample.shape[:2]
    mp, ms = bp * t_len, bs * ts
    past_len = page_table.shape[1] * PAGE_SIZE
    h = jnp.concatenate([x_prompt.reshape(mp, D_MODEL), x_sample.reshape(ms, D_MODEL)], axis=0)
    p_ab, s_ab, p_c, s_c = [], [], [], []
    for layer in range(DEPTH):
        hn = _rmsnorm(h, norm_mix[layer], BF16)
        i = layer // 2
        if layer % 2 == 0:
            h, op, os_ = _ab_layer(h, hn, i, mp, bp, t_len, bs, ts, past_len, w, cache_kv_cmp, cache_kv_sel,
                                   state_win_kv, state_lru_h, state_lru_conv, page_table)
            p_ab.append(op)
            s_ab.append(os_)
        else:
            h, nbp, nbs = _c_layer(h, hn, i, mp, bp, t_len, bs, ts, w, state_sconv)
            p_c.append(nbp)
            s_c.append(nbs)
        hf = _rmsnorm(h, norm_ffn[layer], BF16)
        hid = _ffn_up(hf, ffn_w_gate[layer].astype(BF16), ffn_w_up[layer].astype(BF16))
        h = _ffn_down(hid, ffn_w_down[layer].astype(BF16), h)
    y = _rmsnorm(h, norm_final, F32)
    y_prompt = y[:mp].reshape(bp, t_len, D_MODEL)
    y_sample = y[mp:].reshape(bs, ts, D_MODEL)
    stack = lambda rows, j: jnp.stack([r[j] for r in rows])
    return (y_prompt, y_sample,
            stack(p_ab, 0), stack(p_ab, 1), stack(p_ab, 2), stack(p_ab, 3), stack(p_ab, 4), jnp.stack(p_c),
            stack(s_ab, 0), stack(s_ab, 1), stack(s_ab, 2), stack(s_ab, 3), stack(s_ab, 4), jnp.stack(s_c))
```

```python
import functools

import jax
import jax.numpy as jnp
import numpy as np
from jax import lax
from jax.experimental import pallas as pl
from jax.experimental.pallas import tpu as pltpu

D_MODEL = 4096
DEPTH = 4
PAGE_SIZE = 128
N_AB = (DEPTH + 1) // 2
N_C = DEPTH // 2
D_A = D_MODEL // 2
A_HEADS = 16
A_BLK = D_A // A_HEADS
A_CONV = 4
LRU_C = 8.0
B_HEADS = 16
HEAD_DIM = 128
KV_HEADS = 4
HPG = B_HEADS // KV_HEADS
D_B = B_HEADS * HEAD_DIM
D_KV = KV_HEADS * HEAD_DIM
CMP_LEN = 32
CMP_STRIDE = 16
SUB_PER_CMP = CMP_LEN // CMP_STRIDE
SEL_BLOCK = 64
SUB_PER_SEL = SEL_BLOCK // CMP_STRIDE
SEL_TOP = 16
WINDOW = 512
D_C = D_MODEL
C_CONV = 3
D_FF = -(-8 * D_MODEL // (3 * 256)) * 256
D_IN_MAIN = 2 * D_A + D_B + 6 * D_KV
N_GATES = 3 * B_HEADS
RMS_EPS = 1e-6
NEG_INF = -1e30
FORCE_SCORE = 1e4
QK_SCALE = HEAD_DIM ** -0.5

COL_XA = 0
COL_GA = D_A // 128
COL_Q = 2 * D_A // 128
COL_KVC = (2 * D_A + D_B) // 128
COL_KVS = COL_KVC + 2 * D_KV // 128
COL_KVW = COL_KVS + 2 * D_KV // 128

V7X_VMEM_LIMIT_BYTES = 56 * 1024 * 1024
LANE = 128
ATT_TQ = 128
ATT_CK = 512
F32 = jnp.float32
BF16 = jnp.bfloat16


def _row_tile(m, cap=1024):
    for t in range(cap - cap % 16, 15, -16):
        if m % t == 0:
            return t
    raise ValueError(m)


def _col_tile(n, cap=512):
    for t in range(cap, LANE - 1, -LANE):
        if n % t == 0:
            return t
    raise ValueError(n)


def _round_up(x, m):
    return -(-x // m) * m


def _params(sem):
    return pltpu.CompilerParams(dimension_semantics=sem, vmem_limit_bytes=V7X_VMEM_LIMIT_BYTES)


def _dot(a, b):
    return jnp.dot(a, b, preferred_element_type=F32)


def _dot_nt(a, b):
    return lax.dot_general(a, b, (((1,), (1,)), ((), ())), preferred_element_type=F32)


def _dot_f32_exact(x, w_bf16):
    hi = x.astype(BF16)
    r1 = x - hi.astype(F32)
    mid = r1.astype(BF16)
    lo = (r1 - mid.astype(F32)).astype(BF16)
    return _dot(hi, w_bf16) + (_dot(mid, w_bf16) + _dot(lo, w_bf16))


def _gelu_tanh(x):
    return 0.5 * x * (1.0 + jnp.tanh(0.7978845608028654 * (x + 0.044715 * (x * x * x))))


def _lanes(x128, w):
    return x128 if w == LANE else jnp.concatenate([x128] * (w // LANE), axis=1)


def _rmsnorm_kernel(x_ref, g_ref, o_ref):
    x = x_ref[...]
    y = x * lax.rsqrt(jnp.mean(x * x, axis=-1, keepdims=True) + RMS_EPS)
    o_ref[...] = (y * g_ref[...]).astype(o_ref.dtype)


def _rmsnorm(x, g, out_dtype):
    m, d = x.shape
    tm = _row_tile(m)
    return pl.pallas_call(
        _rmsnorm_kernel,
        out_shape=jax.ShapeDtypeStruct((m, d), out_dtype),
        grid=(m // tm,),
        in_specs=[pl.BlockSpec((tm, d), lambda i: (i, 0)), pl.BlockSpec((1, d), lambda i: (0, 0))],
        out_specs=pl.BlockSpec((tm, d), lambda i: (i, 0)),
        compiler_params=_params(("parallel",)),
        name="rmsnorm",
    )(x, g.reshape(1, d))


def _mm_kernel(x_ref, w_ref, o_ref):
    o_ref[...] = _dot(x_ref[...], w_ref[...]).astype(o_ref.dtype)


def _matmul(x, w, out_dtype=F32, name="matmul", row_cap=1024):
    m, k = x.shape
    n = w.shape[1]
    tm, tn = _row_tile(m, row_cap), _col_tile(n)
    return pl.pallas_call(
        _mm_kernel,
        out_shape=jax.ShapeDtypeStruct((m, n), out_dtype),
        grid=(m // tm, n // tn),
        in_specs=[pl.BlockSpec((tm, k), lambda i, j: (i, 0)), pl.BlockSpec((k, tn), lambda i, j: (0, j))],
        out_specs=pl.BlockSpec((tm, tn), lambda i, j: (i, j)),
        compiler_params=_params(("parallel", "arbitrary")),
        name=name,
    )(x, w)


def _in_proj_kernel(x_ref, w_ref, z_ref, kv_ref, *, j_kv0, tiles_per_seg):
    j = pl.program_id(1)
    tm, tn = z_ref.shape
    heads_per_tile = tn // HEAD_DIM
    rows_per_token = 2 * KV_HEADS
    acc = _dot(x_ref[...], w_ref[...])
    z_ref[...] = acc
    for part in range(tiles_per_seg):
        @pl.when((j >= j_kv0) & ((j - j_kv0) % tiles_per_seg == part))
        def _():
            for c in range(heads_per_tile):
                kv_ref[pl.ds(part * heads_per_tile + c, tm, stride=rows_per_token), :] = (
                    acc[:, c * HEAD_DIM:(c + 1) * HEAD_DIM])


def _in_proj(x, w):
    m, k = x.shape
    n = w.shape[1]
    tm, tn = _row_tile(m), _col_tile(n)
    seg = 2 * D_KV
    assert seg % tn == 0 and (COL_KVC * LANE) % tn == 0 and n == COL_KVC * LANE + 3 * seg
    tiles_per_seg = seg // tn
    j_kv0 = COL_KVC * LANE // tn
    rpt = 2 * KV_HEADS
    return pl.pallas_call(
        functools.partial(_in_proj_kernel, j_kv0=j_kv0, tiles_per_seg=tiles_per_seg),
        out_shape=(jax.ShapeDtypeStruct((m, n), F32), jax.ShapeDtypeStruct((3, m * rpt, HEAD_DIM), F32)),
        grid=(m // tm, n // tn),
        in_specs=[pl.BlockSpec((tm, k), lambda i, j: (i, 0)), pl.BlockSpec((k, tn), lambda i, j: (0, j))],
        out_specs=(pl.BlockSpec((tm, tn), lambda i, j: (i, j)),
                   pl.BlockSpec((None, tm * rpt, HEAD_DIM),
                                lambda i, j: (jnp.maximum(j - j_kv0, 0) // tiles_per_seg, i, 0))),
        compiler_params=_params(("parallel", "arbitrary")),
        name="ab_in_proj",
    )(x, w)


def _mm_res_kernel(x_ref, w_ref, r_ref, o_ref):
    o_ref[...] = r_ref[...] + _dot(x_ref[...], w_ref[...])


def _matmul_residual(x, w, r, name="matmul_res"):
    m, k = x.shape
    n = w.shape[1]
    tm, tn = _row_tile(m), _col_tile(n)
    return pl.pallas_call(
        _mm_res_kernel,
        out_shape=jax.ShapeDtypeStruct((m, n), F32),
        grid=(m // tm, n // tn),
        in_specs=[pl.BlockSpec((tm, k), lambda i, j: (i, 0)), pl.BlockSpec((k, tn), lambda i, j: (0, j)),
                  pl.BlockSpec((tm, tn), lambda i, j: (i, j))],
        out_specs=pl.BlockSpec((tm, tn), lambda i, j: (i, j)),
        compiler_params=_params(("parallel", "arbitrary")),
        name=name,
    )(x, w, r)


def _mm2_res_kernel(x1_ref, x2_ref, w1_ref, w2_ref, r_ref, o_ref):
    o_ref[...] = r_ref[...] + (_dot(x1_ref[...], w1_ref[...]) + _dot(x2_ref[...], w2_ref[...]))


def _matmul2_residual(x1, x2, w, r, name="matmul2_res"):
    m, k1 = x1.shape
    k2 = x2.shape[1]
    n = w.shape[1]
    tm, tn = _row_tile(m), _col_tile(n)
    assert k1 % 8 == 0 and k1 == k2
    return pl.pallas_call(
        _mm2_res_kernel,
        out_shape=jax.ShapeDtypeStruct((m, n), F32),
        grid=(m // tm, n // tn),
        in_specs=[pl.BlockSpec((tm, k1), lambda i, j: (i, 0)), pl.BlockSpec((tm, k2), lambda i, j: (i, 0)),
                  pl.BlockSpec((k1, tn), lambda i, j: (0, j)), pl.BlockSpec((k2, tn), lambda i, j: (1, j)),
                  pl.BlockSpec((tm, tn), lambda i, j: (i, j))],
        out_specs=pl.BlockSpec((tm, tn), lambda i, j: (i, j)),
        compiler_params=_params(("parallel", "arbitrary")),
        name=name,
    )(x1, x2, w, w, r)


def _ffn_up_kernel(x_ref, wg_ref, wu_ref, o_ref):
    x = x_ref[...]
    g = _dot(x, wg_ref[...])
    u = _dot(x, wu_ref[...])
    o_ref[...] = (g * jax.nn.sigmoid(g) * u).astype(o_ref.dtype)


FFN_UP_ROW_CAP = 1536


def _ffn_up(x, wg, wu):
    m, k = x.shape
    n = wg.shape[1]
    tm, tn = _row_tile(m, FFN_UP_ROW_CAP), _col_tile(n)
    return pl.pallas_call(
        _ffn_up_kernel,
        out_shape=jax.ShapeDtypeStruct((m, n), BF16),
        grid=(m // tm, n // tn),
        in_specs=[pl.BlockSpec((tm, k), lambda i, j: (i, 0)), pl.BlockSpec((k, tn), lambda i, j: (0, j)),
                  pl.BlockSpec((k, tn), lambda i, j: (0, j))],
        out_specs=pl.BlockSpec((tm, tn), lambda i, j: (i, j)),
        compiler_params=_params(("parallel", "arbitrary")),
        name="ffn_up",
    )(x, wg, wu)


def _ffn_down_kernel(x_ref, w_ref, r_ref, o_ref, acc_ref):
    kk = pl.program_id(1)
    j = pl.program_id(2)
    part = _dot(x_ref[...], w_ref[...])

    @pl.when(kk == 0)
    def _():
        acc_ref[j] = r_ref[...] + part

    @pl.when(kk == pl.num_programs(1) - 1)
    def _():
        o_ref[...] = acc_ref[j] + part


def _ffn_down(h, wd, r):
    m, k = h.shape
    n = wd.shape[1]
    tm, tn = _row_tile(m), _col_tile(n)
    nk = 2
    tk = k // nk
    assert tk * nk == k and tk % LANE == 0
    return pl.pallas_call(
        _ffn_down_kernel,
        out_shape=jax.ShapeDtypeStruct((m, n), F32),
        grid=(m // tm, nk, n // tn),
        in_specs=[pl.BlockSpec((tm, tk), lambda i, kk, j: (i, kk)), pl.BlockSpec((tk, tn), lambda i, kk, j: (kk, j)),
                  pl.BlockSpec((tm, tn), lambda i, kk, j: (i, jnp.where(kk == 0, j, 0)))],
        out_specs=pl.BlockSpec((tm, tn), lambda i, kk, j: (i, jnp.where(kk == nk - 1, j, 0))),
        scratch_shapes=[pltpu.VMEM((n // tn, tm, tn), F32)],
        compiler_params=_params(("parallel", "arbitrary", "arbitrary")),
        name="ffn_down",
    )(h, wd, r)


def _lru_kernel(xa_ref, ga_ref, h0_ref, buf_ref, cw_ref, cb_ref, wa_ref, ba_ref, wx_ref, bx_ref, lam_ref,
                y_ref, ht_ref, nbuf_ref):
    t_len = xa_ref.shape[0]
    x = xa_ref[...]
    buf = buf_ref[...]
    cw = cw_ref[...]
    row = lax.broadcasted_iota(jnp.int32, x.shape, 0)
    nb = A_CONV - 1
    xc = x * cw[nb:nb + 1]
    for s in range(1, A_CONV):
        xs = pltpu.roll(x, s, 0)
        for r in range(s):
            xs = jnp.where(row == r, buf[nb + r - s:nb + r - s + 1], xs)
        xc = xc + xs * cw[nb - s:nb - s + 1]
    xc = xc + cb_ref[...]
    xcb = xc.astype(BF16)
    r_g = jax.nn.sigmoid(_dot(xcb, wa_ref[...].astype(BF16)) + ba_ref[...])
    i_g = jax.nn.sigmoid(_dot(xcb, wx_ref[...].astype(BF16)) + bx_ref[...])
    nl = -lam_ref[...]
    softplus = jnp.maximum(nl, 0.0) + jnp.log1p(jnp.exp(-jnp.abs(nl)))
    log_a = -LRU_C * r_g * softplus
    a = jnp.exp(log_a)
    u = jnp.sqrt(-jnp.tanh(log_a) * (a * a + 1.0)) * i_g * xc
    d = 1
    while d < t_len:
        keep = row >= d
        a_prev = pltpu.roll(a, d, 0)
        u_prev = pltpu.roll(u, d, 0)
        u = jnp.where(keep, a * u_prev + u, u)
        a = jnp.where(keep, a * a_prev, a)
        d *= 2
    h = a * h0_ref[...] + u
    y_ref[...] = (h * _gelu_tanh(ga_ref[...])).astype(y_ref.dtype)
    ht_ref[...] = h[t_len - 1:t_len]
    nbuf_ref[...] = x[t_len - nb:t_len]


def _lru_mixer(z, row0, n_seq, t_len, h0, buf, conv_w, conv_b, wa, ba, wx, bx, lam, out_dtype):
    assert row0 % t_len == 0 and t_len >= A_CONV - 1 and t_len % 8 == 0
    rb0 = row0 // t_len
    nb = A_CONV - 1
    vec = lambda v: v.reshape(1, D_A)
    cspec = pl.BlockSpec((1, A_BLK), lambda s, h: (0, h))
    wspec = pl.BlockSpec((None, A_BLK, A_BLK), lambda s, h: (h, 0, 0))
    return pl.pallas_call(
        _lru_kernel,
        out_shape=(jax.ShapeDtypeStruct((n_seq * t_len, D_A), out_dtype),
                   jax.ShapeDtypeStruct((n_seq, 1, D_A), F32),
                   jax.ShapeDtypeStruct((n_seq, nb, D_A), F32)),
        grid=(n_seq, A_HEADS),
        in_specs=[pl.BlockSpec((t_len, A_BLK), lambda s, h: (rb0 + s, COL_XA + h)),
                  pl.BlockSpec((t_len, A_BLK), lambda s, h: (rb0 + s, COL_GA + h)),
                  pl.BlockSpec((None, 1, A_BLK), lambda s, h: (s, 0, h)),
                  pl.BlockSpec((None, nb, A_BLK), lambda s, h: (s, 0, h)),
                  pl.BlockSpec((A_CONV, A_BLK), lambda s, h: (0, h)),
                  cspec, wspec, cspec, wspec, cspec, cspec],
        out_specs=(pl.BlockSpec((t_len, A_BLK), lambda s, h: (s, h)),
                   pl.BlockSpec((None, 1, A_BLK), lambda s, h: (s, 0, h)),
                   pl.BlockSpec((None, nb, A_BLK), lambda s, h: (s, 0, h))),
        compiler_params=_params(("parallel", "parallel")),
        name="rglru_mixer",
    )(z, z, h0.reshape(n_seq, 1, D_A), buf, conv_w, vec(conv_b), wa, vec(ba), wx, vec(bx), vec(lam))


def _compress_kernel(x_ref, pe_ref, w1_ref, b1_ref, w2_ref, o_ref):
    n_sub = o_ref.shape[0]
    pe = pe_ref[...]
    y0 = jnp.zeros((n_sub, HEAD_DIM), F32)
    y1 = jnp.zeros((n_sub, HEAD_DIM), F32)
    for j in range(CMP_STRIDE):
        xj = x_ref[pl.ds(j, n_sub, stride=CMP_STRIDE), :]
        y0 = y0 + _dot((xj + pe[j:j + 1]).astype(BF16), w1_ref[j])
        y1 = y1 + _dot((xj + pe[CMP_STRIDE + j:CMP_STRIDE + j + 1]).astype(BF16), w1_ref[CMP_STRIDE + j])
    hid = y0 + pltpu.roll(y1, n_sub - 1, 0)
    hid = _gelu_tanh(hid + b1_ref[...])
    o_ref[...] = _dot(hid.astype(BF16), w2_ref[...]).astype(o_ref.dtype)


def _compress(x2d, x_index, n_seq, t_len, pe, w1, b1, w2):
    assert SUB_PER_CMP == 2 and t_len % (8 * CMP_STRIDE) == 0
    n_sub = t_len // CMP_STRIDE
    nt = 2 * KV_HEADS
    return pl.pallas_call(
        _compress_kernel,
        out_shape=jax.ShapeDtypeStruct((n_seq, nt, n_sub, HEAD_DIM), BF16),
        grid=(n_seq, nt),
        in_specs=[pl.BlockSpec((t_len, HEAD_DIM), x_index),
                  pl.BlockSpec((None, CMP_LEN, HEAD_DIM), lambda s, c: (c // KV_HEADS, 0, 0)),
                  pl.BlockSpec((None, CMP_LEN, HEAD_DIM, HEAD_DIM), lambda s, c: (c // KV_HEADS, 0, 0, 0)),
                  pl.BlockSpec((None, 1, HEAD_DIM), lambda s, c: (c // KV_HEADS, 0, 0)),
                  pl.BlockSpec((None, HEAD_DIM, HEAD_DIM), lambda s, c: (c // KV_HEADS, 0, 0))],
        out_specs=pl.BlockSpec((None, None, n_sub, HEAD_DIM), lambda s, c: (s, c, 0, 0)),
        compiler_params=_params(("parallel", "arbitrary")),
        name="nsa_compress",
    )(x2d, pe, w1.astype(BF16), b1.reshape(2, 1, HEAD_DIM), w2.astype(BF16))


def _head_rows(ref, c, n_tok):
    return ref[pl.ds(c, n_tok, stride=2 * KV_HEADS), :]


def _page_gather_kernel(pt_ref, c_ref, o_ref):
    for c in range(2 * KV_HEADS):
        o_ref[c] = _head_rows(c_ref, c, PAGE_SIZE)


def _page_gather(cache_rows, layer, page_table):
    nb, n_pages = page_table.shape
    nt = 2 * KV_HEADS
    return pl.pallas_call(
        _page_gather_kernel,
        out_shape=jax.ShapeDtypeStruct((nb, nt, n_pages * PAGE_SIZE, HEAD_DIM), cache_rows.dtype),
        grid_spec=pltpu.PrefetchScalarGridSpec(
            num_scalar_prefetch=1, grid=(nb, n_pages),
            in_specs=[pl.BlockSpec((None, None, PAGE_SIZE * nt, HEAD_DIM), lambda b, p, pt: (layer, pt[b, p], 0, 0))],
            out_specs=pl.BlockSpec((None, nt, PAGE_SIZE, HEAD_DIM), lambda b, p, pt: (b, 0, p, 0))),
        compiler_params=_params(("parallel", "arbitrary")),
        name="page_gather",
    )(page_table, cache_rows)


def _stack_heads(q, g0):
    return jnp.concatenate([q[:, (g0 + p) * HEAD_DIM:(g0 + p + 1) * HEAD_DIM] for p in range(HPG)], axis=0).astype(BF16)


def _masked_softmax(s, ok):
    sb = jnp.where(ok, s, NEG_INF)
    m = jnp.max(sb, axis=-1, keepdims=True)
    e = jnp.exp(sb - m)
    return e / jnp.sum(e, axis=-1, keepdims=True)


def _cmp_branch(q4, kc, vc, t_rows, slope, n_cmp):
    ncp = kc.shape[0]
    s = _dot_nt(q4, kc) * QK_SCALE
    col = lax.broadcasted_iota(jnp.int32, s.shape, 1)
    dist_i = t_rows - (col * CMP_STRIDE + (CMP_LEN - 1))
    ok = (dist_i >= 0) & (col < n_cmp)
    p = _masked_softmax(s - _lanes(slope, ncp) * dist_i.astype(F32), ok)
    p = jnp.where(ok, p, 0.0)
    return _dot(p.astype(BF16), vc), p


def _select_blocks(imp, pool, t_q, n_sel):
    imp_sel = _dot_f32_exact(imp, pool)
    jcol = lax.broadcasted_iota(jnp.int32, imp_sel.shape, 1)
    cur = lax.shift_right_logical(t_q, int(np.log2(SEL_BLOCK)))
    forced = (jcol == 0) | (jcol == cur) | (jcol == cur - 1)
    valid = jcol * SEL_BLOCK <= t_q
    score = jnp.where(forced, FORCE_SCORE, jnp.where(valid, imp_sel, -1.0))
    score = jnp.where(jcol < n_sel, score, -2.0)
    rank = jnp.zeros(score.shape, F32)
    for i in range(n_sel):
        ci = score[:, i:i + 1]
        before = (ci > score) | ((ci == score) & (jcol > i))
        rank = rank + jnp.where(before, 1.0, 0.0)
    top = min(SEL_TOP, n_sel)
    return jnp.where(rank < float(top), 1.0, 0.0)


def _pool_matrix(n_cmp, ncp, n_sel, nsp):
    assert n_cmp + SUB_PER_CMP - 1 <= n_sel * SUB_PER_SEL
    m = np.zeros((ncp, nsp), np.float32)
    for c in range(n_cmp):
        for n in range(SUB_PER_CMP):
            m[c, (c + n) // SUB_PER_SEL] += 1.0 / SUB_PER_CMP
    return jnp.asarray(m, BF16)


def _slope_rows(t_rep):
    h = np.arange(1, B_HEADS + 1, dtype=np.float64)
    s = np.exp2(-8.0 * h / B_HEADS).astype(np.float32).reshape(KV_HEADS, HPG, 1, 1)
    return jnp.asarray(np.broadcast_to(s, (KV_HEADS, HPG, t_rep, LANE)).reshape(KV_HEADS, HPG * t_rep, LANE))


def _expand_matrix(shape, blk0):
    j = lax.broadcasted_iota(jnp.int32, shape, 0)
    k = lax.broadcasted_iota(jnp.int32, shape, 1)
    return jnp.where(j == blk0 + lax.shift_right_logical(k, int(np.log2(SEL_BLOCK))), 1.0, 0.0).astype(BF16)


def _gate_combine(gs, p, tq, o_c, o_s, o_w):
    rows = slice(p * tq, (p + 1) * tq)
    return (gs[:, p:p + 1] * o_c[rows] + gs[:, HPG + p:HPG + p + 1] * o_s[rows]
            + gs[:, 2 * HPG + p:2 * HPG + p + 1] * o_w[rows])


def _attn_prompt_kernel(q_ref, kc_ref, vc_ref, ks_ref, vs_ref, kw_ref, vw_ref, gt_ref, slope_ref, pool_ref,
                        o_ref, ksb, vsb, kwb, vwb, *, n_cmp, n_sel):
    qi = pl.program_id(2)
    tq = q_ref.shape[0]
    t_total = ks_ref.shape[0]
    rr = HPG * tq

    @pl.when(qi == 0)
    def _():
        ksb[...] = ks_ref[...].astype(BF16)
        vsb[...] = vs_ref[...].astype(BF16)
        kwb[...] = kw_ref[...].astype(BF16)
        vwb[...] = vw_ref[...].astype(BF16)

    t0 = qi * tq
    q4 = _stack_heads(q_ref[...], 0)
    slope = slope_ref[...]

    def t_rows(width):
        r = lax.broadcasted_iota(jnp.int32, (rr, width), 0)
        return t0 + (r & (tq - 1))

    ncp = kc_ref.shape[0]
    o_c, p_c = _cmp_branch(q4, kc_ref[...], vc_ref[...], t_rows(ncp), slope, n_cmp)
    imp = p_c[0:tq]
    for p in range(1, HPG):
        imp = imp + p_c[p * tq:(p + 1) * tq]
    nsp = pool_ref.shape[1]
    t_q = t0 + lax.broadcasted_iota(jnp.int32, (tq, nsp), 0)
    sel = _select_blocks(imp, pool_ref[...], t_q, n_sel).astype(BF16)
    sel4 = jnp.concatenate([sel] * HPG, axis=0)

    ck = ATT_CK
    tr_ck = t_rows(ck)
    slope_ck = _lanes(slope, ck)
    col_ck = lax.broadcasted_iota(jnp.int32, (rr, ck), 1)

    def sel_chunk(c, carry):
        m, l, acc = carry
        k0 = pl.multiple_of(c * ck, ck)
        s = _dot_nt(q4, ksb[pl.ds(k0, ck), :]) * QK_SCALE
        dist_i = tr_ck - (k0 + col_ck)
        selx = _dot(sel4, _expand_matrix((nsp, ck), c * (ck // SEL_BLOCK)))
        ok = (dist_i >= 0) & (selx > 0.5)
        sb = jnp.where(ok, s - slope_ck * dist_i.astype(F32), NEG_INF)
        m_new = jnp.maximum(m, jnp.max(sb, axis=-1, keepdims=True))
        alpha = jnp.exp(m - m_new)
        e = jnp.exp(sb - m_new)
        l = alpha * l + jnp.sum(e, axis=-1, keepdims=True)
        acc = alpha * acc + _dot(e.astype(BF16), vsb[pl.ds(k0, ck), :])
        return m_new, l, acc

    n_chunks = (t0 + tq + ck - 1) // ck
    m0 = jnp.full((rr, 1), NEG_INF, F32)
    _, l_s, acc_s = lax.fori_loop(0, n_chunks, sel_chunk, (m0, jnp.zeros((rr, 1), F32), jnp.zeros((rr, HEAD_DIM), F32)))
    o_s = acc_s / l_s

    wk = WINDOW + tq
    w0 = pl.multiple_of(jnp.maximum(t0 - WINDOW, 0), tq)
    s = _dot_nt(q4, kwb[pl.ds(w0, wk), :]) * QK_SCALE
    dist_i = t_rows(wk) - (w0 + lax.broadcasted_iota(jnp.int32, (rr, wk), 1))
    ok = (dist_i >= 0) & (dist_i < WINDOW)
    p_w = _masked_softmax(s - _lanes(slope, wk) * dist_i.astype(F32), ok)
    o_w = _dot(p_w.astype(BF16), vwb[pl.ds(w0, wk), :])

    gs = jax.nn.sigmoid(gt_ref[...])
    for p in range(HPG):
        o_ref[:, p * HEAD_DIM:(p + 1) * HEAD_DIM] = _gate_combine(gs, p, tq, o_c, o_s, o_w).astype(o_ref.dtype)


def _attn_prompt(z, gt, kvc_cmp, n_seq, t_len):
    tq = ATT_TQ
    assert t_len % ATT_CK == 0 and t_len >= WINDOW + tq and tq & (tq - 1) == 0
    n_cmp = (t_len - CMP_LEN) // CMP_STRIDE + 1
    n_sel = -(-t_len // SEL_BLOCK)
    ncp = kvc_cmp.shape[2]
    nsp = _round_up(n_sel, LANE)
    nq = t_len // tq
    rr = HPG * tq
    kvspec = lambda col0: pl.BlockSpec((t_len, HEAD_DIM), lambda b, g, i: (b, col0 + g))
    return pl.pallas_call(
        functools.partial(_attn_prompt_kernel, n_cmp=n_cmp, n_sel=n_sel),
        out_shape=jax.ShapeDtypeStruct((n_seq * t_len, D_B), BF16),
        grid=(n_seq, KV_HEADS, nq),
        in_specs=[pl.BlockSpec((tq, HPG * HEAD_DIM), lambda b, g, i: (b * nq + i, COL_Q // HPG + g)),
                  pl.BlockSpec((None, None, ncp, HEAD_DIM), lambda b, g, i: (b, g, 0, 0)),
                  pl.BlockSpec((None, None, ncp, HEAD_DIM), lambda b, g, i: (b, KV_HEADS + g, 0, 0)),
                  kvspec(COL_KVS), kvspec(COL_KVS + KV_HEADS), kvspec(COL_KVW), kvspec(COL_KVW + KV_HEADS),
                  pl.BlockSpec((tq, LANE), lambda b, g, i: (b * nq + i, g)),
                  pl.BlockSpec((None, rr, LANE), lambda b, g, i: (g, 0, 0)),
                  pl.BlockSpec((ncp, nsp), lambda b, g, i: (0, 0))],
        out_specs=pl.BlockSpec((tq, HPG * HEAD_DIM), lambda b, g, i: (b * nq + i, g)),
        scratch_shapes=[pltpu.VMEM((t_len, HEAD_DIM), BF16)] * 4,
        compiler_params=_params(("parallel", "parallel", "arbitrary")),
        name="nsa_prompt_attention",
    )(z, kvc_cmp, kvc_cmp, z, z, z, z, gt, _slope_rows(tq), _pool_matrix(n_cmp, ncp, n_sel, nsp))


def _attn_dec_select_kernel(q_ref, kc_ref, vc_ref, slope_ref, pool_ref, oc_ref, sel_ref, *, n_cmp, n_sel, past_len):
    ts = q_ref.shape[0]
    rr = HPG * ts
    ncp = kc_ref.shape[0]
    q4 = _stack_heads(q_ref[...], 0)
    t_rows = past_len + (lax.broadcasted_iota(jnp.int32, (rr, ncp), 0) & (ts - 1))
    o_c, p_c = _cmp_branch(q4, kc_ref[...], vc_ref[...], t_rows, slope_ref[...], n_cmp)
    imp = p_c[0:ts]
    for p in range(1, HPG):
        imp = imp + p_c[p * ts:(p + 1) * ts]
    nsp = pool_ref.shape[1]
    t_q = past_len + lax.broadcasted_iota(jnp.int32, (ts, nsp), 0)
    oc_ref[...] = o_c
    sel_ref[...] = _select_blocks(imp, pool_ref[...], t_q, n_sel)


def _attn_dec_select(z, row0, kvc_cmp, n_seq, ts, past_len, n_cmp, n_sel):
    assert row0 % ts == 0 and ts & (ts - 1) == 0 and ts % 8 == 0
    rb0 = row0 // ts
    ncp = kvc_cmp.shape[2]
    nsp = _round_up(n_sel, LANE)
    rr = HPG * ts
    return pl.pallas_call(
        functools.partial(_attn_dec_select_kernel, n_cmp=n_cmp, n_sel=n_sel, past_len=past_len),
        out_shape=(jax.ShapeDtypeStruct((n_seq, KV_HEADS, rr, HEAD_DIM), F32),
                   jax.ShapeDtypeStruct((n_seq, KV_HEADS, ts, nsp), F32)),
        grid=(n_seq, KV_HEADS),
        in_specs=[pl.BlockSpec((ts, HPG * HEAD_DIM), lambda b, g: (rb0 + b, COL_Q // HPG + g)),
                  pl.BlockSpec((None, None, ncp, HEAD_DIM), lambda b, g: (b, g, 0, 0)),
                  pl.BlockSpec((None, None, ncp, HEAD_DIM), lambda b, g: (b, KV_HEADS + g, 0, 0)),
                  pl.BlockSpec((None, rr, LANE), lambda b, g: (g, 0, 0)),
                  pl.BlockSpec((ncp, nsp), lambda b, g: (0, 0))],
        out_specs=(pl.BlockSpec((None, None, rr, HEAD_DIM), lambda b, g: (b, g, 0, 0)),
                   pl.BlockSpec((None, None, ts, nsp), lambda b, g: (b, g, 0, 0))),
        compiler_params=_params(("parallel", "parallel")),
        name="nsa_decode_select",
    )(z, kvc_cmp, kvc_cmp, _slope_rows(ts), _pool_matrix(n_cmp, ncp, n_sel, nsp))


def _attn_dec_kernel(pt_ref, q_ref, sel_ref, *rest, past_len, n_steps, pages_per_step):
    page_refs = rest[:pages_per_step]
    kvs_ref, kvw_ref, win_ref, oc_ref, gt_ref, slope_ref, o_ref, nwin_ref, m_sc, l_sc, acc_sc = rest[pages_per_step:]
    pg = pl.program_id(1)
    ts = q_ref.shape[0]
    rr = HPG * ts
    nsp = sel_ref.shape[-1]
    pad_rows = LANE - ts
    row = lax.broadcasted_iota(jnp.int32, (rr, LANE), 0)
    col = lax.broadcasted_iota(jnp.int32, (rr, LANE), 1)
    t_rows = past_len + (row & (ts - 1))

    @pl.when(pg == 0)
    def _():
        m_sc[...] = jnp.full(m_sc.shape, NEG_INF, F32)
        l_sc[...] = jnp.zeros(l_sc.shape, F32)
        acc_sc[...] = jnp.zeros(acc_sc.shape, F32)

    def sel_update(g, q4, kk, vv, dist_i, expand, extra_ok):
        width = kk.shape[0]
        s = _dot_nt(q4, kk) * QK_SCALE
        sel4 = jnp.concatenate([sel_ref[g].astype(BF16)] * HPG, axis=0)
        ok = (dist_i >= 0) & (_dot(sel4, expand) > 0.5) & extra_ok
        sb = jnp.where(ok, s - _lanes(slope_ref[g], width) * dist_i.astype(F32), NEG_INF)
        m_old = m_sc[g]
        m_new = jnp.maximum(m_old, jnp.max(sb, axis=-1, keepdims=True))
        alpha = jnp.exp(m_old - m_new)
        e = jnp.exp(sb - m_new)
        l_sc[g] = alpha * l_sc[g] + jnp.sum(e, axis=-1, keepdims=True)
        acc_sc[g] = alpha * acc_sc[g] + _dot(e.astype(BF16), vv)
        m_sc[g] = m_new

    @pl.when(pg < n_steps)
    def _():
        width = pages_per_step * PAGE_SIZE
        k0 = pg * width
        expand = _expand_matrix((nsp, width), pg * (width // SEL_BLOCK))
        wrow_ = lax.broadcasted_iota(jnp.int32, (rr, width), 0)
        wcol_ = lax.broadcasted_iota(jnp.int32, (rr, width), 1)
        dist_i = (past_len + (wrow_ & (ts - 1))) - (k0 + wcol_)
        for g in range(KV_HEADS):
            q4 = _stack_heads(q_ref[...], g * HPG)
            kk = jnp.concatenate([_head_rows(r, g, PAGE_SIZE) for r in page_refs], axis=0).astype(BF16)
            vv = jnp.concatenate([_head_rows(r, KV_HEADS + g, PAGE_SIZE) for r in page_refs], axis=0).astype(BF16)
            sel_update(g, q4, kk, vv, dist_i, expand, True)

    @pl.when(pg == n_steps)
    def _():
        zpad = jnp.zeros((pad_rows, HEAD_DIM), F32)
        gs = jax.nn.sigmoid(gt_ref[...])
        jj = lax.broadcasted_iota(jnp.int32, (nsp, LANE), 0)
        kk_i = lax.broadcasted_iota(jnp.int32, (nsp, LANE), 1)
        new_blk = lax.shift_right_logical(past_len + kk_i, int(np.log2(SEL_BLOCK)))
        expand_new = jnp.where(jj == new_blk, 1.0, 0.0).astype(BF16)
        is_new = col < ts
        nt = 2 * KV_HEADS
        wlen = win_ref.shape[0] // nt
        wrow = lax.broadcasted_iota(jnp.int32, (rr, wlen), 0)
        wcol = lax.broadcasted_iota(jnp.int32, (rr, wlen), 1)
        dist_w = (past_len + (wrow & (ts - 1))) - (past_len - wlen + wcol)
        ok_w = (dist_w >= 0) & (dist_w < WINDOW)
        dist_n = t_rows - (past_len + col)
        ok_n = (dist_n >= 0) & (dist_n < WINDOW) & is_new
        for g in range(KV_HEADS):
            q4 = _stack_heads(q_ref[...], g * HPG)
            ksl = slice(g * HEAD_DIM, (g + 1) * HEAD_DIM)
            vsl = slice(D_KV + g * HEAD_DIM, D_KV + (g + 1) * HEAD_DIM)
            kn = jnp.concatenate([kvs_ref[:, ksl], zpad], axis=0).astype(BF16)
            vn = jnp.concatenate([kvs_ref[:, vsl], zpad], axis=0).astype(BF16)
            sel_update(g, q4, kn, vn, dist_n, expand_new, is_new)
            o_s = acc_sc[g] / l_sc[g]
            slope = slope_ref[g]
            s1 = _dot_nt(q4, _head_rows(win_ref, g, wlen).astype(BF16)) * QK_SCALE
            s1 = jnp.where(ok_w, s1 - _lanes(slope, wlen) * dist_w.astype(F32), NEG_INF)
            kwn = jnp.concatenate([kvw_ref[:, ksl], zpad], axis=0).astype(BF16)
            vwn = jnp.concatenate([kvw_ref[:, vsl], zpad], axis=0).astype(BF16)
            s2 = _dot_nt(q4, kwn) * QK_SCALE
            s2 = jnp.where(ok_n, s2 - slope * dist_n.astype(F32), NEG_INF)
            m = jnp.maximum(jnp.max(s1, axis=-1, keepdims=True), jnp.max(s2, axis=-1, keepdims=True))
            e1 = jnp.exp(s1 - m)
            e2 = jnp.exp(s2 - m)
            inv = 1.0 / (jnp.sum(e1, axis=-1, keepdims=True) + jnp.sum(e2, axis=-1, keepdims=True))
            o_w = (_dot((e1 * inv).astype(BF16), _head_rows(win_ref, KV_HEADS + g, wlen).astype(BF16))
                   + _dot((e2 * inv).astype(BF16), vwn))
            o_c = oc_ref[g]
            for p in range(HPG):
                c0 = (g * HPG + p) * HEAD_DIM
                o_ref[:, c0:c0 + HEAD_DIM] = _gate_combine(gs[:, g * LANE:(g + 1) * LANE], p, ts, o_c, o_s, o_w)
        keep = (wlen - ts) * nt
        nwin_ref[0:keep, :] = win_ref[ts * nt:wlen * nt, :]
        for c in range(nt):
            nwin_ref[pl.ds(keep + c, ts, stride=nt), :] = kvw_ref[:, c * HEAD_DIM:(c + 1) * HEAD_DIM]


def _attn_decode(z, gt, row0, n_seq, ts, page_table, cache_rows, layer, win_rows, o_c, sel, past_len):
    n_pages = page_table.shape[1]
    pps = 4 if n_pages % 4 == 0 else 1
    n_steps = n_pages // pps
    rb0 = row0 // ts
    rr = HPG * ts
    nsp = sel.shape[-1]
    nt = 2 * KV_HEADS
    wrows = win_rows.shape[2]
    w = 2 * D_KV
    assert wrows == WINDOW * nt and past_len % SEL_BLOCK == 0 and PAGE_SIZE == LANE
    last = n_steps - 1

    def page_spec(k):
        return pl.BlockSpec((None, None, PAGE_SIZE * nt, HEAD_DIM),
                            lambda b, p, pt: (layer, pt[b, jnp.minimum(p, last) * pps + k], 0, 0))

    return pl.pallas_call(
        functools.partial(_attn_dec_kernel, past_len=past_len, n_steps=n_steps, pages_per_step=pps),
        out_shape=(jax.ShapeDtypeStruct((n_seq * ts, D_B), F32),
                   jax.ShapeDtypeStruct((n_seq, wrows, HEAD_DIM), F32)),
        grid_spec=pltpu.PrefetchScalarGridSpec(
            num_scalar_prefetch=1, grid=(n_seq, n_steps + 1),
            in_specs=[pl.BlockSpec((ts, D_B), lambda b, p, pt: (rb0 + b, COL_Q * LANE // D_B)),
                      pl.BlockSpec((None, KV_HEADS, ts, nsp), lambda b, p, pt: (b, 0, 0, 0))]
                     + [page_spec(k) for k in range(pps)]
                     + [pl.BlockSpec((ts, w), lambda b, p, pt: (rb0 + b, COL_KVS * LANE // w)),
                        pl.BlockSpec((ts, w), lambda b, p, pt: (rb0 + b, COL_KVW * LANE // w)),
                        pl.BlockSpec((None, None, wrows, HEAD_DIM), lambda b, p, pt: (layer, b, 0, 0)),
                        pl.BlockSpec((None, KV_HEADS, rr, HEAD_DIM), lambda b, p, pt: (b, 0, 0, 0)),
                        pl.BlockSpec((ts, KV_HEADS * LANE), lambda b, p, pt: (rb0 + b, 0)),
                        pl.BlockSpec((KV_HEADS, rr, LANE), lambda b, p, pt: (0, 0, 0))],
            out_specs=(pl.BlockSpec((ts, D_B), lambda b, p, pt: (b, 0)),
                       pl.BlockSpec((None, wrows, HEAD_DIM), lambda b, p, pt: (b, 0, 0))),
            scratch_shapes=[pltpu.VMEM((KV_HEADS, rr, 1), F32), pltpu.VMEM((KV_HEADS, rr, 1), F32),
                            pltpu.VMEM((KV_HEADS, rr, HEAD_DIM), F32)]),
        compiler_params=_params(("parallel", "arbitrary")),
        name="nsa_decode_attention",
    )(page_table, z, sel, *([cache_rows] * pps), z, z, win_rows, o_c, gt, _slope_rows(ts))


def _sconv_kernel(bg_ref, cg_ref, v_ref, buf_ref, cw_ref, y_ref, nbuf_ref):
    t_len = cg_ref.shape[0]
    nb = C_CONV - 1
    cv = cg_ref[...] * v_ref[...]
    buf = buf_ref[...]
    cw = cw_ref[...]
    row = lax.broadcasted_iota(jnp.int32, cv.shape, 0)
    y = cv * cw[nb:nb + 1]
    for s in range(1, C_CONV):
        xs = pltpu.roll(cv, s, 0)
        for r in range(s):
            xs = jnp.where(row == r, buf[nb + r - s:nb + r - s + 1], xs)
        y = y + xs * cw[nb - s:nb - s + 1]
    y_ref[...] = (bg_ref[...] * y).astype(y_ref.dtype)
    nbuf_ref[...] = cv[t_len - nb:t_len]


def _sconv_mixer(zc, row0, n_seq, t_len, buf, conv_w, out_dtype):
    assert row0 % t_len == 0 and t_len >= C_CONV - 1 and t_len % 8 == 0
    rb0 = row0 // t_len
    nb = C_CONV - 1
    tc = 256
    nc = D_C // tc
    zspec = lambda third: pl.BlockSpec((t_len, tc), lambda s, c: (rb0 + s, third * nc + c))
    return pl.pallas_call(
        _sconv_kernel,
        out_shape=(jax.ShapeDtypeStruct((n_seq * t_len, D_C), out_dtype),
                   jax.ShapeDtypeStruct((n_seq, nb, D_C), F32)),
        grid=(n_seq, nc),
        in_specs=[zspec(0), zspec(1), zspec(2),
                  pl.BlockSpec((None, nb, tc), lambda s, c: (s, 0, c)),
                  pl.BlockSpec((C_CONV, tc), lambda s, c: (0, c))],
        out_specs=(pl.BlockSpec((t_len, tc), lambda s, c: (s, c)),
                   pl.BlockSpec((None, nb, tc), lambda s, c: (s, 0, c))),
        compiler_params=_params(("parallel", "parallel")),
        name="sconv_mixer",
    )(zc, zc, zc, buf, conv_w)


def _gate_weight(w_in):
    k = w_in.shape[0]
    wg = w_in[:, D_IN_MAIN:].reshape(k, 3, KV_HEADS, HPG).transpose(0, 2, 1, 3).reshape(k, KV_HEADS, 3 * HPG)
    return jnp.pad(wg, ((0, 0), (0, 0), (0, LANE - 3 * HPG))).reshape(k, KV_HEADS * LANE)


def _ab_layer(h, hn, i, mp, bp, t_len, bs, ts, past_len, w, cache_kv_cmp, cache_kv_sel, state_win_kv, state_lru_h,
              state_lru_conv, page_table):
    w_in = w["ab_w_in"][i].astype(BF16)
    z, kv_rows = _in_proj(hn, w_in[:, :D_IN_MAIN])
    gt = _matmul(hn, _gate_weight(w_in), name="ab_gate_proj")
    nt = 2 * KV_HEADS
    rows_view = lambda a: a.reshape(a.shape[0], a.shape[1], -1, HEAD_DIM)
    lru_w = (w["ab_conv_w"][i], w["ab_conv_b"][i], w["ab_gate_a_w"][i], w["ab_gate_a_b"][i], w["ab_gate_x_w"][i],
             w["ab_gate_x_b"][i], w["ab_lru_lambda"][i])
    cmp_w = (w["ab_cmp_pe"][i], w["ab_cmp_w1"][i], w["ab_cmp_b1"][i], w["ab_cmp_w2"][i])
    n_pages = page_table.shape[1]
    tk = past_len + ts
    n_cmp_s = (tk - CMP_LEN) // CMP_STRIDE + 1
    n_sel_s = -(-tk // SEL_BLOCK)
    assert n_cmp_s == (past_len - CMP_LEN) // CMP_STRIDE + 1

    ya_p, hT_p, nbuf_p = _lru_mixer(z, 0, bp, t_len, jnp.zeros((bp, D_A), F32), jnp.zeros((bp, A_CONV - 1, D_A), F32),
                                    *lru_w, out_dtype=BF16)
    kvc_p = _compress(z, lambda s, c: (s, COL_KVC + c), bp, t_len, *cmp_w)
    ob_p = _attn_prompt(z, gt, kvc_p, bp, t_len)

    ya_s, hT_s, nbuf_s = _lru_mixer(z, mp, bs, ts, state_lru_h[i], state_lru_conv[i], *lru_w, out_dtype=F32)
    past_cmp = _page_gather(rows_view(cache_kv_cmp), i, page_table)
    kvc_s = _compress(past_cmp.reshape(bs * nt * past_len, HEAD_DIM), lambda s, c: (s * nt + c, 0), bs, past_len, *cmp_w)
    oc_s, sel_s = _attn_dec_select(z, mp, kvc_s, bs, ts, past_len, n_cmp_s, n_sel_s)
    ob_s, nwin_s = _attn_decode(z, gt, mp, bs, ts, page_table, rows_view(cache_kv_sel), i, rows_view(state_win_kv),
                                oc_s, sel_s, past_len)

    ya = jnp.concatenate([ya_p, ya_s.astype(BF16)], axis=0)
    ob = jnp.concatenate([ob_p, ob_s.astype(BF16)], axis=0)
    h = _matmul2_residual(ya, ob, w["ab_w_out"][i].astype(BF16), h, name="ab_out_proj")

    kv5 = lambda a, nb_, tt: a.reshape(nb_, tt, 2, KV_HEADS, HEAD_DIM)
    wl = min(WINDOW, t_len)
    kvw_p = kv5(kv_rows[2, :mp * nt], bp, t_len)
    outs_p = (kv5(kv_rows[0, :mp * nt], bp, t_len), kv5(kv_rows[1, :mp * nt], bp, t_len),
              kvw_p[:, t_len - wl:], hT_p.reshape(bp, D_A), nbuf_p)
    outs_s = (kv5(kv_rows[0, mp * nt:], bs, ts), kv5(kv_rows[1, mp * nt:], bs, ts),
              kv5(nwin_s, bs, WINDOW), hT_s.reshape(bs, D_A), nbuf_s)
    return h, outs_p, outs_s


def _c_layer(h, hn, i, mp, bp, t_len, bs, ts, w, state_sconv):
    zc = _matmul(hn, w["c_w_in"][i].astype(BF16), name="c_in_proj", row_cap=FFN_UP_ROW_CAP)
    y_p, nbuf_p = _sconv_mixer(zc, 0, bp, t_len, jnp.zeros((bp, C_CONV - 1, D_C), F32), w["c_conv_w"][i], BF16)
    y_s, nbuf_s = _sconv_mixer(zc, mp, bs, ts, state_sconv[i], w["c_conv_w"][i], F32)
    y = jnp.concatenate([y_p, y_s.astype(BF16)], axis=0)
    h = _matmul_residual(y, w["c_w_out"][i].astype(BF16), h, name="c_out_proj")
    return h, nbuf_p, nbuf_s


def kernel(x_prompt, x_sample, cache_kv_cmp, cache_kv_sel, state_win_kv, state_lru_h, state_lru_conv, state_sconv,
           page_table, norm_mix, norm_ffn, norm_final, ab_w_in, ab_conv_w, ab_conv_b, ab_gate_a_w, ab_gate_a_b,
           ab_gate_x_w, ab_gate_x_b, ab_lru_lambda, ab_cmp_pe, ab_cmp_w1, ab_cmp_b1, ab_cmp_w2, ab_w_out,
           c_w_in, c_conv_w, c_w_out, ffn_w_gate, ffn_w_up, ffn_w_down):
    w = dict(ab_w_in=ab_w_in, ab_conv_w=ab_conv_w, ab_conv_b=ab_conv_b, ab_gate_a_w=ab_gate_a_w,
             ab_gate_a_b=ab_gate_a_b, ab_gate_x_w=ab_gate_x_w, ab_gate_x_b=ab_gate_x_b, ab_lru_lambda=ab_lru_lambda,
             ab_cmp_pe=ab_cmp_pe, ab_cmp_w1=ab_cmp_w1, ab_cmp_b1=ab_cmp_b1, ab_cmp_w2=ab_cmp_w2, ab_w_out=ab_w_out,
             c_w_in=c_w_in, c_conv_w=c_conv_w, c_w_out=c_w_out)
    bp, t_len = x_prompt.shape[:2]
    bs, ts = x_sample.shape[:2]
    mp, ms = bp * t_len, bs * ts
    past_len = page_table.shape[1] * PAGE_SIZE
    h = jnp.concatenate([x_prompt.reshape(mp, D_MODEL), x_sample.reshape(ms, D_MODEL)], axis=0)
    p_ab, s_ab, p_c, s_c = [], [], [], []
    for layer in range(DEPTH):
        hn = _rmsnorm(h, norm_mix[layer], BF16)
        i = layer // 2
        if layer % 2 == 0:
            h, op, os_ = _ab_layer(h, hn, i, mp, bp, t_len, bs, ts, past_len, w, cache_kv_cmp, cache_kv_sel,
                                   state_win_kv, state_lru_h, state_lru_conv, page_table)
            p_ab.append(op)
            s_ab.append(os_)
        else:
            h, nbp, nbs = _c_layer(h, hn, i, mp, bp, t_len, bs, ts, w, state_sconv)
            p_c.append(nbp)
            s_c.append(nbs)
        hf = _rmsnorm(h, norm_ffn[layer], BF16)
        hid = _ffn_up(hf, ffn_w_gate[layer].astype(BF16), ffn_w_up[layer].astype(BF16))
        h = _ffn_down(hid, ffn_w_down[layer].astype(BF16), h)
    y = _rmsnorm(h, norm_final, F32)
    y_prompt = y[:mp].reshape(bp, t_len, D_MODEL)
    y_sample = y[mp:].reshape(bs, ts, D_MODEL)
    stack = lambda rows, j: jnp.stack([r[j] for r in rows])
    return (y_prompt, y_sample,
            stack(p_ab, 0), stack(p_ab, 1), stack(p_ab, 2), stack(p_ab, 3), stack(p_ab, 4), jnp.stack(p_c),
            stack(s_ab, 0), stack(s_ab, 1), stack(s_ab, 2), stack(s_ab, 3), stack(s_ab, 4), jnp.stack(s_c))
```

```python
import functools

import jax
import jax.numpy as jnp
import numpy as np
from jax import lax
from jax.experimental import pallas as pl
from jax.experimental.pallas import tpu as pltpu

D_MODEL = 4096
DEPTH = 4
PAGE_SIZE = 128
N_AB = (DEPTH + 1) // 2
N_C = DEPTH // 2
D_A = D_MODEL // 2
A_HEADS = 16
A_BLK = D_A // A_HEADS
A_CONV = 4
LRU_C = 8.0
B_HEADS = 16
HEAD_DIM = 128
KV_HEADS = 4
HPG = B_HEADS // KV_HEADS
D_B = B_HEADS * HEAD_DIM
D_KV = KV_HEADS * HEAD_DIM
CMP_LEN = 32
CMP_STRIDE = 16
SUB_PER_CMP = CMP_LEN // CMP_STRIDE
SEL_BLOCK = 64
SUB_PER_SEL = SEL_BLOCK // CMP_STRIDE
SEL_TOP = 16
WINDOW = 512
D_C = D_MODEL
C_CONV = 3
D_FF = -(-8 * D_MODEL // (3 * 256)) * 256
D_IN_MAIN = 2 * D_A + D_B + 6 * D_KV
N_GATES = 3 * B_HEADS
RMS_EPS = 1e-6
NEG_INF = -1e30
FORCE_SCORE = 1e4
QK_SCALE = HEAD_DIM ** -0.5

COL_XA = 0
COL_GA = D_A // 128
COL_Q = 2 * D_A // 128
COL_KVC = (2 * D_A + D_B) // 128
COL_KVS = COL_KVC + 2 * D_KV // 128
COL_KVW = COL_KVS + 2 * D_KV // 128

V7X_VMEM_LIMIT_BYTES = 56 * 1024 * 1024
LANE = 128
ATT_TQ = 128
ATT_CK = 512
F32 = jnp.float32
BF16 = jnp.bfloat16


def _row_tile(m, cap=1024):
    for t in range(cap - cap % 16, 15, -16):
        if m % t == 0:
            return t
    raise ValueError(m)


def _col_tile(n, cap=512):
    for t in range(cap, LANE - 1, -LANE):
        if n % t == 0:
            return t
    raise ValueError(n)


def _round_up(x, m):
    return -(-x // m) * m


def _params(sem):
    return pltpu.CompilerParams(dimension_semantics=sem, vmem_limit_bytes=V7X_VMEM_LIMIT_BYTES)


def _dot(a, b):
    return jnp.dot(a, b, preferred_element_type=F32)


def _dot_nt(a, b):
    return lax.dot_general(a, b, (((1,), (1,)), ((), ())), preferred_element_type=F32)


def _dot_f32_exact(x, w_bf16):
    hi = x.astype(BF16)
    r1 = x - hi.astype(F32)
    mid = r1.astype(BF16)
    lo = (r1 - mid.astype(F32)).astype(BF16)
    return _dot(hi, w_bf16) + (_dot(mid, w_bf16) + _dot(lo, w_bf16))


def _gelu_tanh(x):
    return 0.5 * x * (1.0 + jnp.tanh(0.7978845608028654 * (x + 0.044715 * (x * x * x))))


def _lanes(x128, w):
    return x128 if w == LANE else jnp.concatenate([x128] * (w // LANE), axis=1)


def _rmsnorm_kernel(x_ref, g_ref, o_ref):
    x = x_ref[...]
    y = x * lax.rsqrt(jnp.mean(x * x, axis=-1, keepdims=True) + RMS_EPS)
    o_ref[...] = (y * g_ref[...]).astype(o_ref.dtype)


def _rmsnorm(x, g, out_dtype):
    m, d = x.shape
    tm = _row_tile(m)
    return pl.pallas_call(
        _rmsnorm_kernel,
        out_shape=jax.ShapeDtypeStruct((m, d), out_dtype),
        grid=(m // tm,),
        in_specs=[pl.BlockSpec((tm, d), lambda i: (i, 0)), pl.BlockSpec((1, d), lambda i: (0, 0))],
        out_specs=pl.BlockSpec((tm, d), lambda i: (i, 0)),
        compiler_params=_params(("parallel",)),
        name="rmsnorm",
    )(x, g.reshape(1, d))


def _wspec(layer, k, tn):
    return pl.BlockSpec((None, k, tn), lambda i, j: (layer, 0, j))


def _mm_kernel(x_ref, w_ref, o_ref):
    o_ref[...] = _dot(x_ref[...], w_ref[...].astype(BF16)).astype(o_ref.dtype)


def _matmul(x, w, layer, out_dtype=F32, name="matmul", row_cap=1024, col_cap=512):
    m, k = x.shape
    n = w.shape[2]
    tm, tn = _row_tile(m, row_cap), _col_tile(n, col_cap)
    return pl.pallas_call(
        _mm_kernel,
        out_shape=jax.ShapeDtypeStruct((m, n), out_dtype),
        grid=(m // tm, n // tn),
        in_specs=[pl.BlockSpec((tm, k), lambda i, j: (i, 0)), _wspec(layer, k, tn)],
        out_specs=pl.BlockSpec((tm, tn), lambda i, j: (i, j)),
        compiler_params=_params(("parallel", "arbitrary")),
        name=name,
    )(x, w)


def _in_proj_kernel(x_ref, w_ref, z_ref, kv_ref, *, j_kv0, tiles_per_seg):
    j = pl.program_id(1)
    tm, tn = z_ref.shape
    heads_per_tile = tn // HEAD_DIM
    rows_per_token = 2 * KV_HEADS
    acc = _dot(x_ref[...], w_ref[...].astype(BF16))
    z_ref[...] = acc
    for part in range(tiles_per_seg):
        @pl.when((j >= j_kv0) & ((j - j_kv0) % tiles_per_seg == part))
        def _():
            for c in range(heads_per_tile):
                kv_ref[pl.ds(part * heads_per_tile + c, tm, stride=rows_per_token), :] = (
                    acc[:, c * HEAD_DIM:(c + 1) * HEAD_DIM])


def _in_proj(x, w, layer):
    m, k = x.shape
    n = D_IN_MAIN
    tm, tn = _row_tile(m), _col_tile(n)
    seg = 2 * D_KV
    assert seg % tn == 0 and (COL_KVC * LANE) % tn == 0 and n == COL_KVC * LANE + 3 * seg and w.shape[2] >= n
    tiles_per_seg = seg // tn
    j_kv0 = COL_KVC * LANE // tn
    rpt = 2 * KV_HEADS
    return pl.pallas_call(
        functools.partial(_in_proj_kernel, j_kv0=j_kv0, tiles_per_seg=tiles_per_seg),
        out_shape=(jax.ShapeDtypeStruct((m, n), F32), jax.ShapeDtypeStruct((3, m * rpt, HEAD_DIM), F32)),
        grid=(m // tm, n // tn),
        in_specs=[pl.BlockSpec((tm, k), lambda i, j: (i, 0)), _wspec(layer, k, tn)],
        out_specs=(pl.BlockSpec((tm, tn), lambda i, j: (i, j)),
                   pl.BlockSpec((None, tm * rpt, HEAD_DIM),
                                lambda i, j: (jnp.maximum(j - j_kv0, 0) // tiles_per_seg, i, 0))),
        compiler_params=_params(("parallel", "arbitrary")),
        name="ab_in_proj",
    )(x, w)


def _mm_res_kernel(x_ref, w_ref, r_ref, o_ref):
    o_ref[...] = r_ref[...] + _dot(x_ref[...], w_ref[...].astype(BF16))


def _matmul_residual(x, w, layer, r, name="matmul_res"):
    m, k = x.shape
    n = w.shape[2]
    tm, tn = _row_tile(m), _col_tile(n)
    return pl.pallas_call(
        _mm_res_kernel,
        out_shape=jax.ShapeDtypeStruct((m, n), F32),
        grid=(m // tm, n // tn),
        in_specs=[pl.BlockSpec((tm, k), lambda i, j: (i, 0)), _wspec(layer, k, tn),
                  pl.BlockSpec((tm, tn), lambda i, j: (i, j))],
        out_specs=pl.BlockSpec((tm, tn), lambda i, j: (i, j)),
        compiler_params=_params(("parallel", "arbitrary")),
        name=name,
    )(x, w, r)


def _mm2_res_kernel(x1_ref, x2_ref, w1_ref, w2_ref, r_ref, o_ref):
    o_ref[...] = r_ref[...] + (_dot(x1_ref[...], w1_ref[...].astype(BF16))
                               + _dot(x2_ref[...], w2_ref[...].astype(BF16)))


def _matmul2_residual(x1, x2, w, layer, r, name="matmul2_res"):
    m, k1 = x1.shape
    k2 = x2.shape[1]
    n = w.shape[2]
    tm, tn = _row_tile(m), _col_tile(n)
    assert k1 % 8 == 0 and k1 == k2
    return pl.pallas_call(
        _mm2_res_kernel,
        out_shape=jax.ShapeDtypeStruct((m, n), F32),
        grid=(m // tm, n // tn),
        in_specs=[pl.BlockSpec((tm, k1), lambda i, j: (i, 0)), pl.BlockSpec((tm, k2), lambda i, j: (i, 0)),
                  pl.BlockSpec((None, k1, tn), lambda i, j: (layer, 0, j)),
                  pl.BlockSpec((None, k2, tn), lambda i, j: (layer, 1, j)),
                  pl.BlockSpec((tm, tn), lambda i, j: (i, j))],
        out_specs=pl.BlockSpec((tm, tn), lambda i, j: (i, j)),
        compiler_params=_params(("parallel", "arbitrary")),
        name=name,
    )(x1, x2, w, w, r)


def _ffn_up_kernel(x_ref, wg_ref, wu_ref, o_ref):
    x = x_ref[...]
    g = _dot(x, wg_ref[...].astype(BF16))
    u = _dot(x, wu_ref[...].astype(BF16))
    o_ref[...] = (g * jax.nn.sigmoid(g) * u).astype(o_ref.dtype)


WIDE_ROW_CAP = 1536
F32_WEIGHT_COL_CAP = 256


def _ffn_up(x, wg, wu, layer):
    m, k = x.shape
    n = wg.shape[2]
    tm, tn = _row_tile(m, WIDE_ROW_CAP), _col_tile(n, F32_WEIGHT_COL_CAP)
    return pl.pallas_call(
        _ffn_up_kernel,
        out_shape=jax.ShapeDtypeStruct((m, n), BF16),
        grid=(m // tm, n // tn),
        in_specs=[pl.BlockSpec((tm, k), lambda i, j: (i, 0)), _wspec(layer, k, tn), _wspec(layer, k, tn)],
        out_specs=pl.BlockSpec((tm, tn), lambda i, j: (i, j)),
        compiler_params=_params(("parallel", "arbitrary")),
        name="ffn_up",
    )(x, wg, wu)


def _ffn_down_kernel(x_ref, w_ref, r_ref, o_ref, acc_ref):
    kk = pl.program_id(1)
    j = pl.program_id(2)
    part = _dot(x_ref[...], w_ref[...])

    @pl.when(kk == 0)
    def _():
        acc_ref[j] = r_ref[...] + part

    @pl.when(kk == pl.num_programs(1) - 1)
    def _():
        o_ref[...] = acc_ref[j] + part


def _ffn_down(h, wd, layer, r):
    m, k = h.shape
    n = wd.shape[2]
    tm, tn = _row_tile(m), _col_tile(n)
    nk = 2
    tk = k // nk
    assert tk * nk == k and tk % LANE == 0
    return pl.pallas_call(
        _ffn_down_kernel,
        out_shape=jax.ShapeDtypeStruct((m, n), F32),
        grid=(m // tm, nk, n // tn),
        in_specs=[pl.BlockSpec((tm, tk), lambda i, kk, j: (i, kk)),
                  pl.BlockSpec((None, tk, tn), lambda i, kk, j: (layer, kk, j)),
                  pl.BlockSpec((tm, tn), lambda i, kk, j: (i, jnp.where(kk == 0, j, 0)))],
        out_specs=pl.BlockSpec((tm, tn), lambda i, kk, j: (i, jnp.where(kk == nk - 1, j, 0))),
        scratch_shapes=[pltpu.VMEM((n // tn, tm, tn), F32)],
        compiler_params=_params(("parallel", "arbitrary", "arbitrary")),
        name="ffn_down",
    )(h, wd, r)


def _lru_kernel(xa_ref, ga_ref, h0_ref, buf_ref, cw_ref, cb_ref, wa_ref, ba_ref, wx_ref, bx_ref, lam_ref,
                y_ref, ht_ref, nbuf_ref):
    t_len = xa_ref.shape[0]
    x = xa_ref[...]
    buf = buf_ref[...]
    cw = cw_ref[...]
    row = lax.broadcasted_iota(jnp.int32, x.shape, 0)
    nb = A_CONV - 1
    xc = x * cw[nb:nb + 1]
    for s in range(1, A_CONV):
        xs = pltpu.roll(x, s, 0)
        for r in range(s):
            xs = jnp.where(row == r, buf[nb + r - s:nb + r - s + 1], xs)
        xc = xc + xs * cw[nb - s:nb - s + 1]
    xc = xc + cb_ref[...]
    xcb = xc.astype(BF16)
    r_g = jax.nn.sigmoid(_dot(xcb, wa_ref[...].astype(BF16)) + ba_ref[...])
    i_g = jax.nn.sigmoid(_dot(xcb, wx_ref[...].astype(BF16)) + bx_ref[...])
    nl = -lam_ref[...]
    softplus = jnp.maximum(nl, 0.0) + jnp.log1p(jnp.exp(-jnp.abs(nl)))
    log_a = -LRU_C * r_g * softplus
    a = jnp.exp(log_a)
    u = jnp.sqrt(-jnp.tanh(log_a) * (a * a + 1.0)) * i_g * xc
    d = 1
    while d < t_len:
        keep = row >= d
        a_prev = pltpu.roll(a, d, 0)
        u_prev = pltpu.roll(u, d, 0)
        u = jnp.where(keep, a * u_prev + u, u)
        a = jnp.where(keep, a * a_prev, a)
        d *= 2
    h = a * h0_ref[...] + u
    y_ref[...] = (h * _gelu_tanh(ga_ref[...])).astype(y_ref.dtype)
    ht_ref[...] = h[t_len - 1:t_len]
    nbuf_ref[...] = x[t_len - nb:t_len]


def _into_zeros(kernel_fn, n_in, out_rows, width, dtype):
    def wrapped(*refs):
        return kernel_fn(*refs[:n_in], *refs[n_in + 1:])
    return wrapped, pl.BlockSpec(memory_space=pl.ANY), jnp.zeros((out_rows, width), dtype), {n_in: 0}


def _lru_mixer(z, row0, n_seq, t_len, h0, buf, conv_w, conv_b, wa, ba, wx, bx, lam, out_dtype, out_rows=None):
    assert row0 % t_len == 0 and t_len >= A_CONV - 1 and t_len % 8 == 0
    rb0 = row0 // t_len
    nb = A_CONV - 1
    out_rows = out_rows or n_seq * t_len
    vec = lambda v: v.reshape(1, D_A)
    cspec = pl.BlockSpec((1, A_BLK), lambda s, h: (0, h))
    wspec = pl.BlockSpec((None, A_BLK, A_BLK), lambda s, h: (h, 0, 0))
    body, base_spec, base, alias = _into_zeros(_lru_kernel, 11, out_rows, D_A, out_dtype)
    return pl.pallas_call(
        body,
        out_shape=(jax.ShapeDtypeStruct((out_rows, D_A), out_dtype),
                   jax.ShapeDtypeStruct((n_seq, 1, D_A), F32),
                   jax.ShapeDtypeStruct((n_seq, nb, D_A), F32)),
        grid=(n_seq, A_HEADS),
        in_specs=[pl.BlockSpec((t_len, A_BLK), lambda s, h: (rb0 + s, COL_XA + h)),
                  pl.BlockSpec((t_len, A_BLK), lambda s, h: (rb0 + s, COL_GA + h)),
                  pl.BlockSpec((None, 1, A_BLK), lambda s, h: (s, 0, h)),
                  pl.BlockSpec((None, nb, A_BLK), lambda s, h: (s, 0, h)),
                  pl.BlockSpec((A_CONV, A_BLK), lambda s, h: (0, h)),
                  cspec, wspec, cspec, wspec, cspec, cspec, base_spec],
        out_specs=(pl.BlockSpec((t_len, A_BLK), lambda s, h: (s, h)),
                   pl.BlockSpec((None, 1, A_BLK), lambda s, h: (s, 0, h)),
                   pl.BlockSpec((None, nb, A_BLK), lambda s, h: (s, 0, h))),
        input_output_aliases=alias,
        compiler_params=_params(("parallel", "parallel")),
        name="rglru_mixer",
    )(z, z, h0.reshape(n_seq, 1, D_A), buf, conv_w, vec(conv_b), wa, vec(ba), wx, vec(bx), vec(lam), base)


def _compress_kernel(x_ref, pe_ref, w1_ref, b1_ref, w2_ref, o_ref):
    n_sub = o_ref.shape[0]
    pe = pe_ref[...]
    y0 = jnp.zeros((n_sub, HEAD_DIM), F32)
    y1 = jnp.zeros((n_sub, HEAD_DIM), F32)
    for j in range(CMP_STRIDE):
        xj = x_ref[pl.ds(j, n_sub, stride=CMP_STRIDE), :]
        y0 = y0 + _dot((xj + pe[j:j + 1]).astype(BF16), w1_ref[j])
        y1 = y1 + _dot((xj + pe[CMP_STRIDE + j:CMP_STRIDE + j + 1]).astype(BF16), w1_ref[CMP_STRIDE + j])
    hid = y0 + pltpu.roll(y1, n_sub - 1, 0)
    hid = _gelu_tanh(hid + b1_ref[...])
    o_ref[...] = _dot(hid.astype(BF16), w2_ref[...]).astype(o_ref.dtype)


def _compress(x2d, x_index, n_seq, t_len, pe, w1, b1, w2):
    assert SUB_PER_CMP == 2 and t_len % (8 * CMP_STRIDE) == 0
    n_sub = t_len // CMP_STRIDE
    nt = 2 * KV_HEADS
    return pl.pallas_call(
        _compress_kernel,
        out_shape=jax.ShapeDtypeStruct((n_seq, nt, n_sub, HEAD_DIM), BF16),
        grid=(n_seq, nt),
        in_specs=[pl.BlockSpec((t_len, HEAD_DIM), x_index),
                  pl.BlockSpec((None, CMP_LEN, HEAD_DIM), lambda s, c: (c // KV_HEADS, 0, 0)),
                  pl.BlockSpec((None, CMP_LEN, HEAD_DIM, HEAD_DIM), lambda s, c: (c // KV_HEADS, 0, 0, 0)),
                  pl.BlockSpec((None, 1, HEAD_DIM), lambda s, c: (c // KV_HEADS, 0, 0)),
                  pl.BlockSpec((None, HEAD_DIM, HEAD_DIM), lambda s, c: (c // KV_HEADS, 0, 0))],
        out_specs=pl.BlockSpec((None, None, n_sub, HEAD_DIM), lambda s, c: (s, c, 0, 0)),
        compiler_params=_params(("parallel", "arbitrary")),
        name="nsa_compress",
    )(x2d, pe, w1.astype(BF16), b1.reshape(2, 1, HEAD_DIM), w2.astype(BF16))


def _head_rows(ref, c, n_tok):
    return ref[pl.ds(c, n_tok, stride=2 * KV_HEADS), :]


def _page_gather_kernel(pt_ref, c_ref, o_ref):
    for c in range(2 * KV_HEADS):
        o_ref[c] = _head_rows(c_ref, c, PAGE_SIZE)


def _page_gather(cache_rows, layer, page_table):
    nb, n_pages = page_table.shape
    nt = 2 * KV_HEADS
    return pl.pallas_call(
        _page_gather_kernel,
        out_shape=jax.ShapeDtypeStruct((nb, nt, n_pages * PAGE_SIZE, HEAD_DIM), cache_rows.dtype),
        grid_spec=pltpu.PrefetchScalarGridSpec(
            num_scalar_prefetch=1, grid=(nb, n_pages),
            in_specs=[pl.BlockSpec((None, None, PAGE_SIZE * nt, HEAD_DIM), lambda b, p, pt: (layer, pt[b, p], 0, 0))],
            out_specs=pl.BlockSpec((None, nt, PAGE_SIZE, HEAD_DIM), lambda b, p, pt: (b, 0, p, 0))),
        compiler_params=_params(("parallel", "arbitrary")),
        name="page_gather",
    )(page_table, cache_rows)


def _stack_heads(q, g0):
    return jnp.concatenate([q[:, (g0 + p) * HEAD_DIM:(g0 + p + 1) * HEAD_DIM] for p in range(HPG)], axis=0).astype(BF16)


def _masked_softmax(s, ok):
    sb = jnp.where(ok, s, NEG_INF)
    m = jnp.max(sb, axis=-1, keepdims=True)
    e = jnp.exp(sb - m)
    return e / jnp.sum(e, axis=-1, keepdims=True)


def _cmp_branch(q4, kc, vc, t_rows, slope, n_cmp):
    ncp = kc.shape[0]
    s = _dot_nt(q4, kc) * QK_SCALE
    col = lax.broadcasted_iota(jnp.int32, s.shape, 1)
    dist_i = t_rows - (col * CMP_STRIDE + (CMP_LEN - 1))
    ok = (dist_i >= 0) & (col < n_cmp)
    p = _masked_softmax(s - _lanes(slope, ncp) * dist_i.astype(F32), ok)
    p = jnp.where(ok, p, 0.0)
    return _dot(p.astype(BF16), vc), p


def _select_blocks(imp, pool, t_q, n_sel):
    imp_sel = _dot_f32_exact(imp, pool)
    jcol = lax.broadcasted_iota(jnp.int32, imp_sel.shape, 1)
    cur = lax.shift_right_logical(t_q, int(np.log2(SEL_BLOCK)))
    forced = (jcol == 0) | (jcol == cur) | (jcol == cur - 1)
    valid = jcol * SEL_BLOCK <= t_q
    score = jnp.where(forced, FORCE_SCORE, jnp.where(valid, imp_sel, -1.0))
    score = jnp.where(jcol < n_sel, score, -2.0)
    rank = jnp.zeros(score.shape, F32)
    for i in range(n_sel):
        ci = score[:, i:i + 1]
        before = (ci > score) | ((ci == score) & (jcol > i))
        rank = rank + jnp.where(before, 1.0, 0.0)
    top = min(SEL_TOP, n_sel)
    return jnp.where(rank < float(top), 1.0, 0.0)


def _pool_matrix(n_cmp, ncp, n_sel, nsp):
    assert n_cmp + SUB_PER_CMP - 1 <= n_sel * SUB_PER_SEL
    m = np.zeros((ncp, nsp), np.float32)
    for c in range(n_cmp):
        for n in range(SUB_PER_CMP):
            m[c, (c + n) // SUB_PER_SEL] += 1.0 / SUB_PER_CMP
    return jnp.asarray(m, BF16)


def _slope_rows(t_rep):
    h = np.arange(1, B_HEADS + 1, dtype=np.float64)
    s = np.exp2(-8.0 * h / B_HEADS).astype(np.float32).reshape(KV_HEADS, HPG, 1, 1)
    return jnp.asarray(np.broadcast_to(s, (KV_HEADS, HPG, t_rep, LANE)).reshape(KV_HEADS, HPG * t_rep, LANE))


def _expand_matrix(shape, blk0):
    j = lax.broadcasted_iota(jnp.int32, shape, 0)
    k = lax.broadcasted_iota(jnp.int32, shape, 1)
    return jnp.where(j == blk0 + lax.shift_right_logical(k, int(np.log2(SEL_BLOCK))), 1.0, 0.0).astype(BF16)


def _gate_combine(gs, p, tq, o_c, o_s, o_w):
    rows = slice(p * tq, (p + 1) * tq)
    return (gs[:, p:p + 1] * o_c[rows] + gs[:, HPG + p:HPG + p + 1] * o_s[rows]
            + gs[:, 2 * HPG + p:2 * HPG + p + 1] * o_w[rows])


def _attn_prompt_kernel(q_ref, kc_ref, vc_ref, ks_ref, vs_ref, kw_ref, vw_ref, gt_ref, slope_ref, pool_ref,
                        o_ref, ksb, vsb, kwb, vwb, *, n_cmp, n_sel):
    qi = pl.program_id(2)
    tq = q_ref.shape[0]
    t_total = ks_ref.shape[0]
    rr = HPG * tq

    @pl.when(qi == 0)
    def _():
        ksb[...] = ks_ref[...].astype(BF16)
        vsb[...] = vs_ref[...].astype(BF16)
        kwb[...] = kw_ref[...].astype(BF16)
        vwb[...] = vw_ref[...].astype(BF16)

    t0 = qi * tq
    q4 = _stack_heads(q_ref[...], 0)
    slope = slope_ref[...]

    def t_rows(width):
        r = lax.broadcasted_iota(jnp.int32, (rr, width), 0)
        return t0 + (r & (tq - 1))

    ncp = kc_ref.shape[0]
    o_c, p_c = _cmp_branch(q4, kc_ref[...], vc_ref[...], t_rows(ncp), slope, n_cmp)
    imp = p_c[0:tq]
    for p in range(1, HPG):
        imp = imp + p_c[p * tq:(p + 1) * tq]
    nsp = pool_ref.shape[1]
    t_q = t0 + lax.broadcasted_iota(jnp.int32, (tq, nsp), 0)
    sel = _select_blocks(imp, pool_ref[...], t_q, n_sel)
    sel_bias4 = jnp.concatenate([((sel - 1.0) * -NEG_INF).astype(BF16)] * HPG, axis=0)

    wk = WINDOW + tq
    ck = ATT_CK
    col_w = lax.broadcasted_iota(jnp.int32, (rr, wk), 1)
    cmr_w = col_w - (lax.broadcasted_iota(jnp.int32, (rr, wk), 0) & (tq - 1))
    kbias_w = _lanes(slope, wk) * col_w.astype(F32)
    cmr_ck, kbias_ck = cmr_w[:, :ck], kbias_w[:, :ck]
    slope_col = slope[:, :1]

    def sel_chunk(c, carry):
        m, l, acc = carry
        k0 = pl.multiple_of(c * ck, ck)
        s = _dot_nt(q4, ksb[pl.ds(k0, ck), :])
        logit = s * QK_SCALE + kbias_ck + _dot(sel_bias4, _expand_matrix((nsp, ck), c * (ck // SEL_BLOCK)))
        logit = jnp.where(cmr_ck <= t0 - k0, logit, NEG_INF)
        chunk_bias = slope_col * k0.astype(F32)
        m_new = jnp.maximum(m, jnp.max(logit, axis=-1, keepdims=True) + chunk_bias)
        alpha = jnp.exp(m - m_new)
        e = jnp.exp(logit + (chunk_bias - m_new))
        l = alpha * l + jnp.sum(e, axis=-1, keepdims=True)
        acc = alpha * acc + _dot(e.astype(BF16), vsb[pl.ds(k0, ck), :])
        return m_new, l, acc

    n_chunks = (t0 + tq + ck - 1) // ck
    m0 = jnp.full((rr, 1), NEG_INF, F32)
    _, l_s, acc_s = lax.fori_loop(0, n_chunks, sel_chunk, (m0, jnp.zeros((rr, 1), F32), jnp.zeros((rr, HEAD_DIM), F32)))
    o_s = acc_s / l_s

    w0 = pl.multiple_of(jnp.maximum(t0 - WINDOW, 0), tq)
    s = _dot_nt(q4, kwb[pl.ds(w0, wk), :])
    ok = (cmr_w <= t0 - w0) & (cmr_w > t0 - w0 - WINDOW)
    p_w = _masked_softmax(s * QK_SCALE + kbias_w, ok)
    o_w = _dot(p_w.astype(BF16), vwb[pl.ds(w0, wk), :])

    gs = jax.nn.sigmoid(gt_ref[...])
    for p in range(HPG):
        o_ref[:, p * HEAD_DIM:(p + 1) * HEAD_DIM] = _gate_combine(gs, p, tq, o_c, o_s, o_w).astype(o_ref.dtype)


def _attn_prompt(z, gt, kvc_cmp, n_seq, t_len, out_rows=None):
    tq = ATT_TQ
    assert t_len % ATT_CK == 0 and t_len >= WINDOW + tq and tq & (tq - 1) == 0
    n_cmp = (t_len - CMP_LEN) // CMP_STRIDE + 1
    n_sel = -(-t_len // SEL_BLOCK)
    ncp = kvc_cmp.shape[2]
    nsp = _round_up(n_sel, LANE)
    nq = t_len // tq
    rr = HPG * tq
    kvspec = lambda col0: pl.BlockSpec((t_len, HEAD_DIM), lambda b, g, i: (b, col0 + g))
    out_rows = out_rows or n_seq * t_len
    body, base_spec, base, alias = _into_zeros(functools.partial(_attn_prompt_kernel, n_cmp=n_cmp, n_sel=n_sel),
                                               10, out_rows, D_B, BF16)
    return pl.pallas_call(
        body,
        out_shape=jax.ShapeDtypeStruct((out_rows, D_B), BF16),
        grid=(n_seq, KV_HEADS, nq),
        in_specs=[pl.BlockSpec((tq, HPG * HEAD_DIM), lambda b, g, i: (b * nq + i, COL_Q // HPG + g)),
                  pl.BlockSpec((None, None, ncp, HEAD_DIM), lambda b, g, i: (b, g, 0, 0)),
                  pl.BlockSpec((None, None, ncp, HEAD_DIM), lambda b, g, i: (b, KV_HEADS + g, 0, 0)),
                  kvspec(COL_KVS), kvspec(COL_KVS + KV_HEADS), kvspec(COL_KVW), kvspec(COL_KVW + KV_HEADS),
                  pl.BlockSpec((tq, LANE), lambda b, g, i: (b * nq + i, g)),
                  pl.BlockSpec((None, rr, LANE), lambda b, g, i: (g, 0, 0)),
                  pl.BlockSpec((ncp, nsp), lambda b, g, i: (0, 0)), base_spec],
        out_specs=pl.BlockSpec((tq, HPG * HEAD_DIM), lambda b, g, i: (b * nq + i, g)),
        scratch_shapes=[pltpu.VMEM((t_len, HEAD_DIM), BF16)] * 4,
        input_output_aliases=alias,
        compiler_params=_params(("parallel", "parallel", "arbitrary")),
        name="nsa_prompt_attention",
    )(z, kvc_cmp, kvc_cmp, z, z, z, z, gt, _slope_rows(tq), _pool_matrix(n_cmp, ncp, n_sel, nsp), base)


def _attn_dec_select_kernel(q_ref, kc_ref, vc_ref, slope_ref, pool_ref, oc_ref, sel_ref, *, n_cmp, n_sel, past_len):
    ts = q_ref.shape[0]
    rr = HPG * ts
    ncp = kc_ref.shape[0]
    q4 = _stack_heads(q_ref[...], 0)
    t_rows = past_len + (lax.broadcasted_iota(jnp.int32, (rr, ncp), 0) & (ts - 1))
    o_c, p_c = _cmp_branch(q4, kc_ref[...], vc_ref[...], t_rows, slope_ref[...], n_cmp)
    imp = p_c[0:ts]
    for p in range(1, HPG):
        imp = imp + p_c[p * ts:(p + 1) * ts]
    nsp = pool_ref.shape[1]
    t_q = past_len + lax.broadcasted_iota(jnp.int32, (ts, nsp), 0)
    oc_ref[...] = o_c
    sel_ref[...] = _select_blocks(imp, pool_ref[...], t_q, n_sel)


def _attn_dec_select(z, row0, kvc_cmp, n_seq, ts, past_len, n_cmp, n_sel):
    assert row0 % ts == 0 and ts & (ts - 1) == 0 and ts % 8 == 0
    rb0 = row0 // ts
    ncp = kvc_cmp.shape[2]
    nsp = _round_up(n_sel, LANE)
    rr = HPG * ts
    return pl.pallas_call(
        functools.partial(_attn_dec_select_kernel, n_cmp=n_cmp, n_sel=n_sel, past_len=past_len),
        out_shape=(jax.ShapeDtypeStruct((n_seq, KV_HEADS, rr, HEAD_DIM), F32),
                   jax.ShapeDtypeStruct((n_seq, KV_HEADS, ts, nsp), F32)),
        grid=(n_seq, KV_HEADS),
        in_specs=[pl.BlockSpec((ts, HPG * HEAD_DIM), lambda b, g: (rb0 + b, COL_Q // HPG + g)),
                  pl.BlockSpec((None, None, ncp, HEAD_DIM), lambda b, g: (b, g, 0, 0)),
                  pl.BlockSpec((None, None, ncp, HEAD_DIM), lambda b, g: (b, KV_HEADS + g, 0, 0)),
                  pl.BlockSpec((None, rr, LANE), lambda b, g: (g, 0, 0)),
                  pl.BlockSpec((ncp, nsp), lambda b, g: (0, 0))],
        out_specs=(pl.BlockSpec((None, None, rr, HEAD_DIM), lambda b, g: (b, g, 0, 0)),
                   pl.BlockSpec((None, None, ts, nsp), lambda b, g: (b, g, 0, 0))),
        compiler_params=_params(("parallel", "parallel")),
        name="nsa_decode_select",
    )(z, kvc_cmp, kvc_cmp, _slope_rows(ts), _pool_matrix(n_cmp, ncp, n_sel, nsp))


def _attn_dec_kernel(pt_ref, q_ref, sel_ref, *rest, past_len, n_steps, pages_per_step):
    page_refs = rest[:pages_per_step]
    kvs_ref, kvw_ref, win_ref, oc_ref, gt_ref, slope_ref, o_ref, nwin_ref, m_sc, l_sc, acc_sc = rest[pages_per_step:]
    pg = pl.program_id(1)
    ts = q_ref.shape[0]
    rr = HPG * ts
    nsp = sel_ref.shape[-1]
    pad_rows = LANE - ts
    row = lax.broadcasted_iota(jnp.int32, (rr, LANE), 0)
    col = lax.broadcasted_iota(jnp.int32, (rr, LANE), 1)
    t_rows = past_len + (row & (ts - 1))

    @pl.when(pg == 0)
    def _():
        m_sc[...] = jnp.full(m_sc.shape, NEG_INF, F32)
        l_sc[...] = jnp.zeros(l_sc.shape, F32)
        acc_sc[...] = jnp.zeros(acc_sc.shape, F32)

    def sel_update(g, q4, kk, vv, dist_i, expand, extra_ok):
        width = kk.shape[0]
        s = _dot_nt(q4, kk) * QK_SCALE
        sel4 = jnp.concatenate([sel_ref[g].astype(BF16)] * HPG, axis=0)
        ok = (dist_i >= 0) & (_dot(sel4, expand) > 0.5) & extra_ok
        sb = jnp.where(ok, s - _lanes(slope_ref[g], width) * dist_i.astype(F32), NEG_INF)
        m_old = m_sc[g]
        m_new = jnp.maximum(m_old, jnp.max(sb, axis=-1, keepdims=True))
        alpha = jnp.exp(m_old - m_new)
        e = jnp.exp(sb - m_new)
        l_sc[g] = alpha * l_sc[g] + jnp.sum(e, axis=-1, keepdims=True)
        acc_sc[g] = alpha * acc_sc[g] + _dot(e.astype(BF16), vv)
        m_sc[g] = m_new

    @pl.when(pg < n_steps)
    def _():
        width = pages_per_step * PAGE_SIZE
        k0 = pg * width
        expand = _expand_matrix((nsp, width), pg * (width // SEL_BLOCK))
        wrow_ = lax.broadcasted_iota(jnp.int32, (rr, width), 0)
        wcol_ = lax.broadcasted_iota(jnp.int32, (rr, width), 1)
        dist_i = (past_len + (wrow_ & (ts - 1))) - (k0 + wcol_)
        for g in range(KV_HEADS):
            q4 = _stack_heads(q_ref[...], g * HPG)
            kk = jnp.concatenate([_head_rows(r, g, PAGE_SIZE) for r in page_refs], axis=0).astype(BF16)
            vv = jnp.concatenate([_head_rows(r, KV_HEADS + g, PAGE_SIZE) for r in page_refs], axis=0).astype(BF16)
            sel_update(g, q4, kk, vv, dist_i, expand, True)

    @pl.when(pg == n_steps)
    def _():
        zpad = jnp.zeros((pad_rows, HEAD_DIM), F32)
        gs = jax.nn.sigmoid(gt_ref[...])
        jj = lax.broadcasted_iota(jnp.int32, (nsp, LANE), 0)
        kk_i = lax.broadcasted_iota(jnp.int32, (nsp, LANE), 1)
        new_blk = lax.shift_right_logical(past_len + kk_i, int(np.log2(SEL_BLOCK)))
        expand_new = jnp.where(jj == new_blk, 1.0, 0.0).astype(BF16)
        is_new = col < ts
        nt = 2 * KV_HEADS
        wlen = win_ref.shape[0] // nt
        wrow = lax.broadcasted_iota(jnp.int32, (rr, wlen), 0)
        wcol = lax.broadcasted_iota(jnp.int32, (rr, wlen), 1)
        dist_w = (past_len + (wrow & (ts - 1))) - (past_len - wlen + wcol)
        ok_w = (dist_w >= 0) & (dist_w < WINDOW)
        dist_n = t_rows - (past_len + col)
        ok_n = (dist_n >= 0) & (dist_n < WINDOW) & is_new
        for g in range(KV_HEADS):
            q4 = _stack_heads(q_ref[...], g * HPG)
            ksl = slice(g * HEAD_DIM, (g + 1) * HEAD_DIM)
            vsl = slice(D_KV + g * HEAD_DIM, D_KV + (g + 1) * HEAD_DIM)
            kn = jnp.concatenate([kvs_ref[:, ksl], zpad], axis=0).astype(BF16)
            vn = jnp.concatenate([kvs_ref[:, vsl], zpad], axis=0).astype(BF16)
            sel_update(g, q4, kn, vn, dist_n, expand_new, is_new)
            o_s = acc_sc[g] / l_sc[g]
            slope = slope_ref[g]
            s1 = _dot_nt(q4, _head_rows(win_ref, g, wlen).astype(BF16)) * QK_SCALE
            s1 = jnp.where(ok_w, s1 - _lanes(slope, wlen) * dist_w.astype(F32), NEG_INF)
            kwn = jnp.concatenate([kvw_ref[:, ksl], zpad], axis=0).astype(BF16)
            vwn = jnp.concatenate([kvw_ref[:, vsl], zpad], axis=0).astype(BF16)
            s2 = _dot_nt(q4, kwn) * QK_SCALE
            s2 = jnp.where(ok_n, s2 - slope * dist_n.astype(F32), NEG_INF)
            m = jnp.maximum(jnp.max(s1, axis=-1, keepdims=True), jnp.max(s2, axis=-1, keepdims=True))
            e1 = jnp.exp(s1 - m)
            e2 = jnp.exp(s2 - m)
            inv = 1.0 / (jnp.sum(e1, axis=-1, keepdims=True) + jnp.sum(e2, axis=-1, keepdims=True))
            o_w = (_dot((e1 * inv).astype(BF16), _head_rows(win_ref, KV_HEADS + g, wlen).astype(BF16))
                   + _dot((e2 * inv).astype(BF16), vwn))
            o_c = oc_ref[g]
            for p in range(HPG):
                c0 = (g * HPG + p) * HEAD_DIM
                o_ref[:, c0:c0 + HEAD_DIM] = _gate_combine(gs[:, g * LANE:(g + 1) * LANE], p, ts, o_c, o_s, o_w)
        keep = (wlen - ts) * nt
        nwin_ref[0:keep, :] = win_ref[ts * nt:wlen * nt, :]
        for c in range(nt):
            nwin_ref[pl.ds(keep + c, ts, stride=nt), :] = kvw_ref[:, c * HEAD_DIM:(c + 1) * HEAD_DIM]


def _attn_decode(z, gt, row0, n_seq, ts, page_table, cache_rows, layer, win_rows, o_c, sel, past_len):
    n_pages = page_table.shape[1]
    pps = 4 if n_pages % 4 == 0 else 1
    n_steps = n_pages // pps
    rb0 = row0 // ts
    rr = HPG * ts
    nsp = sel.shape[-1]
    nt = 2 * KV_HEADS
    wrows = win_rows.shape[2]
    w = 2 * D_KV
    assert wrows == WINDOW * nt and past_len % SEL_BLOCK == 0 and PAGE_SIZE == LANE
    last = n_steps - 1

    def page_spec(k):
        return pl.BlockSpec((None, None, PAGE_SIZE * nt, HEAD_DIM),
                            lambda b, p, pt: (layer, pt[b, jnp.minimum(p, last) * pps + k], 0, 0))

    return pl.pallas_call(
        functools.partial(_attn_dec_kernel, past_len=past_len, n_steps=n_steps, pages_per_step=pps),
        out_shape=(jax.ShapeDtypeStruct((n_seq * ts, D_B), F32),
                   jax.ShapeDtypeStruct((n_seq, wrows, HEAD_DIM), F32)),
        grid_spec=pltpu.PrefetchScalarGridSpec(
            num_scalar_prefetch=1, grid=(n_seq, n_steps + 1),
            in_specs=[pl.BlockSpec((ts, D_B), lambda b, p, pt: (rb0 + b, COL_Q * LANE // D_B)),
                      pl.BlockSpec((None, KV_HEADS, ts, nsp), lambda b, p, pt: (b, 0, 0, 0))]
                     + [page_spec(k) for k in range(pps)]
                     + [pl.BlockSpec((ts, w), lambda b, p, pt: (rb0 + b, COL_KVS * LANE // w)),
                        pl.BlockSpec((ts, w), lambda b, p, pt: (rb0 + b, COL_KVW * LANE // w)),
                        pl.BlockSpec((None, None, wrows, HEAD_DIM), lambda b, p, pt: (layer, b, 0, 0)),
                        pl.BlockSpec((None, KV_HEADS, rr, HEAD_DIM), lambda b, p, pt: (b, 0, 0, 0)),
                        pl.BlockSpec((ts, KV_HEADS * LANE), lambda b, p, pt: (rb0 + b, 0)),
                        pl.BlockSpec((KV_HEADS, rr, LANE), lambda b, p, pt: (0, 0, 0))],
            out_specs=(pl.BlockSpec((ts, D_B), lambda b, p, pt: (b, 0)),
                       pl.BlockSpec((None, wrows, HEAD_DIM), lambda b, p, pt: (b, 0, 0))),
            scratch_shapes=[pltpu.VMEM((KV_HEADS, rr, 1), F32), pltpu.VMEM((KV_HEADS, rr, 1), F32),
                            pltpu.VMEM((KV_HEADS, rr, HEAD_DIM), F32)]),
        compiler_params=_params(("parallel", "arbitrary")),
        name="nsa_decode_attention",
    )(page_table, z, sel, *([cache_rows] * pps), z, z, win_rows, o_c, gt, _slope_rows(ts))


def _sconv_kernel(bg_ref, cg_ref, v_ref, buf_ref, cw_ref, y_ref, nbuf_ref):
    t_len = cg_ref.shape[0]
    nb = C_CONV - 1
    cv = cg_ref[...] * v_ref[...]
    buf = buf_ref[...]
    cw = cw_ref[...]
    row = lax.broadcasted_iota(jnp.int32, cv.shape, 0)
    y = cv * cw[nb:nb + 1]
    for s in range(1, C_CONV):
        xs = pltpu.roll(cv, s, 0)
        for r in range(s):
            xs = jnp.where(row == r, buf[nb + r - s:nb + r - s + 1], xs)
        y = y + xs * cw[nb - s:nb - s + 1]
    y_ref[...] = (bg_ref[...] * y).astype(y_ref.dtype)
    nbuf_ref[...] = cv[t_len - nb:t_len]


def _sconv_mixer(zc, row0, n_seq, t_len, buf, conv_w, out_dtype, out_rows=None):
    assert row0 % t_len == 0 and t_len >= C_CONV - 1 and t_len % 8 == 0
    rb0 = row0 // t_len
    nb = C_CONV - 1
    tc = 256
    nc = D_C // tc
    zspec = lambda third: pl.BlockSpec((t_len, tc), lambda s, c: (rb0 + s, third * nc + c))
    out_rows = out_rows or n_seq * t_len
    body, base_spec, base, alias = _into_zeros(_sconv_kernel, 5, out_rows, D_C, out_dtype)
    return pl.pallas_call(
        body,
        out_shape=(jax.ShapeDtypeStruct((out_rows, D_C), out_dtype),
                   jax.ShapeDtypeStruct((n_seq, nb, D_C), F32)),
        grid=(n_seq, nc),
        in_specs=[zspec(0), zspec(1), zspec(2),
                  pl.BlockSpec((None, nb, tc), lambda s, c: (s, 0, c)),
                  pl.BlockSpec((C_CONV, tc), lambda s, c: (0, c)), base_spec],
        out_specs=(pl.BlockSpec((t_len, tc), lambda s, c: (s, c)),
                   pl.BlockSpec((None, nb, tc), lambda s, c: (s, 0, c))),
        input_output_aliases=alias,
        compiler_params=_params(("parallel", "parallel")),
        name="sconv_mixer",
    )(zc, zc, zc, buf, conv_w, base)


def _gate_weight(w_in):
    nl, k = w_in.shape[:2]
    wg = w_in[:, :, D_IN_MAIN:].reshape(nl, k, 3, KV_HEADS, HPG).transpose(0, 1, 3, 2, 4)
    wg = jnp.pad(wg.reshape(nl, k, KV_HEADS, 3 * HPG), ((0, 0), (0, 0), (0, 0), (0, LANE - 3 * HPG)))
    return wg.reshape(nl, k, KV_HEADS * LANE).astype(BF16)


def _ab_layer(h, hn, i, mp, bp, t_len, bs, ts, past_len, w, cache_kv_cmp, cache_kv_sel, state_win_kv, state_lru_h,
              state_lru_conv, page_table):
    z, kv_rows = _in_proj(hn, w["ab_w_in_bf16"], i)
    gt = _matmul(hn, w["ab_w_gate_bf16"], i, name="ab_gate_proj")
    nt = 2 * KV_HEADS
    rows_view = lambda a: a.reshape(a.shape[0], a.shape[1], -1, HEAD_DIM)
    lru_w = (w["ab_conv_w"][i], w["ab_conv_b"][i], w["ab_gate_a_w"][i], w["ab_gate_a_b"][i], w["ab_gate_x_w"][i],
             w["ab_gate_x_b"][i], w["ab_lru_lambda"][i])
    cmp_w = (w["ab_cmp_pe"][i], w["ab_cmp_w1"][i], w["ab_cmp_b1"][i], w["ab_cmp_w2"][i])
    n_pages = page_table.shape[1]
    tk = past_len + ts
    n_cmp_s = (tk - CMP_LEN) // CMP_STRIDE + 1
    n_sel_s = -(-tk // SEL_BLOCK)
    assert n_cmp_s == (past_len - CMP_LEN) // CMP_STRIDE + 1

    ya_p, hT_p, nbuf_p = _lru_mixer(z, 0, bp, t_len, jnp.zeros((bp, D_A), F32), jnp.zeros((bp, A_CONV - 1, D_A), F32),
                                    *lru_w, out_dtype=BF16, out_rows=mp + bs * ts)
    kvc_p = _compress(z, lambda s, c: (s, COL_KVC + c), bp, t_len, *cmp_w)
    ob_p = _attn_prompt(z, gt, kvc_p, bp, t_len, out_rows=mp + bs * ts)

    ya_s, hT_s, nbuf_s = _lru_mixer(z, mp, bs, ts, state_lru_h[i], state_lru_conv[i], *lru_w, out_dtype=F32)
    past_cmp = _page_gather(rows_view(cache_kv_cmp), i, page_table)
    kvc_s = _compress(past_cmp.reshape(bs * nt * past_len, HEAD_DIM), lambda s, c: (s * nt + c, 0), bs, past_len, *cmp_w)
    oc_s, sel_s = _attn_dec_select(z, mp, kvc_s, bs, ts, past_len, n_cmp_s, n_sel_s)
    ob_s, nwin_s = _attn_decode(z, gt, mp, bs, ts, page_table, rows_view(cache_kv_sel), i, rows_view(state_win_kv),
                                oc_s, sel_s, past_len)

    ya = lax.dynamic_update_slice(ya_p, ya_s.astype(BF16), (mp, 0))
    ob = lax.dynamic_update_slice(ob_p, ob_s.astype(BF16), (mp, 0))
    h = _matmul2_residual(ya, ob, w["ab_w_out_bf16"], i, h, name="ab_out_proj")

    kv5 = lambda a, nb_, tt: a.reshape(nb_, tt, 2, KV_HEADS, HEAD_DIM)
    wl = min(WINDOW, t_len)
    kvw_p = kv5(kv_rows[2, :mp * nt], bp, t_len)
    outs_p = (kv5(kv_rows[0, :mp * nt], bp, t_len), kv5(kv_rows[1, :mp * nt], bp, t_len),
              kvw_p[:, t_len - wl:], hT_p.reshape(bp, D_A), nbuf_p)
    outs_s = (kv5(kv_rows[0, mp * nt:], bs, ts), kv5(kv_rows[1, mp * nt:], bs, ts),
              kv5(nwin_s, bs, WINDOW), hT_s.reshape(bs, D_A), nbuf_s)
    return h, outs_p, outs_s


def _c_layer(h, hn, i, mp, bp, t_len, bs, ts, w, state_sconv):
    zc = _matmul(hn, w["c_w_in"], i, name="c_in_proj", row_cap=WIDE_ROW_CAP, col_cap=F32_WEIGHT_COL_CAP)
    y_p, nbuf_p = _sconv_mixer(zc, 0, bp, t_len, jnp.zeros((bp, C_CONV - 1, D_C), F32), w["c_conv_w"][i], BF16,
                               out_rows=mp + bs * ts)
    y_s, nbuf_s = _sconv_mixer(zc, mp, bs, ts, state_sconv[i], w["c_conv_w"][i], F32)
    y = lax.dynamic_update_slice(y_p, y_s.astype(BF16), (mp, 0))
    h = _matmul_residual(y, w["c_w_out_bf16"], i, h, name="c_out_proj")
    return h, nbuf_p, nbuf_s


def kernel(x_prompt, x_sample, cache_kv_cmp, cache_kv_sel, state_win_kv, state_lru_h, state_lru_conv, state_sconv,
           page_table, norm_mix, norm_ffn, norm_final, ab_w_in, ab_conv_w, ab_conv_b, ab_gate_a_w, ab_gate_a_b,
           ab_gate_x_w, ab_gate_x_b, ab_lru_lambda, ab_cmp_pe, ab_cmp_w1, ab_cmp_b1, ab_cmp_w2, ab_w_out,
           c_w_in, c_conv_w, c_w_out, ffn_w_gate, ffn_w_up, ffn_w_down):
    w = dict(ab_w_in_bf16=ab_w_in.astype(BF16), ab_w_gate_bf16=_gate_weight(ab_w_in), ab_conv_w=ab_conv_w,
             ab_conv_b=ab_conv_b, ab_gate_a_w=ab_gate_a_w, ab_gate_a_b=ab_gate_a_b, ab_gate_x_w=ab_gate_x_w,
             ab_gate_x_b=ab_gate_x_b, ab_lru_lambda=ab_lru_lambda, ab_cmp_pe=ab_cmp_pe, ab_cmp_w1=ab_cmp_w1,
             ab_cmp_b1=ab_cmp_b1, ab_cmp_w2=ab_cmp_w2, ab_w_out_bf16=ab_w_out.astype(BF16),
             c_w_in=c_w_in, c_conv_w=c_conv_w, c_w_out_bf16=c_w_out.astype(BF16))
    ffn_w_down_bf16 = ffn_w_down.astype(BF16)
    bp, t_len = x_prompt.shape[:2]
    bs, ts = x_sample.shape[:2]
    mp, ms = bp * t_len, bs * ts
    past_len = page_table.shape[1] * PAGE_SIZE
    h = jnp.concatenate([x_prompt.reshape(mp, D_MODEL), x_sample.reshape(ms, D_MODEL)], axis=0)
    p_ab, s_ab, p_c, s_c = [], [], [], []
    for layer in range(DEPTH):
        hn = _rmsnorm(h, norm_mix[layer], BF16)
        i = layer // 2
        if layer % 2 == 0:
            h, op, os_ = _ab_layer(h, hn, i, mp, bp, t_len, bs, ts, past_len, w, cache_kv_cmp, cache_kv_sel,
                                   state_win_kv, state_lru_h, state_lru_conv, page_table)
            p_ab.append(op)
            s_ab.append(os_)
        else:
            h, nbp, nbs = _c_layer(h, hn, i, mp, bp, t_len, bs, ts, w, state_sconv)
            p_c.append(nbp)
            s_c.append(nbs)
        hf = _rmsnorm(h, norm_ffn[layer], BF16)
        hid = _ffn_up(hf, ffn_w_gate, ffn_w_up, layer)
        h = _ffn_down(hid, ffn_w_down_bf16, layer, h)
    y = _rmsnorm(h, norm_final, F32)
    y_prompt = y[:mp].reshape(bp, t_len, D_MODEL)
    y_sample = y[mp:].reshape(bs, ts, D_MODEL)
    stack = lambda rows, j: jnp.stack([r[j] for r in rows])
    return (y_prompt, y_sample,
            stack(p_ab, 0), stack(p_ab, 1), stack(p_ab, 2), stack(p_ab, 3), stack(p_ab, 4), jnp.stack(p_c),
            stack(s_ab, 0), stack(s_ab, 1), stack(s_ab, 2), stack(s_ab, 3), stack(s_ab, 4), jnp.stack(s_c))
```

```python
import functools

import jax
import jax.numpy as jnp
import numpy as np
from jax import lax
from jax.experimental import pallas as pl
from jax.experimental.pallas import tpu as pltpu

D_MODEL = 4096
DEPTH = 4
PAGE_SIZE = 128
N_AB = (DEPTH + 1) // 2
N_C = DEPTH // 2
D_A = D_MODEL // 2
A_HEADS = 16
A_BLK = D_A // A_HEADS
A_CONV = 4
LRU_C = 8.0
B_HEADS = 16
HEAD_DIM = 128
KV_HEADS = 4
HPG = B_HEADS // KV_HEADS
D_B = B_HEADS * HEAD_DIM
D_KV = KV_HEADS * HEAD_DIM
CMP_LEN = 32
CMP_STRIDE = 16
SUB_PER_CMP = CMP_LEN // CMP_STRIDE
SEL_BLOCK = 64
SUB_PER_SEL = SEL_BLOCK // CMP_STRIDE
SEL_TOP = 16
WINDOW = 512
D_C = D_MODEL
C_CONV = 3
D_FF = -(-8 * D_MODEL // (3 * 256)) * 256
D_IN_MAIN = 2 * D_A + D_B + 6 * D_KV
N_GATES = 3 * B_HEADS
RMS_EPS = 1e-6
NEG_INF = -1e30
FORCE_SCORE = 1e4
QK_SCALE = HEAD_DIM ** -0.5

COL_XA = 0
COL_GA = D_A // 128
COL_Q = 2 * D_A // 128
COL_KVC = (2 * D_A + D_B) // 128
COL_KVS = COL_KVC + 2 * D_KV // 128
COL_KVW = COL_KVS + 2 * D_KV // 128

V7X_VMEM_LIMIT_BYTES = 56 * 1024 * 1024
LANE = 128
ATT_TQ = 128
ATT_CK = 512
F32 = jnp.float32
BF16 = jnp.bfloat16


def _row_tile(m, cap=1024):
    for t in range(cap - cap % 16, 15, -16):
        if m % t == 0:
            return t
    raise ValueError(m)


def _col_tile(n, cap=512):
    for t in range(cap, LANE - 1, -LANE):
        if n % t == 0:
            return t
    raise ValueError(n)


def _round_up(x, m):
    return -(-x // m) * m


def _params(sem):
    return pltpu.CompilerParams(dimension_semantics=sem, vmem_limit_bytes=V7X_VMEM_LIMIT_BYTES)


def _dot(a, b):
    return jnp.dot(a, b, preferred_element_type=F32)


def _dot_nt(a, b):
    return lax.dot_general(a, b, (((1,), (1,)), ((), ())), preferred_element_type=F32)


def _dot_f32_exact(x, w_bf16):
    hi = x.astype(BF16)
    r1 = x - hi.astype(F32)
    mid = r1.astype(BF16)
    lo = (r1 - mid.astype(F32)).astype(BF16)
    return _dot(hi, w_bf16) + (_dot(mid, w_bf16) + _dot(lo, w_bf16))


def _gelu_tanh(x):
    return 0.5 * x * (1.0 + jnp.tanh(0.7978845608028654 * (x + 0.044715 * (x * x * x))))


def _lanes(x128, w):
    return x128 if w == LANE else jnp.concatenate([x128] * (w // LANE), axis=1)


def _rmsnorm_kernel(x_ref, g_ref, o_ref):
    x = x_ref[...]
    y = x * lax.rsqrt(jnp.mean(x * x, axis=-1, keepdims=True) + RMS_EPS)
    o_ref[...] = (y * g_ref[...]).astype(o_ref.dtype)


def _rmsnorm(x, g, out_dtype):
    m, d = x.shape
    tm = _row_tile(m)
    return pl.pallas_call(
        _rmsnorm_kernel,
        out_shape=jax.ShapeDtypeStruct((m, d), out_dtype),
        grid=(m // tm,),
        in_specs=[pl.BlockSpec((tm, d), lambda i: (i, 0)), pl.BlockSpec((1, d), lambda i: (0, 0))],
        out_specs=pl.BlockSpec((tm, d), lambda i: (i, 0)),
        compiler_params=_params(("parallel",)),
        name="rmsnorm",
    )(x, g.reshape(1, d))


def _wspec(layer, k, tn):
    return pl.BlockSpec((None, k, tn), lambda i, j: (layer, 0, j))


def _mm_kernel(x_ref, w_ref, o_ref):
    o_ref[...] = _dot(x_ref[...], w_ref[...].astype(BF16)).astype(o_ref.dtype)


def _matmul(x, w, layer, out_dtype=F32, name="matmul", row_cap=1024, col_cap=512):
    m, k = x.shape
    n = w.shape[2]
    tm, tn = _row_tile(m, row_cap), _col_tile(n, col_cap)
    return pl.pallas_call(
        _mm_kernel,
        out_shape=jax.ShapeDtypeStruct((m, n), out_dtype),
        grid=(m // tm, n // tn),
        in_specs=[pl.BlockSpec((tm, k), lambda i, j: (i, 0)), _wspec(layer, k, tn)],
        out_specs=pl.BlockSpec((tm, tn), lambda i, j: (i, j)),
        compiler_params=_params(("parallel", "arbitrary")),
        name=name,
    )(x, w)


def _in_proj_kernel(x_ref, w_ref, z_ref, kv_ref, *, j_kv0, tiles_per_seg):
    j = pl.program_id(1)
    tm, tn = z_ref.shape
    heads_per_tile = tn // HEAD_DIM
    rows_per_token = 2 * KV_HEADS
    acc = _dot(x_ref[...], w_ref[...].astype(BF16))
    z_ref[...] = acc
    for part in range(tiles_per_seg):
        @pl.when((j >= j_kv0) & ((j - j_kv0) % tiles_per_seg == part))
        def _():
            for c in range(heads_per_tile):
                kv_ref[pl.ds(part * heads_per_tile + c, tm, stride=rows_per_token), :] = (
                    acc[:, c * HEAD_DIM:(c + 1) * HEAD_DIM])


def _in_proj(x, w, layer):
    m, k = x.shape
    n = D_IN_MAIN
    tm, tn = _row_tile(m, WIDE_ROW_CAP), _col_tile(n, F32_WEIGHT_COL_CAP)
    seg = 2 * D_KV
    assert seg % tn == 0 and (COL_KVC * LANE) % tn == 0 and n == COL_KVC * LANE + 3 * seg and w.shape[2] >= n
    tiles_per_seg = seg // tn
    j_kv0 = COL_KVC * LANE // tn
    rpt = 2 * KV_HEADS
    return pl.pallas_call(
        functools.partial(_in_proj_kernel, j_kv0=j_kv0, tiles_per_seg=tiles_per_seg),
        out_shape=(jax.ShapeDtypeStruct((m, n), F32), jax.ShapeDtypeStruct((3, m * rpt, HEAD_DIM), F32)),
        grid=(m // tm, n // tn),
        in_specs=[pl.BlockSpec((tm, k), lambda i, j: (i, 0)), _wspec(layer, k, tn)],
        out_specs=(pl.BlockSpec((tm, tn), lambda i, j: (i, j)),
                   pl.BlockSpec((None, tm * rpt, HEAD_DIM),
                                lambda i, j: (jnp.maximum(j - j_kv0, 0) // tiles_per_seg, i, 0))),
        compiler_params=_params(("parallel", "arbitrary")),
        name="ab_in_proj",
    )(x, w)


def _mm_res_kernel(x_ref, w_ref, r_ref, o_ref):
    o_ref[...] = r_ref[...] + _dot(x_ref[...], w_ref[...].astype(BF16))


def _matmul_residual(x, w, layer, r, name="matmul_res"):
    m, k = x.shape
    n = w.shape[2]
    tm, tn = _row_tile(m), _col_tile(n)
    return pl.pallas_call(
        _mm_res_kernel,
        out_shape=jax.ShapeDtypeStruct((m, n), F32),
        grid=(m // tm, n // tn),
        in_specs=[pl.BlockSpec((tm, k), lambda i, j: (i, 0)), _wspec(layer, k, tn),
                  pl.BlockSpec((tm, tn), lambda i, j: (i, j))],
        out_specs=pl.BlockSpec((tm, tn), lambda i, j: (i, j)),
        compiler_params=_params(("parallel", "arbitrary")),
        name=name,
    )(x, w, r)


def _mm2_res_kernel(x1_ref, x2_ref, w1_ref, w2_ref, r_ref, o_ref):
    o_ref[...] = r_ref[...] + (_dot(x1_ref[...], w1_ref[...].astype(BF16))
                               + _dot(x2_ref[...], w2_ref[...].astype(BF16)))


def _matmul2_residual(x1, x2, w, layer, r, name="matmul2_res"):
    m, k1 = x1.shape
    k2 = x2.shape[1]
    n = w.shape[2]
    tm, tn = _row_tile(m), _col_tile(n)
    assert k1 % 8 == 0 and k1 == k2
    return pl.pallas_call(
        _mm2_res_kernel,
        out_shape=jax.ShapeDtypeStruct((m, n), F32),
        grid=(m // tm, n // tn),
        in_specs=[pl.BlockSpec((tm, k1), lambda i, j: (i, 0)), pl.BlockSpec((tm, k2), lambda i, j: (i, 0)),
                  pl.BlockSpec((None, k1, tn), lambda i, j: (layer, 0, j)),
                  pl.BlockSpec((None, k2, tn), lambda i, j: (layer, 1, j)),
                  pl.BlockSpec((tm, tn), lambda i, j: (i, j))],
        out_specs=pl.BlockSpec((tm, tn), lambda i, j: (i, j)),
        compiler_params=_params(("parallel", "arbitrary")),
        name=name,
    )(x1, x2, w, w, r)


def _ffn_up_kernel(x_ref, wg_ref, wu_ref, o_ref):
    x = x_ref[...]
    g = _dot(x, wg_ref[...].astype(BF16))
    u = _dot(x, wu_ref[...].astype(BF16))
    o_ref[...] = (g * jax.nn.sigmoid(g) * u).astype(o_ref.dtype)


WIDE_ROW_CAP = 1536
F32_WEIGHT_COL_CAP = 256


def _ffn_up(x, wg, wu, layer):
    m, k = x.shape
    n = wg.shape[2]
    tm, tn = _row_tile(m, WIDE_ROW_CAP), _col_tile(n, F32_WEIGHT_COL_CAP)
    return pl.pallas_call(
        _ffn_up_kernel,
        out_shape=jax.ShapeDtypeStruct((m, n), BF16),
        grid=(m // tm, n // tn),
        in_specs=[pl.BlockSpec((tm, k), lambda i, j: (i, 0)), _wspec(layer, k, tn), _wspec(layer, k, tn)],
        out_specs=pl.BlockSpec((tm, tn), lambda i, j: (i, j)),
        compiler_params=_params(("parallel", "arbitrary")),
        name="ffn_up",
    )(x, wg, wu)


def _ffn_down_kernel(x_ref, w_ref, r_ref, o_ref, acc_ref):
    kk = pl.program_id(1)
    j = pl.program_id(2)
    part = _dot(x_ref[...], w_ref[...])

    @pl.when(kk == 0)
    def _():
        acc_ref[j] = r_ref[...] + part

    @pl.when(kk == pl.num_programs(1) - 1)
    def _():
        o_ref[...] = acc_ref[j] + part


def _ffn_down(h, wd, layer, r):
    m, k = h.shape
    n = wd.shape[2]
    tm, tn = _row_tile(m), _col_tile(n)
    nk = 2
    tk = k // nk
    assert tk * nk == k and tk % LANE == 0
    return pl.pallas_call(
        _ffn_down_kernel,
        out_shape=jax.ShapeDtypeStruct((m, n), F32),
        grid=(m // tm, nk, n // tn),
        in_specs=[pl.BlockSpec((tm, tk), lambda i, kk, j: (i, kk)),
                  pl.BlockSpec((None, tk, tn), lambda i, kk, j: (layer, kk, j)),
                  pl.BlockSpec((tm, tn), lambda i, kk, j: (i, jnp.where(kk == 0, j, 0)))],
        out_specs=pl.BlockSpec((tm, tn), lambda i, kk, j: (i, jnp.where(kk == nk - 1, j, 0))),
        scratch_shapes=[pltpu.VMEM((n // tn, tm, tn), F32)],
        compiler_params=_params(("parallel", "arbitrary", "arbitrary")),
        name="ffn_down",
    )(h, wd, r)


def _lru_kernel(xa_ref, ga_ref, h0_ref, buf_ref, cw_ref, cb_ref, wa_ref, ba_ref, wx_ref, bx_ref, lam_ref,
                y_ref, ht_ref, nbuf_ref):
    t_len = xa_ref.shape[0]
    x = xa_ref[...]
    buf = buf_ref[...]
    cw = cw_ref[...]
    row = lax.broadcasted_iota(jnp.int32, x.shape, 0)
    nb = A_CONV - 1
    xc = x * cw[nb:nb + 1]
    for s in range(1, A_CONV):
        xs = pltpu.roll(x, s, 0)
        for r in range(s):
            xs = jnp.where(row == r, buf[nb + r - s:nb + r - s + 1], xs)
        xc = xc + xs * cw[nb - s:nb - s + 1]
    xc = xc + cb_ref[...]
    xcb = xc.astype(BF16)
    r_g = jax.nn.sigmoid(_dot(xcb, wa_ref[...].astype(BF16)) + ba_ref[...])
    i_g = jax.nn.sigmoid(_dot(xcb, wx_ref[...].astype(BF16)) + bx_ref[...])
    nl = -lam_ref[...]
    softplus = jnp.maximum(nl, 0.0) + jnp.log1p(jnp.exp(-jnp.abs(nl)))
    log_a = -LRU_C * r_g * softplus
    a = jnp.exp(log_a)
    u = jnp.sqrt(-jnp.tanh(log_a) * (a * a + 1.0)) * i_g * xc
    d = 1
    while d < t_len:
        keep = row >= d
        a_prev = pltpu.roll(a, d, 0)
        u_prev = pltpu.roll(u, d, 0)
        u = jnp.where(keep, a * u_prev + u, u)
        a = jnp.where(keep, a * a_prev, a)
        d *= 2
    h = a * h0_ref[...] + u
    y_ref[...] = (h * _gelu_tanh(ga_ref[...])).astype(y_ref.dtype)
    ht_ref[...] = h[t_len - 1:t_len]
    nbuf_ref[...] = x[t_len - nb:t_len]


def _into_zeros(kernel_fn, n_in, out_rows, width, dtype):
    def wrapped(*refs):
        return kernel_fn(*refs[:n_in], *refs[n_in + 1:])
    return wrapped, pl.BlockSpec(memory_space=pl.ANY), jnp.zeros((out_rows, width), dtype), {n_in: 0}


def _lru_mixer(z, row0, n_seq, t_len, h0, buf, conv_w, conv_b, wa, ba, wx, bx, lam, out_dtype, out_rows=None):
    assert row0 % t_len == 0 and t_len >= A_CONV - 1 and t_len % 8 == 0
    rb0 = row0 // t_len
    nb = A_CONV - 1
    out_rows = out_rows or n_seq * t_len
    vec = lambda v: v.reshape(1, D_A)
    cspec = pl.BlockSpec((1, A_BLK), lambda s, h: (0, h))
    wspec = pl.BlockSpec((None, A_BLK, A_BLK), lambda s, h: (h, 0, 0))
    body, base_spec, base, alias = _into_zeros(_lru_kernel, 11, out_rows, D_A, out_dtype)
    return pl.pallas_call(
        body,
        out_shape=(jax.ShapeDtypeStruct((out_rows, D_A), out_dtype),
                   jax.ShapeDtypeStruct((n_seq, 1, D_A), F32),
                   jax.ShapeDtypeStruct((n_seq, nb, D_A), F32)),
        grid=(n_seq, A_HEADS),
        in_specs=[pl.BlockSpec((t_len, A_BLK), lambda s, h: (rb0 + s, COL_XA + h)),
                  pl.BlockSpec((t_len, A_BLK), lambda s, h: (rb0 + s, COL_GA + h)),
                  pl.BlockSpec((None, 1, A_BLK), lambda s, h: (s, 0, h)),
                  pl.BlockSpec((None, nb, A_BLK), lambda s, h: (s, 0, h)),
                  pl.BlockSpec((A_CONV, A_BLK), lambda s, h: (0, h)),
                  cspec, wspec, cspec, wspec, cspec, cspec, base_spec],
        out_specs=(pl.BlockSpec((t_len, A_BLK), lambda s, h: (s, h)),
                   pl.BlockSpec((None, 1, A_BLK), lambda s, h: (s, 0, h)),
                   pl.BlockSpec((None, nb, A_BLK), lambda s, h: (s, 0, h))),
        input_output_aliases=alias,
        compiler_params=_params(("parallel", "parallel")),
        name="rglru_mixer",
    )(z, z, h0.reshape(n_seq, 1, D_A), buf, conv_w, vec(conv_b), wa, vec(ba), wx, vec(bx), vec(lam), base)


def _compress_kernel(x_ref, pe_ref, w1_ref, b1_ref, w2_ref, o_ref):
    n_sub = o_ref.shape[0]
    pe = pe_ref[...]
    y0 = jnp.zeros((n_sub, HEAD_DIM), F32)
    y1 = jnp.zeros((n_sub, HEAD_DIM), F32)
    for j in range(CMP_STRIDE):
        xj = x_ref[pl.ds(j, n_sub, stride=CMP_STRIDE), :]
        y0 = y0 + _dot((xj + pe[j:j + 1]).astype(BF16), w1_ref[j])
        y1 = y1 + _dot((xj + pe[CMP_STRIDE + j:CMP_STRIDE + j + 1]).astype(BF16), w1_ref[CMP_STRIDE + j])
    hid = y0 + pltpu.roll(y1, n_sub - 1, 0)
    hid = _gelu_tanh(hid + b1_ref[...])
    o_ref[...] = _dot(hid.astype(BF16), w2_ref[...]).astype(o_ref.dtype)


def _compress(x2d, x_index, n_seq, t_len, pe, w1, b1, w2):
    assert SUB_PER_CMP == 2 and t_len % (8 * CMP_STRIDE) == 0
    n_sub = t_len // CMP_STRIDE
    nt = 2 * KV_HEADS
    return pl.pallas_call(
        _compress_kernel,
        out_shape=jax.ShapeDtypeStruct((n_seq, nt, n_sub, HEAD_DIM), BF16),
        grid=(n_seq, nt),
        in_specs=[pl.BlockSpec((t_len, HEAD_DIM), x_index),
                  pl.BlockSpec((None, CMP_LEN, HEAD_DIM), lambda s, c: (c // KV_HEADS, 0, 0)),
                  pl.BlockSpec((None, CMP_LEN, HEAD_DIM, HEAD_DIM), lambda s, c: (c // KV_HEADS, 0, 0, 0)),
                  pl.BlockSpec((None, 1, HEAD_DIM), lambda s, c: (c // KV_HEADS, 0, 0)),
                  pl.BlockSpec((None, HEAD_DIM, HEAD_DIM), lambda s, c: (c // KV_HEADS, 0, 0))],
        out_specs=pl.BlockSpec((None, None, n_sub, HEAD_DIM), lambda s, c: (s, c, 0, 0)),
        compiler_params=_params(("parallel", "arbitrary")),
        name="nsa_compress",
    )(x2d, pe, w1.astype(BF16), b1.reshape(2, 1, HEAD_DIM), w2.astype(BF16))


def _head_rows(ref, c, n_tok):
    return ref[pl.ds(c, n_tok, stride=2 * KV_HEADS), :]


def _page_gather_kernel(pt_ref, *refs):
    page_refs, o_ref = refs[:-1], refs[-1]
    for k, c_ref in enumerate(page_refs):
        for c in range(2 * KV_HEADS):
            o_ref[c, k * PAGE_SIZE:(k + 1) * PAGE_SIZE, :] = _head_rows(c_ref, c, PAGE_SIZE)


def _pages_per_step(n_pages, want=4):
    pps = want
    while n_pages % pps:
        pps //= 2
    return pps


def _page_gather(cache_rows, layer, page_table):
    nb, n_pages = page_table.shape
    nt = 2 * KV_HEADS
    pps = _pages_per_step(n_pages, 16)

    def page_spec(k):
        return pl.BlockSpec((None, None, PAGE_SIZE * nt, HEAD_DIM), lambda b, p, pt: (layer, pt[b, p * pps + k], 0, 0))

    return pl.pallas_call(
        _page_gather_kernel,
        out_shape=jax.ShapeDtypeStruct((nb, nt, n_pages * PAGE_SIZE, HEAD_DIM), cache_rows.dtype),
        grid_spec=pltpu.PrefetchScalarGridSpec(
            num_scalar_prefetch=1, grid=(nb, n_pages // pps),
            in_specs=[page_spec(k) for k in range(pps)],
            out_specs=pl.BlockSpec((None, nt, pps * PAGE_SIZE, HEAD_DIM), lambda b, p, pt: (b, 0, p, 0))),
        compiler_params=_params(("parallel", "arbitrary")),
        name="page_gather",
    )(page_table, *([cache_rows] * pps))


def _stack_heads(q, g0):
    return jnp.concatenate([q[:, (g0 + p) * HEAD_DIM:(g0 + p + 1) * HEAD_DIM] for p in range(HPG)], axis=0).astype(BF16)


def _masked_softmax(s, ok):
    sb = jnp.where(ok, s, NEG_INF)
    m = jnp.max(sb, axis=-1, keepdims=True)
    e = jnp.exp(sb - m)
    return e / jnp.sum(e, axis=-1, keepdims=True)


def _cmp_branch(q4, kc, vc, t_rows, slope, n_cmp):
    ncp = kc.shape[0]
    s = _dot_nt(q4, kc) * QK_SCALE
    col = lax.broadcasted_iota(jnp.int32, s.shape, 1)
    dist_i = t_rows - (col * CMP_STRIDE + (CMP_LEN - 1))
    ok = (dist_i >= 0) & (col < n_cmp)
    p = _masked_softmax(s - _lanes(slope, ncp) * dist_i.astype(F32), ok)
    p = jnp.where(ok, p, 0.0)
    return _dot(p.astype(BF16), vc), p


def _select_blocks(imp, pool, t_q, n_sel):
    imp_sel = _dot_f32_exact(imp, pool)
    jcol = lax.broadcasted_iota(jnp.int32, imp_sel.shape, 1)
    cur = lax.shift_right_logical(t_q, int(np.log2(SEL_BLOCK)))
    forced = (jcol == 0) | (jcol == cur) | (jcol == cur - 1)
    valid = jcol * SEL_BLOCK <= t_q
    score = jnp.where(forced, FORCE_SCORE, jnp.where(valid, imp_sel, -1.0))
    score = jnp.where(jcol < n_sel, score, -2.0)
    rank = jnp.zeros(score.shape, F32)
    for i in range(n_sel):
        ci = score[:, i:i + 1]
        before = (ci > score) | ((ci == score) & (jcol > i))
        rank = rank + jnp.where(before, 1.0, 0.0)
    top = min(SEL_TOP, n_sel)
    return jnp.where(rank < float(top), 1.0, 0.0)


def _pool_matrix(n_cmp, ncp, n_sel, nsp):
    assert n_cmp + SUB_PER_CMP - 1 <= n_sel * SUB_PER_SEL
    m = np.zeros((ncp, nsp), np.float32)
    for c in range(n_cmp):
        for n in range(SUB_PER_CMP):
            m[c, (c + n) // SUB_PER_SEL] += 1.0 / SUB_PER_CMP
    return jnp.asarray(m, BF16)


def _slope_rows(t_rep):
    h = np.arange(1, B_HEADS + 1, dtype=np.float64)
    s = np.exp2(-8.0 * h / B_HEADS).astype(np.float32).reshape(KV_HEADS, HPG, 1, 1)
    return jnp.asarray(np.broadcast_to(s, (KV_HEADS, HPG, t_rep, LANE)).reshape(KV_HEADS, HPG * t_rep, LANE))


def _expand_matrix(shape, blk0):
    j = lax.broadcasted_iota(jnp.int32, shape, 0)
    k = lax.broadcasted_iota(jnp.int32, shape, 1)
    return jnp.where(j == blk0 + lax.shift_right_logical(k, int(np.log2(SEL_BLOCK))), 1.0, 0.0).astype(BF16)


def _gate_combine(gs, p, tq, o_c, o_s, o_w):
    rows = slice(p * tq, (p + 1) * tq)
    return (gs[:, p:p + 1] * o_c[rows] + gs[:, HPG + p:HPG + p + 1] * o_s[rows]
            + gs[:, 2 * HPG + p:2 * HPG + p + 1] * o_w[rows])


def _attn_prompt_kernel(q_ref, kc_ref, vc_ref, ks_ref, vs_ref, kw_ref, vw_ref, gt_ref, slope_ref, pool_ref,
                        o_ref, ksb, vsb, kwb, vwb, *, n_cmp, n_sel):
    qi = pl.program_id(2)
    tq = q_ref.shape[0]
    t_total = ks_ref.shape[0]
    rr = HPG * tq

    @pl.when(qi == 0)
    def _():
        ksb[...] = ks_ref[...].astype(BF16)
        vsb[...] = vs_ref[...].astype(BF16)
        kwb[...] = kw_ref[...].astype(BF16)
        vwb[...] = vw_ref[...].astype(BF16)

    t0 = qi * tq
    q4 = _stack_heads(q_ref[...], 0)
    slope = slope_ref[...]

    def t_rows(width):
        r = lax.broadcasted_iota(jnp.int32, (rr, width), 0)
        return t0 + (r & (tq - 1))

    ncp = kc_ref.shape[0]
    o_c, p_c = _cmp_branch(q4, kc_ref[...], vc_ref[...], t_rows(ncp), slope, n_cmp)
    imp = p_c[0:tq]
    for p in range(1, HPG):
        imp = imp + p_c[p * tq:(p + 1) * tq]
    nsp = pool_ref.shape[1]
    t_q = t0 + lax.broadcasted_iota(jnp.int32, (tq, nsp), 0)
    sel = _select_blocks(imp, pool_ref[...], t_q, n_sel).astype(BF16)
    sel4 = jnp.concatenate([sel] * HPG, axis=0)

    ck = ATT_CK
    tr_ck = t_rows(ck)
    slope_ck = _lanes(slope, ck)
    col_ck = lax.broadcasted_iota(jnp.int32, (rr, ck), 1)

    def sel_chunk(c, carry):
        m, l, acc = carry
        k0 = pl.multiple_of(c * ck, ck)
        s = _dot_nt(q4, ksb[pl.ds(k0, ck), :]) * QK_SCALE
        dist_i = tr_ck - (k0 + col_ck)
        selx = _dot(sel4, _expand_matrix((nsp, ck), c * (ck // SEL_BLOCK)))
        ok = (dist_i >= 0) & (selx > 0.5)
        sb = jnp.where(ok, s - slope_ck * dist_i.astype(F32), NEG_INF)
        m_new = jnp.maximum(m, jnp.max(sb, axis=-1, keepdims=True))
        alpha = jnp.exp(m - m_new)
        e = jnp.exp(sb - m_new)
        l = alpha * l + jnp.sum(e, axis=-1, keepdims=True)
        acc = alpha * acc + _dot(e.astype(BF16), vsb[pl.ds(k0, ck), :])
        return m_new, l, acc

    n_chunks = (t0 + tq + ck - 1) // ck
    m0 = jnp.full((rr, 1), NEG_INF, F32)
    _, l_s, acc_s = lax.fori_loop(0, n_chunks, sel_chunk, (m0, jnp.zeros((rr, 1), F32), jnp.zeros((rr, HEAD_DIM), F32)))
    o_s = acc_s / l_s

    wk = WINDOW + tq
    w0 = pl.multiple_of(jnp.maximum(t0 - WINDOW, 0), tq)
    s = _dot_nt(q4, kwb[pl.ds(w0, wk), :]) * QK_SCALE
    dist_i = t_rows(wk) - (w0 + lax.broadcasted_iota(jnp.int32, (rr, wk), 1))
    ok = (dist_i >= 0) & (dist_i < WINDOW)
    p_w = _masked_softmax(s - _lanes(slope, wk) * dist_i.astype(F32), ok)
    o_w = _dot(p_w.astype(BF16), vwb[pl.ds(w0, wk), :])

    gs = jax.nn.sigmoid(gt_ref[...])
    for p in range(HPG):
        o_ref[:, p * HEAD_DIM:(p + 1) * HEAD_DIM] = _gate_combine(gs, p, tq, o_c, o_s, o_w).astype(o_ref.dtype)


def _attn_prompt(z, gt, kvc_cmp, n_seq, t_len, out_rows=None):
    tq = ATT_TQ
    assert t_len % ATT_CK == 0 and t_len >= WINDOW + tq and tq & (tq - 1) == 0
    n_cmp = (t_len - CMP_LEN) // CMP_STRIDE + 1
    n_sel = -(-t_len // SEL_BLOCK)
    ncp = kvc_cmp.shape[2]
    nsp = _round_up(n_sel, LANE)
    nq = t_len // tq
    rr = HPG * tq
    kvspec = lambda col0: pl.BlockSpec((t_len, HEAD_DIM), lambda b, g, i: (b, col0 + g))
    out_rows = out_rows or n_seq * t_len
    body, base_spec, base, alias = _into_zeros(functools.partial(_attn_prompt_kernel, n_cmp=n_cmp, n_sel=n_sel),
                                               10, out_rows, D_B, BF16)
    return pl.pallas_call(
        body,
        out_shape=jax.ShapeDtypeStruct((out_rows, D_B), BF16),
        grid=(n_seq, KV_HEADS, nq),
        in_specs=[pl.BlockSpec((tq, HPG * HEAD_DIM), lambda b, g, i: (b * nq + i, COL_Q // HPG + g)),
                  pl.BlockSpec((None, None, ncp, HEAD_DIM), lambda b, g, i: (b, g, 0, 0)),
                  pl.BlockSpec((None, None, ncp, HEAD_DIM), lambda b, g, i: (b, KV_HEADS + g, 0, 0)),
                  kvspec(COL_KVS), kvspec(COL_KVS + KV_HEADS), kvspec(COL_KVW), kvspec(COL_KVW + KV_HEADS),
                  pl.BlockSpec((tq, LANE), lambda b, g, i: (b * nq + i, g)),
                  pl.BlockSpec((None, rr, LANE), lambda b, g, i: (g, 0, 0)),
                  pl.BlockSpec((ncp, nsp), lambda b, g, i: (0, 0)), base_spec],
        out_specs=pl.BlockSpec((tq, HPG * HEAD_DIM), lambda b, g, i: (b * nq + i, g)),
        scratch_shapes=[pltpu.VMEM((t_len, HEAD_DIM), BF16)] * 4,
        input_output_aliases=alias,
        compiler_params=_params(("parallel", "parallel", "arbitrary")),
        name="nsa_prompt_attention",
    )(z, kvc_cmp, kvc_cmp, z, z, z, z, gt, _slope_rows(tq), _pool_matrix(n_cmp, ncp, n_sel, nsp), base)


def _attn_dec_select_kernel(q_ref, kc_ref, vc_ref, slope_ref, pool_ref, oc_ref, sel_ref, *, n_cmp, n_sel, past_len):
    ts = q_ref.shape[0]
    rr = HPG * ts
    ncp = kc_ref.shape[0]
    q4 = _stack_heads(q_ref[...], 0)
    t_rows = past_len + (lax.broadcasted_iota(jnp.int32, (rr, ncp), 0) & (ts - 1))
    o_c, p_c = _cmp_branch(q4, kc_ref[...], vc_ref[...], t_rows, slope_ref[...], n_cmp)
    imp = p_c[0:ts]
    for p in range(1, HPG):
        imp = imp + p_c[p * ts:(p + 1) * ts]
    nsp = pool_ref.shape[1]
    t_q = past_len + lax.broadcasted_iota(jnp.int32, (ts, nsp), 0)
    oc_ref[...] = o_c
    sel_ref[...] = _select_blocks(imp, pool_ref[...], t_q, n_sel)


def _attn_dec_select(z, row0, kvc_cmp, n_seq, ts, past_len, n_cmp, n_sel):
    assert row0 % ts == 0 and ts & (ts - 1) == 0 and ts % 8 == 0
    rb0 = row0 // ts
    ncp = kvc_cmp.shape[2]
    nsp = _round_up(n_sel, LANE)
    rr = HPG * ts
    return pl.pallas_call(
        functools.partial(_attn_dec_select_kernel, n_cmp=n_cmp, n_sel=n_sel, past_len=past_len),
        out_shape=(jax.ShapeDtypeStruct((n_seq, KV_HEADS, rr, HEAD_DIM), F32),
                   jax.ShapeDtypeStruct((n_seq, KV_HEADS, ts, nsp), F32)),
        grid=(n_seq, KV_HEADS),
        in_specs=[pl.BlockSpec((ts, HPG * HEAD_DIM), lambda b, g: (rb0 + b, COL_Q // HPG + g)),
                  pl.BlockSpec((None, None, ncp, HEAD_DIM), lambda b, g: (b, g, 0, 0)),
                  pl.BlockSpec((None, None, ncp, HEAD_DIM), lambda b, g: (b, KV_HEADS + g, 0, 0)),
                  pl.BlockSpec((None, rr, LANE), lambda b, g: (g, 0, 0)),
                  pl.BlockSpec((ncp, nsp), lambda b, g: (0, 0))],
        out_specs=(pl.BlockSpec((None, None, rr, HEAD_DIM), lambda b, g: (b, g, 0, 0)),
                   pl.BlockSpec((None, None, ts, nsp), lambda b, g: (b, g, 0, 0))),
        compiler_params=_params(("parallel", "parallel")),
        name="nsa_decode_select",
    )(z, kvc_cmp, kvc_cmp, _slope_rows(ts), _pool_matrix(n_cmp, ncp, n_sel, nsp))


def _attn_dec_kernel(pt_ref, q_ref, sel_ref, *rest, past_len, n_steps, pages_per_step):
    page_refs = rest[:pages_per_step]
    kvs_ref, kvw_ref, win_ref, oc_ref, gt_ref, slope_ref, o_ref, nwin_ref, m_sc, l_sc, acc_sc = rest[pages_per_step:]
    pg = pl.program_id(1)
    ts = q_ref.shape[0]
    rr = HPG * ts
    nsp = sel_ref.shape[-1]
    pad_rows = LANE - ts
    row = lax.broadcasted_iota(jnp.int32, (rr, LANE), 0)
    col = lax.broadcasted_iota(jnp.int32, (rr, LANE), 1)
    t_rows = past_len + (row & (ts - 1))

    @pl.when(pg == 0)
    def _():
        m_sc[...] = jnp.full(m_sc.shape, NEG_INF, F32)
        l_sc[...] = jnp.zeros(l_sc.shape, F32)
        acc_sc[...] = jnp.zeros(acc_sc.shape, F32)

    def sel_update(g, q4, kk, vv, dist_i, expand, extra_ok):
        width = kk.shape[0]
        s = _dot_nt(q4, kk) * QK_SCALE
        sel4 = jnp.concatenate([sel_ref[g].astype(BF16)] * HPG, axis=0)
        ok = (dist_i >= 0) & (_dot(sel4, expand) > 0.5) & extra_ok
        sb = jnp.where(ok, s - _lanes(slope_ref[g], width) * dist_i.astype(F32), NEG_INF)
        m_old = m_sc[g]
        m_new = jnp.maximum(m_old, jnp.max(sb, axis=-1, keepdims=True))
        alpha = jnp.exp(m_old - m_new)
        e = jnp.exp(sb - m_new)
        l_sc[g] = alpha * l_sc[g] + jnp.sum(e, axis=-1, keepdims=True)
        acc_sc[g] = alpha * acc_sc[g] + _dot(e.astype(BF16), vv)
        m_sc[g] = m_new

    @pl.when(pg < n_steps)
    def _():
        width = pages_per_step * PAGE_SIZE
        k0 = pg * width
        expand = _expand_matrix((nsp, width), pg * (width // SEL_BLOCK))
        wrow_ = lax.broadcasted_iota(jnp.int32, (rr, width), 0)
        wcol_ = lax.broadcasted_iota(jnp.int32, (rr, width), 1)
        dist_i = (past_len + (wrow_ & (ts - 1))) - (k0 + wcol_)
        for g in range(KV_HEADS):
            q4 = _stack_heads(q_ref[...], g * HPG)
            kk = jnp.concatenate([_head_rows(r, g, PAGE_SIZE) for r in page_refs], axis=0).astype(BF16)
            vv = jnp.concatenate([_head_rows(r, KV_HEADS + g, PAGE_SIZE) for r in page_refs], axis=0).astype(BF16)
            sel_update(g, q4, kk, vv, dist_i, expand, True)

    @pl.when(pg == n_steps)
    def _():
        zpad = jnp.zeros((pad_rows, HEAD_DIM), F32)
        gs = jax.nn.sigmoid(gt_ref[...])
        jj = lax.broadcasted_iota(jnp.int32, (nsp, LANE), 0)
        kk_i = lax.broadcasted_iota(jnp.int32, (nsp, LANE), 1)
        new_blk = lax.shift_right_logical(past_len + kk_i, int(np.log2(SEL_BLOCK)))
        expand_new = jnp.where(jj == new_blk, 1.0, 0.0).astype(BF16)
        is_new = col < ts
        nt = 2 * KV_HEADS
        wlen = win_ref.shape[0] // nt
        wrow = lax.broadcasted_iota(jnp.int32, (rr, wlen), 0)
        wcol = lax.broadcasted_iota(jnp.int32, (rr, wlen), 1)
        dist_w = (past_len + (wrow & (ts - 1))) - (past_len - wlen + wcol)
        ok_w = (dist_w >= 0) & (dist_w < WINDOW)
        dist_n = t_rows - (past_len + col)
        ok_n = (dist_n >= 0) & (dist_n < WINDOW) & is_new
        for g in range(KV_HEADS):
            q4 = _stack_heads(q_ref[...], g * HPG)
            ksl = slice(g * HEAD_DIM, (g + 1) * HEAD_DIM)
            vsl = slice(D_KV + g * HEAD_DIM, D_KV + (g + 1) * HEAD_DIM)
            kn = jnp.concatenate([kvs_ref[:, ksl], zpad], axis=0).astype(BF16)
            vn = jnp.concatenate([kvs_ref[:, vsl], zpad], axis=0).astype(BF16)
            sel_update(g, q4, kn, vn, dist_n, expand_new, is_new)
            o_s = acc_sc[g] / l_sc[g]
            slope = slope_ref[g]
            s1 = _dot_nt(q4, _head_rows(win_ref, g, wlen).astype(BF16)) * QK_SCALE
            s1 = jnp.where(ok_w, s1 - _lanes(slope, wlen) * dist_w.astype(F32), NEG_INF)
            kwn = jnp.concatenate([kvw_ref[:, ksl], zpad], axis=0).astype(BF16)
            vwn = jnp.concatenate([kvw_ref[:, vsl], zpad], axis=0).astype(BF16)
            s2 = _dot_nt(q4, kwn) * QK_SCALE
            s2 = jnp.where(ok_n, s2 - slope * dist_n.astype(F32), NEG_INF)
            m = jnp.maximum(jnp.max(s1, axis=-1, keepdims=True), jnp.max(s2, axis=-1, keepdims=True))
            e1 = jnp.exp(s1 - m)
            e2 = jnp.exp(s2 - m)
            inv = 1.0 / (jnp.sum(e1, axis=-1, keepdims=True) + jnp.sum(e2, axis=-1, keepdims=True))
            o_w = (_dot((e1 * inv).astype(BF16), _head_rows(win_ref, KV_HEADS + g, wlen).astype(BF16))
                   + _dot((e2 * inv).astype(BF16), vwn))
            o_c = oc_ref[g]
            for p in range(HPG):
                c0 = (g * HPG + p) * HEAD_DIM
                o_ref[:, c0:c0 + HEAD_DIM] = _gate_combine(gs[:, g * LANE:(g + 1) * LANE], p, ts, o_c, o_s, o_w)
        keep = (wlen - ts) * nt
        nwin_ref[0:keep, :] = win_ref[ts * nt:wlen * nt, :]
        for c in range(nt):
            nwin_ref[pl.ds(keep + c, ts, stride=nt), :] = kvw_ref[:, c * HEAD_DIM:(c + 1) * HEAD_DIM]


def _attn_decode(z, gt, row0, n_seq, ts, page_table, cache_rows, layer, win_rows, o_c, sel, past_len):
    n_pages = page_table.shape[1]
    pps = _pages_per_step(n_pages)
    n_steps = n_pages // pps
    rb0 = row0 // ts
    rr = HPG * ts
    nsp = sel.shape[-1]
    nt = 2 * KV_HEADS
    wrows = win_rows.shape[2]
    w = 2 * D_KV
    assert wrows == WINDOW * nt and past_len % SEL_BLOCK == 0 and PAGE_SIZE == LANE
    last = n_steps - 1

    def page_spec(k):
        return pl.BlockSpec((None, None, PAGE_SIZE * nt, HEAD_DIM),
                            lambda b, p, pt: (layer, pt[b, jnp.minimum(p, last) * pps + k], 0, 0))

    return pl.pallas_call(
        functools.partial(_attn_dec_kernel, past_len=past_len, n_steps=n_steps, pages_per_step=pps),
        out_shape=(jax.ShapeDtypeStruct((n_seq * ts, D_B), F32),
                   jax.ShapeDtypeStruct((n_seq, wrows, HEAD_DIM), F32)),
        grid_spec=pltpu.PrefetchScalarGridSpec(
            num_scalar_prefetch=1, grid=(n_seq, n_steps + 1),
            in_specs=[pl.BlockSpec((ts, D_B), lambda b, p, pt: (rb0 + b, COL_Q * LANE // D_B)),
                      pl.BlockSpec((None, KV_HEADS, ts, nsp), lambda b, p, pt: (b, 0, 0, 0))]
                     + [page_spec(k) for k in range(pps)]
                     + [pl.BlockSpec((ts, w), lambda b, p, pt: (rb0 + b, COL_KVS * LANE // w)),
                        pl.BlockSpec((ts, w), lambda b, p, pt: (rb0 + b, COL_KVW * LANE // w)),
                        pl.BlockSpec((None, None, wrows, HEAD_DIM), lambda b, p, pt: (layer, b, 0, 0)),
                        pl.BlockSpec((None, KV_HEADS, rr, HEAD_DIM), lambda b, p, pt: (b, 0, 0, 0)),
                        pl.BlockSpec((ts, KV_HEADS * LANE), lambda b, p, pt: (rb0 + b, 0)),
                        pl.BlockSpec((KV_HEADS, rr, LANE), lambda b, p, pt: (0, 0, 0))],
            out_specs=(pl.BlockSpec((ts, D_B), lambda b, p, pt: (b, 0)),
                       pl.BlockSpec((None, wrows, HEAD_DIM), lambda b, p, pt: (b, 0, 0))),
            scratch_shapes=[pltpu.VMEM((KV_HEADS, rr, 1), F32), pltpu.VMEM((KV_HEADS, rr, 1), F32),
                            pltpu.VMEM((KV_HEADS, rr, HEAD_DIM), F32)]),
        compiler_params=_params(("parallel", "arbitrary")),
        name="nsa_decode_attention",
    )(page_table, z, sel, *([cache_rows] * pps), z, z, win_rows, o_c, gt, _slope_rows(ts))


def _sconv_kernel(bg_ref, cg_ref, v_ref, buf_ref, cw_ref, y_ref, nbuf_ref):
    t_len = cg_ref.shape[0]
    nb = C_CONV - 1
    cv = cg_ref[...] * v_ref[...]
    buf = buf_ref[...]
    cw = cw_ref[...]
    row = lax.broadcasted_iota(jnp.int32, cv.shape, 0)
    y = cv * cw[nb:nb + 1]
    for s in range(1, C_CONV):
        xs = pltpu.roll(cv, s, 0)
        for r in range(s):
            xs = jnp.where(row == r, buf[nb + r - s:nb + r - s + 1], xs)
        y = y + xs * cw[nb - s:nb - s + 1]
    y_ref[...] = (bg_ref[...] * y).astype(y_ref.dtype)
    nbuf_ref[...] = cv[t_len - nb:t_len]


def _sconv_mixer(zc, row0, n_seq, t_len, buf, conv_w, out_dtype, out_rows=None):
    assert row0 % t_len == 0 and t_len >= C_CONV - 1 and t_len % 8 == 0
    rb0 = row0 // t_len
    nb = C_CONV - 1
    tc = 256
    nc = D_C // tc
    zspec = lambda third: pl.BlockSpec((t_len, tc), lambda s, c: (rb0 + s, third * nc + c))
    out_rows = out_rows or n_seq * t_len
    body, base_spec, base, alias = _into_zeros(_sconv_kernel, 5, out_rows, D_C, out_dtype)
    return pl.pallas_call(
        body,
        out_shape=(jax.ShapeDtypeStruct((out_rows, D_C), out_dtype),
                   jax.ShapeDtypeStruct((n_seq, nb, D_C), F32)),
        grid=(n_seq, nc),
        in_specs=[zspec(0), zspec(1), zspec(2),
                  pl.BlockSpec((None, nb, tc), lambda s, c: (s, 0, c)),
                  pl.BlockSpec((C_CONV, tc), lambda s, c: (0, c)), base_spec],
        out_specs=(pl.BlockSpec((t_len, tc), lambda s, c: (s, c)),
                   pl.BlockSpec((None, nb, tc), lambda s, c: (s, 0, c))),
        input_output_aliases=alias,
        compiler_params=_params(("parallel", "parallel")),
        name="sconv_mixer",
    )(zc, zc, zc, buf, conv_w, base)


def _gate_weight(w_in):
    nl, k = w_in.shape[:2]
    wg = w_in[:, :, D_IN_MAIN:].reshape(nl, k, 3, KV_HEADS, HPG).transpose(0, 1, 3, 2, 4)
    wg = jnp.pad(wg.reshape(nl, k, KV_HEADS, 3 * HPG), ((0, 0), (0, 0), (0, 0), (0, LANE - 3 * HPG)))
    return wg.reshape(nl, k, KV_HEADS * LANE).astype(BF16)


def _ab_layer(h, hn, i, mp, bp, t_len, bs, ts, past_len, w, cache_kv_cmp, cache_kv_sel, state_win_kv, state_lru_h,
              state_lru_conv, page_table):
    z, kv_rows = _in_proj(hn, w["ab_w_in"], i)
    gt = _matmul(hn, w["ab_w_gate_bf16"], i, name="ab_gate_proj")
    nt = 2 * KV_HEADS
    rows_view = lambda a: a.reshape(a.shape[0], a.shape[1], -1, HEAD_DIM)
    lru_w = (w["ab_conv_w"][i], w["ab_conv_b"][i], w["ab_gate_a_w"][i], w["ab_gate_a_b"][i], w["ab_gate_x_w"][i],
             w["ab_gate_x_b"][i], w["ab_lru_lambda"][i])
    cmp_w = (w["ab_cmp_pe"][i], w["ab_cmp_w1"][i], w["ab_cmp_b1"][i], w["ab_cmp_w2"][i])
    n_pages = page_table.shape[1]
    tk = past_len + ts
    n_cmp_s = (tk - CMP_LEN) // CMP_STRIDE + 1
    n_sel_s = -(-tk // SEL_BLOCK)
    assert n_cmp_s == (past_len - CMP_LEN) // CMP_STRIDE + 1

    ya_p, hT_p, nbuf_p = _lru_mixer(z, 0, bp, t_len, jnp.zeros((bp, D_A), F32), jnp.zeros((bp, A_CONV - 1, D_A), F32),
                                    *lru_w, out_dtype=BF16, out_rows=mp + bs * ts)
    kvc_p = _compress(z, lambda s, c: (s, COL_KVC + c), bp, t_len, *cmp_w)
    ob_p = _attn_prompt(z, gt, kvc_p, bp, t_len, out_rows=mp + bs * ts)

    ya_s, hT_s, nbuf_s = _lru_mixer(z, mp, bs, ts, state_lru_h[i], state_lru_conv[i], *lru_w, out_dtype=F32)
    past_cmp = _page_gather(rows_view(cache_kv_cmp), i, page_table)
    kvc_s = _compress(past_cmp.reshape(bs * nt * past_len, HEAD_DIM), lambda s, c: (s * nt + c, 0), bs, past_len, *cmp_w)
    oc_s, sel_s = _attn_dec_select(z, mp, kvc_s, bs, ts, past_len, n_cmp_s, n_sel_s)
    ob_s, nwin_s = _attn_decode(z, gt, mp, bs, ts, page_table, rows_view(cache_kv_sel), i, rows_view(state_win_kv),
                                oc_s, sel_s, past_len)

    ya = lax.dynamic_update_slice(ya_p, ya_s.astype(BF16), (mp, 0))
    ob = lax.dynamic_update_slice(ob_p, ob_s.astype(BF16), (mp, 0))
    h = _matmul2_residual(ya, ob, w["ab_w_out_bf16"], i, h, name="ab_out_proj")

    kv5 = lambda a, nb_, tt: a.reshape(nb_, tt, 2, KV_HEADS, HEAD_DIM)
    wl = min(WINDOW, t_len)
    kvw_p = kv5(kv_rows[2, :mp * nt], bp, t_len)
    outs_p = (kv5(kv_rows[0, :mp * nt], bp, t_len), kv5(kv_rows[1, :mp * nt], bp, t_len),
              kvw_p[:, t_len - wl:], hT_p.reshape(bp, D_A), nbuf_p)
    outs_s = (kv5(kv_rows[0, mp * nt:], bs, ts), kv5(kv_rows[1, mp * nt:], bs, ts),
              kv5(nwin_s, bs, WINDOW), hT_s.reshape(bs, D_A), nbuf_s)
    return h, outs_p, outs_s


def _c_layer(h, hn, i, mp, bp, t_len, bs, ts, w, state_sconv):
    zc = _matmul(hn, w["c_w_in"], i, name="c_in_proj", row_cap=WIDE_ROW_CAP, col_cap=F32_WEIGHT_COL_CAP)
    y_p, nbuf_p = _sconv_mixer(zc, 0, bp, t_len, jnp.zeros((bp, C_CONV - 1, D_C), F32), w["c_conv_w"][i], BF16,
                               out_rows=mp + bs * ts)
    y_s, nbuf_s = _sconv_mixer(zc, mp, bs, ts, state_sconv[i], w["c_conv_w"][i], F32)
    y = lax.dynamic_update_slice(y_p, y_s.astype(BF16), (mp, 0))
    h = _matmul_residual(y, w["c_w_out_bf16"], i, h, name="c_out_proj")
    return h, nbuf_p, nbuf_s


def kernel(x_prompt, x_sample, cache_kv_cmp, cache_kv_sel, state_win_kv, state_lru_h, state_lru_conv, state_sconv,
           page_table, norm_mix, norm_ffn, norm_final, ab_w_in, ab_conv_w, ab_conv_b, ab_gate_a_w, ab_gate_a_b,
           ab_gate_x_w, ab_gate_x_b, ab_lru_lambda, ab_cmp_pe, ab_cmp_w1, ab_cmp_b1, ab_cmp_w2, ab_w_out,
           c_w_in, c_conv_w, c_w_out, ffn_w_gate, ffn_w_up, ffn_w_down):
    w = dict(ab_w_in=ab_w_in, ab_w_gate_bf16=_gate_weight(ab_w_in), ab_conv_w=ab_conv_w,
             ab_conv_b=ab_conv_b, ab_gate_a_w=ab_gate_a_w, ab_gate_a_b=ab_gate_a_b, ab_gate_x_w=ab_gate_x_w,
             ab_gate_x_b=ab_gate_x_b, ab_lru_lambda=ab_lru_lambda, ab_cmp_pe=ab_cmp_pe, ab_cmp_w1=ab_cmp_w1,
             ab_cmp_b1=ab_cmp_b1, ab_cmp_w2=ab_cmp_w2, ab_w_out_bf16=ab_w_out.astype(BF16),
             c_w_in=c_w_in, c_conv_w=c_conv_w, c_w_out_bf16=c_w_out.astype(BF16))
    ffn_w_down_bf16 = ffn_w_down.astype(BF16)
    bp, t_len = x_prompt.shape[:2]
    bs, ts = x_sample.shape[:2]
    mp, ms = bp * t_len, bs * ts
    past_len = page_table.shape[1] * PAGE_SIZE
    h = jnp.concatenate([x_prompt.reshape(mp, D_MODEL), x_sample.reshape(ms, D_MODEL)], axis=0)
    p_ab, s_ab, p_c, s_c = [], [], [], []
    for layer in range(DEPTH):
        hn = _rmsnorm(h, norm_mix[layer], BF16)
        i = layer // 2
        if layer % 2 == 0:
            h, op, os_ = _ab_layer(h, hn, i, mp, bp, t_len, bs, ts, past_len, w, cache_kv_cmp, cache_kv_sel,
                                   state_win_kv, state_lru_h, state_lru_conv, page_table)
            p_ab.append(op)
            s_ab.append(os_)
        else:
            h, nbp, nbs = _c_layer(h, hn, i, mp, bp, t_len, bs, ts, w, state_sconv)
            p_c.append(nbp)
            s_c.append(nbs)
        hf = _rmsnorm(h, norm_ffn[layer], BF16)
        hid = _ffn_up(hf, ffn_w_gate, ffn_w_up, layer)
        h = _ffn_down(hid, ffn_w_down_bf16, layer, h)
    y = _rmsnorm(h, norm_final, F32)
    y_prompt = y[:mp].reshape(bp, t_len, D_MODEL)
    y_sample = y[mp:].reshape(bs, ts, D_MODEL)
    stack = lambda rows, j: jnp.stack([r[j] for r in rows])
    return (y_prompt, y_sample,
            stack(p_ab, 0), stack(p_ab, 1), stack(p_ab, 2), stack(p_ab, 3), stack(p_ab, 4), jnp.stack(p_c),
            stack(s_ab, 0), stack(s_ab, 1), stack(s_ab, 2), stack(s_ab, 3), stack(s_ab, 4), jnp.stack(s_c))
```

```python
import functools

import jax
import jax.numpy as jnp
import numpy as np
from jax import lax
from jax.experimental import pallas as pl
from jax.experimental.pallas import tpu as pltpu

D_MODEL = 4096
DEPTH = 4
PAGE_SIZE = 128
N_AB = (DEPTH + 1) // 2
N_C = DEPTH // 2
D_A = D_MODEL // 2
A_HEADS = 16
A_BLK = D_A // A_HEADS
A_CONV = 4
LRU_C = 8.0
B_HEADS = 16
HEAD_DIM = 128
KV_HEADS = 4
HPG = B_HEADS // KV_HEADS
D_B = B_HEADS * HEAD_DIM
D_KV = KV_HEADS * HEAD_DIM
CMP_LEN = 32
CMP_STRIDE = 16
SUB_PER_CMP = CMP_LEN // CMP_STRIDE
SEL_BLOCK = 64
SUB_PER_SEL = SEL_BLOCK // CMP_STRIDE
SEL_TOP = 16
WINDOW = 512
D_C = D_MODEL
C_CONV = 3
D_FF = -(-8 * D_MODEL // (3 * 256)) * 256
D_IN_MAIN = 2 * D_A + D_B + 6 * D_KV
N_GATES = 3 * B_HEADS
RMS_EPS = 1e-6
NEG_INF = -1e30
FORCE_SCORE = 1e4
QK_SCALE = HEAD_DIM ** -0.5

COL_XA = 0
COL_GA = D_A // 128
COL_Q = 2 * D_A // 128
COL_KVC = (2 * D_A + D_B) // 128
COL_KVS = COL_KVC + 2 * D_KV // 128
COL_KVW = COL_KVS + 2 * D_KV // 128

V7X_VMEM_LIMIT_BYTES = 56 * 1024 * 1024
LANE = 128
ATT_TQ = 256
ATT_CK = 512
F32 = jnp.float32
BF16 = jnp.bfloat16


def _row_tile(m, cap=1024):
    for t in range(cap - cap % 16, 15, -16):
        if m % t == 0:
            return t
    raise ValueError(m)


def _col_tile(n, cap=512):
    for t in range(cap, LANE - 1, -LANE):
        if n % t == 0:
            return t
    raise ValueError(n)


def _round_up(x, m):
    return -(-x // m) * m


def _params(sem):
    return pltpu.CompilerParams(dimension_semantics=sem, vmem_limit_bytes=V7X_VMEM_LIMIT_BYTES)


def _dot(a, b):
    return jnp.dot(a, b, preferred_element_type=F32)


def _dot_nt(a, b):
    return lax.dot_general(a, b, (((1,), (1,)), ((), ())), preferred_element_type=F32)


def _dot_f32_exact(x, w_bf16):
    hi = x.astype(BF16)
    r1 = x - hi.astype(F32)
    mid = r1.astype(BF16)
    lo = (r1 - mid.astype(F32)).astype(BF16)
    return _dot(hi, w_bf16) + (_dot(mid, w_bf16) + _dot(lo, w_bf16))


def _gelu_tanh(x):
    return 0.5 * x * (1.0 + jnp.tanh(0.7978845608028654 * (x + 0.044715 * (x * x * x))))


def _lanes(x128, w):
    return x128 if w == LANE else jnp.concatenate([x128] * (w // LANE), axis=1)


def _rmsnorm_kernel(x_ref, g_ref, o_ref):
    x = x_ref[...]
    y = x * lax.rsqrt(jnp.mean(x * x, axis=-1, keepdims=True) + RMS_EPS)
    o_ref[...] = (y * g_ref[...]).astype(o_ref.dtype)


def _rmsnorm(x, g, out_dtype):
    m, d = x.shape
    tm = _row_tile(m)
    return pl.pallas_call(
        _rmsnorm_kernel,
        out_shape=jax.ShapeDtypeStruct((m, d), out_dtype),
        grid=(m // tm,),
        in_specs=[pl.BlockSpec((tm, d), lambda i: (i, 0)), pl.BlockSpec((1, d), lambda i: (0, 0))],
        out_specs=pl.BlockSpec((tm, d), lambda i: (i, 0)),
        compiler_params=_params(("parallel",)),
        name="rmsnorm",
    )(x, g.reshape(1, d))


def _wspec(layer, k, tn):
    return pl.BlockSpec((None, k, tn), lambda i, j: (layer, 0, j))


def _mm_kernel(x_ref, w_ref, o_ref):
    o_ref[...] = _dot(x_ref[...], w_ref[...].astype(BF16)).astype(o_ref.dtype)


def _matmul(x, w, layer, out_dtype=F32, name="matmul", row_cap=1024, col_cap=512):
    m, k = x.shape
    n = w.shape[2]
    tm, tn = _row_tile(m, row_cap), _col_tile(n, col_cap)
    return pl.pallas_call(
        _mm_kernel,
        out_shape=jax.ShapeDtypeStruct((m, n), out_dtype),
        grid=(m // tm, n // tn),
        in_specs=[pl.BlockSpec((tm, k), lambda i, j: (i, 0)), _wspec(layer, k, tn)],
        out_specs=pl.BlockSpec((tm, tn), lambda i, j: (i, j)),
        compiler_params=_params(("parallel", "arbitrary")),
        name=name,
    )(x, w)


def _in_proj_kernel(x_ref, w_ref, z_ref, kv_ref, *, j_kv0, tiles_per_seg):
    j = pl.program_id(1)
    tm, tn = z_ref.shape
    heads_per_tile = tn // HEAD_DIM
    rows_per_token = 2 * KV_HEADS
    acc = _dot(x_ref[...], w_ref[...].astype(BF16))
    z_ref[...] = acc
    for part in range(tiles_per_seg):
        @pl.when((j >= j_kv0) & ((j - j_kv0) % tiles_per_seg == part))
        def _():
            for c in range(heads_per_tile):
                kv_ref[pl.ds(part * heads_per_tile + c, tm, stride=rows_per_token), :] = (
                    acc[:, c * HEAD_DIM:(c + 1) * HEAD_DIM])


def _in_proj(x, w, layer):
    m, k = x.shape
    n = D_IN_MAIN
    tm, tn = _row_tile(m), _col_tile(n)
    seg = 2 * D_KV
    assert seg % tn == 0 and (COL_KVC * LANE) % tn == 0 and n == COL_KVC * LANE + 3 * seg and w.shape[2] >= n
    tiles_per_seg = seg // tn
    j_kv0 = COL_KVC * LANE // tn
    rpt = 2 * KV_HEADS
    return pl.pallas_call(
        functools.partial(_in_proj_kernel, j_kv0=j_kv0, tiles_per_seg=tiles_per_seg),
        out_shape=(jax.ShapeDtypeStruct((m, n), F32), jax.ShapeDtypeStruct((3, m * rpt, HEAD_DIM), F32)),
        grid=(m // tm, n // tn),
        in_specs=[pl.BlockSpec((tm, k), lambda i, j: (i, 0)), _wspec(layer, k, tn)],
        out_specs=(pl.BlockSpec((tm, tn), lambda i, j: (i, j)),
                   pl.BlockSpec((None, tm * rpt, HEAD_DIM),
                                lambda i, j: (jnp.maximum(j - j_kv0, 0) // tiles_per_seg, i, 0))),
        compiler_params=_params(("parallel", "arbitrary")),
        name="ab_in_proj",
    )(x, w)


def _mm_res_kernel(x_ref, w_ref, r_ref, o_ref):
    o_ref[...] = r_ref[...] + _dot(x_ref[...], w_ref[...].astype(BF16))


def _matmul_residual(x, w, layer, r, name="matmul_res", col_cap=512):
    m, k = x.shape
    n = w.shape[2]
    tm, tn = _row_tile(m), _col_tile(n, col_cap)
    return pl.pallas_call(
        _mm_res_kernel,
        out_shape=jax.ShapeDtypeStruct((m, n), F32),
        grid=(m // tm, n // tn),
        in_specs=[pl.BlockSpec((tm, k), lambda i, j: (i, 0)), _wspec(layer, k, tn),
                  pl.BlockSpec((tm, tn), lambda i, j: (i, j))],
        out_specs=pl.BlockSpec((tm, tn), lambda i, j: (i, j)),
        compiler_params=_params(("parallel", "arbitrary")),
        name=name,
    )(x, w, r)


def _mm2_res_kernel(x1_ref, x2_ref, w1_ref, w2_ref, r_ref, o_ref):
    o_ref[...] = r_ref[...] + (_dot(x1_ref[...], w1_ref[...].astype(BF16))
                               + _dot(x2_ref[...], w2_ref[...].astype(BF16)))


def _matmul2_residual(x1, x2, w, layer, r, name="matmul2_res"):
    m, k1 = x1.shape
    k2 = x2.shape[1]
    n = w.shape[2]
    tm, tn = _row_tile(m), _col_tile(n)
    assert k1 % 8 == 0 and k1 == k2
    return pl.pallas_call(
        _mm2_res_kernel,
        out_shape=jax.ShapeDtypeStruct((m, n), F32),
        grid=(m // tm, n // tn),
        in_specs=[pl.BlockSpec((tm, k1), lambda i, j: (i, 0)), pl.BlockSpec((tm, k2), lambda i, j: (i, 0)),
                  pl.BlockSpec((None, k1, tn), lambda i, j: (layer, 0, j)),
                  pl.BlockSpec((None, k2, tn), lambda i, j: (layer, 1, j)),
                  pl.BlockSpec((tm, tn), lambda i, j: (i, j))],
        out_specs=pl.BlockSpec((tm, tn), lambda i, j: (i, j)),
        compiler_params=_params(("parallel", "arbitrary")),
        name=name,
    )(x1, x2, w, w, r)


def _ffn_up_kernel(x_ref, wg_ref, wu_ref, o_ref):
    x = x_ref[...]
    g = _dot(x, wg_ref[...].astype(BF16))
    u = _dot(x, wu_ref[...].astype(BF16))
    o_ref[...] = (g * jax.nn.sigmoid(g) * u).astype(o_ref.dtype)


WIDE_ROW_CAP = 1536
F32_WEIGHT_COL_CAP = 256


def _ffn_up(x, wg, wu, layer):
    m, k = x.shape
    n = wg.shape[2]
    tm, tn = _row_tile(m, WIDE_ROW_CAP), _col_tile(n, F32_WEIGHT_COL_CAP)
    return pl.pallas_call(
        _ffn_up_kernel,
        out_shape=jax.ShapeDtypeStruct((m, n), BF16),
        grid=(m // tm, n // tn),
        in_specs=[pl.BlockSpec((tm, k), lambda i, j: (i, 0)), _wspec(layer, k, tn), _wspec(layer, k, tn)],
        out_specs=pl.BlockSpec((tm, tn), lambda i, j: (i, j)),
        compiler_params=_params(("parallel", "arbitrary")),
        name="ffn_up",
    )(x, wg, wu)


FFN_DOWN_COL_CAP = 256


def _lru_kernel(xa_ref, ga_ref, h0_ref, buf_ref, cw_ref, cb_ref, wa_ref, ba_ref, wx_ref, bx_ref, lam_ref,
                y_ref, ht_ref, nbuf_ref):
    t_len = xa_ref.shape[0]
    x = xa_ref[...]
    buf = buf_ref[...]
    cw = cw_ref[...]
    row = lax.broadcasted_iota(jnp.int32, x.shape, 0)
    nb = A_CONV - 1
    xc = x * cw[nb:nb + 1]
    for s in range(1, A_CONV):
        xs = pltpu.roll(x, s, 0)
        for r in range(s):
            xs = jnp.where(row == r, buf[nb + r - s:nb + r - s + 1], xs)
        xc = xc + xs * cw[nb - s:nb - s + 1]
    xc = xc + cb_ref[...]
    xcb = xc.astype(BF16)
    r_g = jax.nn.sigmoid(_dot(xcb, wa_ref[...].astype(BF16)) + ba_ref[...])
    i_g = jax.nn.sigmoid(_dot(xcb, wx_ref[...].astype(BF16)) + bx_ref[...])
    nl = -lam_ref[...]
    softplus = jnp.maximum(nl, 0.0) + jnp.log1p(jnp.exp(-jnp.abs(nl)))
    log_a = -LRU_C * r_g * softplus
    a = jnp.exp(log_a)
    u = jnp.sqrt(-jnp.tanh(log_a) * (a * a + 1.0)) * i_g * xc
    d = 1
    while d < t_len:
        keep = row >= d
        a_prev = pltpu.roll(a, d, 0)
        u_prev = pltpu.roll(u, d, 0)
        u = jnp.where(keep, a * u_prev + u, u)
        a = jnp.where(keep, a * a_prev, a)
        d *= 2
    h = a * h0_ref[...] + u
    y_ref[...] = (h * _gelu_tanh(ga_ref[...])).astype(y_ref.dtype)
    ht_ref[...] = h[t_len - 1:t_len]
    nbuf_ref[...] = x[t_len - nb:t_len]


def _into_zeros(kernel_fn, n_in, out_rows, width, dtype):
    def wrapped(*refs):
        return kernel_fn(*refs[:n_in], *refs[n_in + 1:])
    return wrapped, pl.BlockSpec(memory_space=pl.ANY), jnp.zeros((out_rows, width), dtype), {n_in: 0}


def _lru_mixer(z, row0, n_seq, t_len, h0, buf, conv_w, conv_b, wa, ba, wx, bx, lam, out_dtype, out_rows=None):
    assert row0 % t_len == 0 and t_len >= A_CONV - 1 and t_len % 8 == 0
    rb0 = row0 // t_len
    nb = A_CONV - 1
    out_rows = out_rows or n_seq * t_len
    vec = lambda v: v.reshape(1, D_A)
    cspec = pl.BlockSpec((1, A_BLK), lambda s, h: (0, h))
    wspec = pl.BlockSpec((None, A_BLK, A_BLK), lambda s, h: (h, 0, 0))
    body, base_spec, base, alias = _into_zeros(_lru_kernel, 11, out_rows, D_A, out_dtype)
    return pl.pallas_call(
        body,
        out_shape=(jax.ShapeDtypeStruct((out_rows, D_A), out_dtype),
                   jax.ShapeDtypeStruct((n_seq, 1, D_A), F32),
                   jax.ShapeDtypeStruct((n_seq, nb, D_A), F32)),
        grid=(n_seq, A_HEADS),
        in_specs=[pl.BlockSpec((t_len, A_BLK), lambda s, h: (rb0 + s, COL_XA + h)),
                  pl.BlockSpec((t_len, A_BLK), lambda s, h: (rb0 + s, COL_GA + h)),
                  pl.BlockSpec((None, 1, A_BLK), lambda s, h: (s, 0, h)),
                  pl.BlockSpec((None, nb, A_BLK), lambda s, h: (s, 0, h)),
                  pl.BlockSpec((A_CONV, A_BLK), lambda s, h: (0, h)),
                  cspec, wspec, cspec, wspec, cspec, cspec, base_spec],
        out_specs=(pl.BlockSpec((t_len, A_BLK), lambda s, h: (s, h)),
                   pl.BlockSpec((None, 1, A_BLK), lambda s, h: (s, 0, h)),
                   pl.BlockSpec((None, nb, A_BLK), lambda s, h: (s, 0, h))),
        input_output_aliases=alias,
        compiler_params=_params(("parallel", "parallel")),
        name="rglru_mixer",
    )(z, z, h0.reshape(n_seq, 1, D_A), buf, conv_w, vec(conv_b), wa, vec(ba), wx, vec(bx), vec(lam), base)


def _compress_kernel(x_ref, pe_ref, w1_ref, b1_ref, w2_ref, o_ref):
    n_sub = o_ref.shape[0]
    pe = pe_ref[...]
    y0 = jnp.zeros((n_sub, HEAD_DIM), F32)
    y1 = jnp.zeros((n_sub, HEAD_DIM), F32)
    for j in range(CMP_STRIDE):
        xj = x_ref[pl.ds(j, n_sub, stride=CMP_STRIDE), :]
        y0 = y0 + _dot((xj + pe[j:j + 1]).astype(BF16), w1_ref[j])
        y1 = y1 + _dot((xj + pe[CMP_STRIDE + j:CMP_STRIDE + j + 1]).astype(BF16), w1_ref[CMP_STRIDE + j])
    hid = y0 + pltpu.roll(y1, n_sub - 1, 0)
    hid = _gelu_tanh(hid + b1_ref[...])
    o_ref[...] = _dot(hid.astype(BF16), w2_ref[...]).astype(o_ref.dtype)


def _compress(x2d, x_index, n_seq, t_len, pe, w1, b1, w2):
    assert SUB_PER_CMP == 2 and t_len % (8 * CMP_STRIDE) == 0
    n_sub = t_len // CMP_STRIDE
    nt = 2 * KV_HEADS
    return pl.pallas_call(
        _compress_kernel,
        out_shape=jax.ShapeDtypeStruct((n_seq, nt, n_sub, HEAD_DIM), BF16),
        grid=(n_seq, nt),
        in_specs=[pl.BlockSpec((t_len, HEAD_DIM), x_index),
                  pl.BlockSpec((None, CMP_LEN, HEAD_DIM), lambda s, c: (c // KV_HEADS, 0, 0)),
                  pl.BlockSpec((None, CMP_LEN, HEAD_DIM, HEAD_DIM), lambda s, c: (c // KV_HEADS, 0, 0, 0)),
                  pl.BlockSpec((None, 1, HEAD_DIM), lambda s, c: (c // KV_HEADS, 0, 0)),
                  pl.BlockSpec((None, HEAD_DIM, HEAD_DIM), lambda s, c: (c // KV_HEADS, 0, 0))],
        out_specs=pl.BlockSpec((None, None, n_sub, HEAD_DIM), lambda s, c: (s, c, 0, 0)),
        compiler_params=_params(("parallel", "arbitrary")),
        name="nsa_compress",
    )(x2d, pe, w1.astype(BF16), b1.reshape(2, 1, HEAD_DIM), w2.astype(BF16))


def _head_rows(ref, c, n_tok):
    return ref[pl.ds(c, n_tok, stride=2 * KV_HEADS), :]


def _page_gather_kernel(pt_ref, *refs):
    page_refs, o_ref = refs[:-1], refs[-1]
    for k, c_ref in enumerate(page_refs):
        for c in range(2 * KV_HEADS):
            o_ref[c, k * PAGE_SIZE:(k + 1) * PAGE_SIZE, :] = _head_rows(c_ref, c, PAGE_SIZE)


def _pages_per_step(n_pages, want=4):
    pps = want
    while n_pages % pps:
        pps //= 2
    return pps


def _page_gather(cache_rows, layer, page_table):
    nb, n_pages = page_table.shape
    nt = 2 * KV_HEADS
    pps = _pages_per_step(n_pages, 16)

    def page_spec(k):
        return pl.BlockSpec((None, None, PAGE_SIZE * nt, HEAD_DIM), lambda b, p, pt: (layer, pt[b, p * pps + k], 0, 0))

    return pl.pallas_call(
        _page_gather_kernel,
        out_shape=jax.ShapeDtypeStruct((nb, nt, n_pages * PAGE_SIZE, HEAD_DIM), cache_rows.dtype),
        grid_spec=pltpu.PrefetchScalarGridSpec(
            num_scalar_prefetch=1, grid=(nb, n_pages // pps),
            in_specs=[page_spec(k) for k in range(pps)],
            out_specs=pl.BlockSpec((None, nt, pps * PAGE_SIZE, HEAD_DIM), lambda b, p, pt: (b, 0, p, 0))),
        compiler_params=_params(("parallel", "arbitrary")),
        name="page_gather",
    )(page_table, *([cache_rows] * pps))


def _stack_heads(q, g0):
    return jnp.concatenate([q[:, (g0 + p) * HEAD_DIM:(g0 + p + 1) * HEAD_DIM] for p in range(HPG)], axis=0).astype(BF16)


def _masked_softmax(s, ok):
    sb = jnp.where(ok, s, NEG_INF)
    m = jnp.max(sb, axis=-1, keepdims=True)
    e = jnp.exp(sb - m)
    return e / jnp.sum(e, axis=-1, keepdims=True)


def _cmp_branch(q4, kc, vc, t_rows, slope, n_cmp):
    ncp = kc.shape[0]
    s = _dot_nt(q4, kc) * QK_SCALE
    col = lax.broadcasted_iota(jnp.int32, s.shape, 1)
    dist_i = t_rows - (col * CMP_STRIDE + (CMP_LEN - 1))
    ok = (dist_i >= 0) & (col < n_cmp)
    p = _masked_softmax(s - _lanes(slope, ncp) * dist_i.astype(F32), ok)
    p = jnp.where(ok, p, 0.0)
    return _dot(p.astype(BF16), vc), p


def _select_blocks(imp, pool, t_q, n_sel):
    imp_sel = _dot_f32_exact(imp, pool)
    jcol = lax.broadcasted_iota(jnp.int32, imp_sel.shape, 1)
    cur = lax.shift_right_logical(t_q, int(np.log2(SEL_BLOCK)))
    forced = (jcol == 0) | (jcol == cur) | (jcol == cur - 1)
    valid = jcol * SEL_BLOCK <= t_q
    score = jnp.where(forced, FORCE_SCORE, jnp.where(valid, imp_sel, -1.0))
    score = jnp.where(jcol < n_sel, score, -2.0)
    rank = jnp.zeros(score.shape, F32)
    for i in range(n_sel):
        ci = score[:, i:i + 1]
        before = (ci > score) | ((ci == score) & (jcol > i))
        rank = rank + jnp.where(before, 1.0, 0.0)
    top = min(SEL_TOP, n_sel)
    return jnp.where(rank < float(top), 1.0, 0.0)


def _pool_matrix(n_cmp, ncp, n_sel, nsp):
    assert n_cmp + SUB_PER_CMP - 1 <= n_sel * SUB_PER_SEL
    m = np.zeros((ncp, nsp), np.float32)
    for c in range(n_cmp):
        for n in range(SUB_PER_CMP):
            m[c, (c + n) // SUB_PER_SEL] += 1.0 / SUB_PER_CMP
    return jnp.asarray(m, BF16)


def _slope_rows(t_rep):
    h = np.arange(1, B_HEADS + 1, dtype=np.float64)
    s = np.exp2(-8.0 * h / B_HEADS).astype(np.float32).reshape(KV_HEADS, HPG, 1, 1)
    return jnp.asarray(np.broadcast_to(s, (KV_HEADS, HPG, t_rep, LANE)).reshape(KV_HEADS, HPG * t_rep, LANE))


def _expand_matrix(shape, blk0):
    j = lax.broadcasted_iota(jnp.int32, shape, 0)
    k = lax.broadcasted_iota(jnp.int32, shape, 1)
    return jnp.where(j == blk0 + lax.shift_right_logical(k, int(np.log2(SEL_BLOCK))), 1.0, 0.0).astype(BF16)


def _gate_combine(gs, p, tq, o_c, o_s, o_w):
    rows = slice(p * tq, (p + 1) * tq)
    return (gs[:, p:p + 1] * o_c[rows] + gs[:, HPG + p:HPG + p + 1] * o_s[rows]
            + gs[:, 2 * HPG + p:2 * HPG + p + 1] * o_w[rows])


def _attn_prompt_kernel(q_ref, kc_ref, vc_ref, ks_ref, vs_ref, kw_ref, vw_ref, gt_ref, slope_ref, pool_ref,
                        o_ref, ksb, vsb, kwb, vwb, *, n_cmp, n_sel):
    qi = pl.program_id(2)
    tq = q_ref.shape[0]
    t_total = ks_ref.shape[0]
    rr = HPG * tq

    @pl.when(qi == 0)
    def _():
        ksb[...] = ks_ref[...].astype(BF16)
        vsb[...] = vs_ref[...].astype(BF16)
        kwb[...] = kw_ref[...].astype(BF16)
        vwb[...] = vw_ref[...].astype(BF16)

    t0 = qi * tq
    q4 = _stack_heads(q_ref[...], 0)
    slope = slope_ref[...]

    def t_rows(width):
        r = lax.broadcasted_iota(jnp.int32, (rr, width), 0)
        return t0 + (r & (tq - 1))

    ncp = kc_ref.shape[0]
    o_c, p_c = _cmp_branch(q4, kc_ref[...], vc_ref[...], t_rows(ncp), slope, n_cmp)
    imp = p_c[0:tq]
    for p in range(1, HPG):
        imp = imp + p_c[p * tq:(p + 1) * tq]
    nsp = pool_ref.shape[1]
    t_q = t0 + lax.broadcasted_iota(jnp.int32, (tq, nsp), 0)
    sel = _select_blocks(imp, pool_ref[...], t_q, n_sel).astype(BF16)
    sel4 = jnp.concatenate([sel] * HPG, axis=0)

    ck = ATT_CK
    tr_ck = t_rows(ck)
    slope_ck = _lanes(slope, ck)
    col_ck = lax.broadcasted_iota(jnp.int32, (rr, ck), 1)

    def sel_chunk(c, carry):
        m, l, acc = carry
        k0 = pl.multiple_of(c * ck, ck)
        s = _dot_nt(q4, ksb[pl.ds(k0, ck), :]) * QK_SCALE
        dist_i = tr_ck - (k0 + col_ck)
        selx = _dot(sel4, _expand_matrix((nsp, ck), c * (ck // SEL_BLOCK)))
        ok = (dist_i >= 0) & (selx > 0.5)
        sb = jnp.where(ok, s - slope_ck * dist_i.astype(F32), NEG_INF)
        m_new = jnp.maximum(m, jnp.max(sb, axis=-1, keepdims=True))
        alpha = jnp.exp(m - m_new)
        e = jnp.exp(sb - m_new)
        l = alpha * l + jnp.sum(e, axis=-1, keepdims=True)
        acc = alpha * acc + _dot(e.astype(BF16), vsb[pl.ds(k0, ck), :])
        return m_new, l, acc

    n_chunks = (t0 + tq + ck - 1) // ck
    m0 = jnp.full((rr, 1), NEG_INF, F32)
    _, l_s, acc_s = lax.fori_loop(0, n_chunks, sel_chunk, (m0, jnp.zeros((rr, 1), F32), jnp.zeros((rr, HEAD_DIM), F32)))
    o_s = acc_s / l_s

    wk = WINDOW + tq
    w0 = pl.multiple_of(jnp.maximum(t0 - WINDOW, 0), tq)
    s = _dot_nt(q4, kwb[pl.ds(w0, wk), :]) * QK_SCALE
    dist_i = t_rows(wk) - (w0 + lax.broadcasted_iota(jnp.int32, (rr, wk), 1))
    ok = (dist_i >= 0) & (dist_i < WINDOW)
    p_w = _masked_softmax(s - _lanes(slope, wk) * dist_i.astype(F32), ok)
    o_w = _dot(p_w.astype(BF16), vwb[pl.ds(w0, wk), :])

    gs = jax.nn.sigmoid(gt_ref[...])
    for p in range(HPG):
        o_ref[:, p * HEAD_DIM:(p + 1) * HEAD_DIM] = _gate_combine(gs, p, tq, o_c, o_s, o_w).astype(o_ref.dtype)


def _attn_prompt(z, gt, kvc_cmp, n_seq, t_len, out_rows=None):
    tq = ATT_TQ
    assert t_len % ATT_CK == 0 and t_len >= WINDOW + tq and tq & (tq - 1) == 0
    n_cmp = (t_len - CMP_LEN) // CMP_STRIDE + 1
    n_sel = -(-t_len // SEL_BLOCK)
    ncp = kvc_cmp.shape[2]
    nsp = _round_up(n_sel, LANE)
    nq = t_len // tq
    rr = HPG * tq
    kvspec = lambda col0: pl.BlockSpec((t_len, HEAD_DIM), lambda b, g, i: (b, col0 + g))
    out_rows = out_rows or n_seq * t_len
    body, base_spec, base, alias = _into_zeros(functools.partial(_attn_prompt_kernel, n_cmp=n_cmp, n_sel=n_sel),
                                               10, out_rows, D_B, BF16)
    return pl.pallas_call(
        body,
        out_shape=jax.ShapeDtypeStruct((out_rows, D_B), BF16),
        grid=(n_seq, KV_HEADS, nq),
        in_specs=[pl.BlockSpec((tq, HPG * HEAD_DIM), lambda b, g, i: (b * nq + i, COL_Q // HPG + g)),
                  pl.BlockSpec((None, None, ncp, HEAD_DIM), lambda b, g, i: (b, g, 0, 0)),
                  pl.BlockSpec((None, None, ncp, HEAD_DIM), lambda b, g, i: (b, KV_HEADS + g, 0, 0)),
                  kvspec(COL_KVS), kvspec(COL_KVS + KV_HEADS), kvspec(COL_KVW), kvspec(COL_KVW + KV_HEADS),
                  pl.BlockSpec((tq, LANE), lambda b, g, i: (b * nq + i, g)),
                  pl.BlockSpec((None, rr, LANE), lambda b, g, i: (g, 0, 0)),
                  pl.BlockSpec((ncp, nsp), lambda b, g, i: (0, 0)), base_spec],
        out_specs=pl.BlockSpec((tq, HPG * HEAD_DIM), lambda b, g, i: (b * nq + i, g)),
        scratch_shapes=[pltpu.VMEM((t_len, HEAD_DIM), BF16)] * 4,
        input_output_aliases=alias,
        compiler_params=_params(("parallel", "parallel", "arbitrary")),
        name="nsa_prompt_attention",
    )(z, kvc_cmp, kvc_cmp, z, z, z, z, gt, _slope_rows(tq), _pool_matrix(n_cmp, ncp, n_sel, nsp), base)


def _attn_dec_select_kernel(q_ref, kc_ref, vc_ref, slope_ref, pool_ref, oc_ref, sel_ref, *, n_cmp, n_sel, past_len):
    ts = q_ref.shape[0]
    rr = HPG * ts
    ncp = kc_ref.shape[0]
    q4 = _stack_heads(q_ref[...], 0)
    t_rows = past_len + (lax.broadcasted_iota(jnp.int32, (rr, ncp), 0) & (ts - 1))
    o_c, p_c = _cmp_branch(q4, kc_ref[...], vc_ref[...], t_rows, slope_ref[...], n_cmp)
    imp = p_c[0:ts]
    for p in range(1, HPG):
        imp = imp + p_c[p * ts:(p + 1) * ts]
    nsp = pool_ref.shape[1]
    t_q = past_len + lax.broadcasted_iota(jnp.int32, (ts, nsp), 0)
    oc_ref[...] = o_c
    sel_ref[...] = _select_blocks(imp, pool_ref[...], t_q, n_sel)


def _attn_dec_select(z, row0, kvc_cmp, n_seq, ts, past_len, n_cmp, n_sel):
    assert row0 % ts == 0 and ts & (ts - 1) == 0 and ts % 8 == 0
    rb0 = row0 // ts
    ncp = kvc_cmp.shape[2]
    nsp = _round_up(n_sel, LANE)
    rr = HPG * ts
    return pl.pallas_call(
        functools.partial(_attn_dec_select_kernel, n_cmp=n_cmp, n_sel=n_sel, past_len=past_len),
        out_shape=(jax.ShapeDtypeStruct((n_seq, KV_HEADS, rr, HEAD_DIM), F32),
                   jax.ShapeDtypeStruct((n_seq, KV_HEADS, ts, nsp), F32)),
        grid=(n_seq, KV_HEADS),
        in_specs=[pl.BlockSpec((ts, HPG * HEAD_DIM), lambda b, g: (rb0 + b, COL_Q // HPG + g)),
                  pl.BlockSpec((None, None, ncp, HEAD_DIM), lambda b, g: (b, g, 0, 0)),
                  pl.BlockSpec((None, None, ncp, HEAD_DIM), lambda b, g: (b, KV_HEADS + g, 0, 0)),
                  pl.BlockSpec((None, rr, LANE), lambda b, g: (g, 0, 0)),
                  pl.BlockSpec((ncp, nsp), lambda b, g: (0, 0))],
        out_specs=(pl.BlockSpec((None, None, rr, HEAD_DIM), lambda b, g: (b, g, 0, 0)),
                   pl.BlockSpec((None, None, ts, nsp), lambda b, g: (b, g, 0, 0))),
        compiler_params=_params(("parallel", "parallel")),
        name="nsa_decode_select",
    )(z, kvc_cmp, kvc_cmp, _slope_rows(ts), _pool_matrix(n_cmp, ncp, n_sel, nsp))


def _attn_dec_kernel(pt_ref, q_ref, sel_ref, *rest, past_len, n_steps, pages_per_step):
    page_refs = rest[:pages_per_step]
    kvs_ref, kvw_ref, win_ref, oc_ref, gt_ref, slope_ref, o_ref, nwin_ref, m_sc, l_sc, acc_sc = rest[pages_per_step:]
    pg = pl.program_id(1)
    ts = q_ref.shape[0]
    rr = HPG * ts
    nsp = sel_ref.shape[-1]
    pad_rows = LANE - ts
    row = lax.broadcasted_iota(jnp.int32, (rr, LANE), 0)
    col = lax.broadcasted_iota(jnp.int32, (rr, LANE), 1)
    t_rows = past_len + (row & (ts - 1))

    @pl.when(pg == 0)
    def _():
        m_sc[...] = jnp.full(m_sc.shape, NEG_INF, F32)
        l_sc[...] = jnp.zeros(l_sc.shape, F32)
        acc_sc[...] = jnp.zeros(acc_sc.shape, F32)

    def sel_update(g, q4, kk, vv, dist_i, expand, extra_ok):
        width = kk.shape[0]
        s = _dot_nt(q4, kk) * QK_SCALE
        sel4 = jnp.concatenate([sel_ref[g].astype(BF16)] * HPG, axis=0)
        ok = (dist_i >= 0) & (_dot(sel4, expand) > 0.5) & extra_ok
        sb = jnp.where(ok, s - _lanes(slope_ref[g], width) * dist_i.astype(F32), NEG_INF)
        m_old = m_sc[g]
        m_new = jnp.maximum(m_old, jnp.max(sb, axis=-1, keepdims=True))
        alpha = jnp.exp(m_old - m_new)
        e = jnp.exp(sb - m_new)
        l_sc[g] = alpha * l_sc[g] + jnp.sum(e, axis=-1, keepdims=True)
        acc_sc[g] = alpha * acc_sc[g] + _dot(e.astype(BF16), vv)
        m_sc[g] = m_new

    @pl.when(pg < n_steps)
    def _():
        width = pages_per_step * PAGE_SIZE
        k0 = pg * width
        expand = _expand_matrix((nsp, width), pg * (width // SEL_BLOCK))
        wrow_ = lax.broadcasted_iota(jnp.int32, (rr, width), 0)
        wcol_ = lax.broadcasted_iota(jnp.int32, (rr, width), 1)
        dist_i = (past_len + (wrow_ & (ts - 1))) - (k0 + wcol_)
        for g in range(KV_HEADS):
            q4 = _stack_heads(q_ref[...], g * HPG)
            kk = jnp.concatenate([_head_rows(r, g, PAGE_SIZE) for r in page_refs], axis=0).astype(BF16)
            vv = jnp.concatenate([_head_rows(r, KV_HEADS + g, PAGE_SIZE) for r in page_refs], axis=0).astype(BF16)
            sel_update(g, q4, kk, vv, dist_i, expand, True)

    @pl.when(pg == n_steps)
    def _():
        zpad = jnp.zeros((pad_rows, HEAD_DIM), F32)
        gs = jax.nn.sigmoid(gt_ref[...])
        jj = lax.broadcasted_iota(jnp.int32, (nsp, LANE), 0)
        kk_i = lax.broadcasted_iota(jnp.int32, (nsp, LANE), 1)
        new_blk = lax.shift_right_logical(past_len + kk_i, int(np.log2(SEL_BLOCK)))
        expand_new = jnp.where(jj == new_blk, 1.0, 0.0).astype(BF16)
        is_new = col < ts
        nt = 2 * KV_HEADS
        wlen = win_ref.shape[0] // nt
        wrow = lax.broadcasted_iota(jnp.int32, (rr, wlen), 0)
        wcol = lax.broadcasted_iota(jnp.int32, (rr, wlen), 1)
        dist_w = (past_len + (wrow & (ts - 1))) - (past_len - wlen + wcol)
        ok_w = (dist_w >= 0) & (dist_w < WINDOW)
        dist_n = t_rows - (past_len + col)
        ok_n = (dist_n >= 0) & (dist_n < WINDOW) & is_new
        for g in range(KV_HEADS):
            q4 = _stack_heads(q_ref[...], g * HPG)
            ksl = slice(g * HEAD_DIM, (g + 1) * HEAD_DIM)
            vsl = slice(D_KV + g * HEAD_DIM, D_KV + (g + 1) * HEAD_DIM)
            kn = jnp.concatenate([kvs_ref[:, ksl], zpad], axis=0).astype(BF16)
            vn = jnp.concatenate([kvs_ref[:, vsl], zpad], axis=0).astype(BF16)
            sel_update(g, q4, kn, vn, dist_n, expand_new, is_new)
            o_s = acc_sc[g] / l_sc[g]
            slope = slope_ref[g]
            s1 = _dot_nt(q4, _head_rows(win_ref, g, wlen).astype(BF16)) * QK_SCALE
            s1 = jnp.where(ok_w, s1 - _lanes(slope, wlen) * dist_w.astype(F32), NEG_INF)
            kwn = jnp.concatenate([kvw_ref[:, ksl], zpad], axis=0).astype(BF16)
            vwn = jnp.concatenate([kvw_ref[:, vsl], zpad], axis=0).astype(BF16)
            s2 = _dot_nt(q4, kwn) * QK_SCALE
            s2 = jnp.where(ok_n, s2 - slope * dist_n.astype(F32), NEG_INF)
            m = jnp.maximum(jnp.max(s1, axis=-1, keepdims=True), jnp.max(s2, axis=-1, keepdims=True))
            e1 = jnp.exp(s1 - m)
            e2 = jnp.exp(s2 - m)
            inv = 1.0 / (jnp.sum(e1, axis=-1, keepdims=True) + jnp.sum(e2, axis=-1, keepdims=True))
            o_w = (_dot((e1 * inv).astype(BF16), _head_rows(win_ref, KV_HEADS + g, wlen).astype(BF16))
                   + _dot((e2 * inv).astype(BF16), vwn))
            o_c = oc_ref[g]
            for p in range(HPG):
                c0 = (g * HPG + p) * HEAD_DIM
                o_ref[:, c0:c0 + HEAD_DIM] = _gate_combine(gs[:, g * LANE:(g + 1) * LANE], p, ts, o_c, o_s, o_w)
        keep = (wlen - ts) * nt
        nwin_ref[0:keep, :] = win_ref[ts * nt:wlen * nt, :]
        for c in range(nt):
            nwin_ref[pl.ds(keep + c, ts, stride=nt), :] = kvw_ref[:, c * HEAD_DIM:(c + 1) * HEAD_DIM]


def _attn_decode(z, gt, row0, n_seq, ts, page_table, cache_rows, layer, win_rows, o_c, sel, past_len):
    n_pages = page_table.shape[1]
    pps = _pages_per_step(n_pages)
    n_steps = n_pages // pps
    rb0 = row0 // ts
    rr = HPG * ts
    nsp = sel.shape[-1]
    nt = 2 * KV_HEADS
    wrows = win_rows.shape[2]
    w = 2 * D_KV
    assert wrows == WINDOW * nt and past_len % SEL_BLOCK == 0 and PAGE_SIZE == LANE
    last = n_steps - 1

    def page_spec(k):
        return pl.BlockSpec((None, None, PAGE_SIZE * nt, HEAD_DIM),
                            lambda b, p, pt: (layer, pt[b, jnp.minimum(p, last) * pps + k], 0, 0))

    return pl.pallas_call(
        functools.partial(_attn_dec_kernel, past_len=past_len, n_steps=n_steps, pages_per_step=pps),
        out_shape=(jax.ShapeDtypeStruct((n_seq * ts, D_B), F32),
                   jax.ShapeDtypeStruct((n_seq, wrows, HEAD_DIM), F32)),
        grid_spec=pltpu.PrefetchScalarGridSpec(
            num_scalar_prefetch=1, grid=(n_seq, n_steps + 1),
            in_specs=[pl.BlockSpec((ts, D_B), lambda b, p, pt: (rb0 + b, COL_Q * LANE // D_B)),
                      pl.BlockSpec((None, KV_HEADS, ts, nsp), lambda b, p, pt: (b, 0, 0, 0))]
                     + [page_spec(k) for k in range(pps)]
                     + [pl.BlockSpec((ts, w), lambda b, p, pt: (rb0 + b, COL_KVS * LANE // w)),
                        pl.BlockSpec((ts, w), lambda b, p, pt: (rb0 + b, COL_KVW * LANE // w)),
                        pl.BlockSpec((None, None, wrows, HEAD_DIM), lambda b, p, pt: (layer, b, 0, 0)),
                        pl.BlockSpec((None, KV_HEADS, rr, HEAD_DIM), lambda b, p, pt: (b, 0, 0, 0)),
                        pl.BlockSpec((ts, KV_HEADS * LANE), lambda b, p, pt: (rb0 + b, 0)),
                        pl.BlockSpec((KV_HEADS, rr, LANE), lambda b, p, pt: (0, 0, 0))],
            out_specs=(pl.BlockSpec((ts, D_B), lambda b, p, pt: (b, 0)),
                       pl.BlockSpec((None, wrows, HEAD_DIM), lambda b, p, pt: (b, 0, 0))),
            scratch_shapes=[pltpu.VMEM((KV_HEADS, rr, 1), F32), pltpu.VMEM((KV_HEADS, rr, 1), F32),
                            pltpu.VMEM((KV_HEADS, rr, HEAD_DIM), F32)]),
        compiler_params=_params(("parallel", "arbitrary")),
        name="nsa_decode_attention",
    )(page_table, z, sel, *([cache_rows] * pps), z, z, win_rows, o_c, gt, _slope_rows(ts))


def _sconv_kernel(bg_ref, cg_ref, v_ref, buf_ref, cw_ref, y_ref, nbuf_ref):
    t_len = cg_ref.shape[0]
    nb = C_CONV - 1
    cv = cg_ref[...] * v_ref[...]
    buf = buf_ref[...]
    cw = cw_ref[...]
    row = lax.broadcasted_iota(jnp.int32, cv.shape, 0)
    y = cv * cw[nb:nb + 1]
    for s in range(1, C_CONV):
        xs = pltpu.roll(cv, s, 0)
        for r in range(s):
            xs = jnp.where(row == r, buf[nb + r - s:nb + r - s + 1], xs)
        y = y + xs * cw[nb - s:nb - s + 1]
    y_ref[...] = (bg_ref[...] * y).astype(y_ref.dtype)
    nbuf_ref[...] = cv[t_len - nb:t_len]


def _sconv_mixer(zc, row0, n_seq, t_len, buf, conv_w, out_dtype, out_rows=None):
    assert row0 % t_len == 0 and t_len >= C_CONV - 1 and t_len % 8 == 0
    rb0 = row0 // t_len
    nb = C_CONV - 1
    tc = 256 if t_len > 256 else 2048
    nc = D_C // tc
    zspec = lambda third: pl.BlockSpec((t_len, tc), lambda s, c: (rb0 + s, third * nc + c))
    out_rows = out_rows or n_seq * t_len
    body, base_spec, base, alias = _into_zeros(_sconv_kernel, 5, out_rows, D_C, out_dtype)
    return pl.pallas_call(
        body,
        out_shape=(jax.ShapeDtypeStruct((out_rows, D_C), out_dtype),
                   jax.ShapeDtypeStruct((n_seq, nb, D_C), F32)),
        grid=(n_seq, nc),
        in_specs=[zspec(0), zspec(1), zspec(2),
                  pl.BlockSpec((None, nb, tc), lambda s, c: (s, 0, c)),
                  pl.BlockSpec((C_CONV, tc), lambda s, c: (0, c)), base_spec],
        out_specs=(pl.BlockSpec((t_len, tc), lambda s, c: (s, c)),
                   pl.BlockSpec((None, nb, tc), lambda s, c: (s, 0, c))),
        input_output_aliases=alias,
        compiler_params=_params(("parallel", "parallel")),
        name="sconv_mixer",
    )(zc, zc, zc, buf, conv_w, base)


def _gate_weight(w_in):
    nl, k = w_in.shape[:2]
    wg = w_in[:, :, D_IN_MAIN:].reshape(nl, k, 3, KV_HEADS, HPG).transpose(0, 1, 3, 2, 4)
    wg = jnp.pad(wg.reshape(nl, k, KV_HEADS, 3 * HPG), ((0, 0), (0, 0), (0, 0), (0, LANE - 3 * HPG)))
    return wg.reshape(nl, k, KV_HEADS * LANE).astype(BF16)


def _ab_layer(h, hn, i, mp, bp, t_len, bs, ts, past_len, w, cache_kv_cmp, cache_kv_sel, state_win_kv, state_lru_h,
              state_lru_conv, page_table):
    z, kv_rows = _in_proj(hn, w["ab_w_in_bf16"], i)
    gt = _matmul(hn, w["ab_w_gate_bf16"], i, name="ab_gate_proj")
    nt = 2 * KV_HEADS
    rows_view = lambda a: a.reshape(a.shape[0], a.shape[1], -1, HEAD_DIM)
    lru_w = (w["ab_conv_w"][i], w["ab_conv_b"][i], w["ab_gate_a_w"][i], w["ab_gate_a_b"][i], w["ab_gate_x_w"][i],
             w["ab_gate_x_b"][i], w["ab_lru_lambda"][i])
    cmp_w = (w["ab_cmp_pe"][i], w["ab_cmp_w1"][i], w["ab_cmp_b1"][i], w["ab_cmp_w2"][i])
    n_pages = page_table.shape[1]
    tk = past_len + ts
    n_cmp_s = (tk - CMP_LEN) // CMP_STRIDE + 1
    n_sel_s = -(-tk // SEL_BLOCK)
    assert n_cmp_s == (past_len - CMP_LEN) // CMP_STRIDE + 1

    ya_p, hT_p, nbuf_p = _lru_mixer(z, 0, bp, t_len, jnp.zeros((bp, D_A), F32), jnp.zeros((bp, A_CONV - 1, D_A), F32),
                                    *lru_w, out_dtype=BF16, out_rows=mp + bs * ts)
    kvc_p = _compress(z, lambda s, c: (s, COL_KVC + c), bp, t_len, *cmp_w)
    ob_p = _attn_prompt(z, gt, kvc_p, bp, t_len, out_rows=mp + bs * ts)

    ya_s, hT_s, nbuf_s = _lru_mixer(z, mp, bs, ts, state_lru_h[i], state_lru_conv[i], *lru_w, out_dtype=F32)
    past_cmp = _page_gather(rows_view(cache_kv_cmp), i, page_table)
    kvc_s = _compress(past_cmp.reshape(bs * nt * past_len, HEAD_DIM), lambda s, c: (s * nt + c, 0), bs, past_len, *cmp_w)
    oc_s, sel_s = _attn_dec_select(z, mp, kvc_s, bs, ts, past_len, n_cmp_s, n_sel_s)
    ob_s, nwin_s = _attn_decode(z, gt, mp, bs, ts, page_table, rows_view(cache_kv_sel), i, rows_view(state_win_kv),
                                oc_s, sel_s, past_len)

    ya = lax.dynamic_update_slice(ya_p, ya_s.astype(BF16), (mp, 0))
    ob = lax.dynamic_update_slice(ob_p, ob_s.astype(BF16), (mp, 0))
    h = _matmul2_residual(ya, ob, w["ab_w_out_bf16"], i, h, name="ab_out_proj")

    kv5 = lambda a, nb_, tt: a.reshape(nb_, tt, 2, KV_HEADS, HEAD_DIM)
    wl = min(WINDOW, t_len)
    kvw_p = kv5(kv_rows[2, :mp * nt], bp, t_len)
    outs_p = (kv5(kv_rows[0, :mp * nt], bp, t_len), kv5(kv_rows[1, :mp * nt], bp, t_len),
              kvw_p[:, t_len - wl:], hT_p.reshape(bp, D_A), nbuf_p)
    outs_s = (kv5(kv_rows[0, mp * nt:], bs, ts), kv5(kv_rows[1, mp * nt:], bs, ts),
              kv5(nwin_s, bs, WINDOW), hT_s.reshape(bs, D_A), nbuf_s)
    return h, outs_p, outs_s


def _c_layer(h, hn, i, mp, bp, t_len, bs, ts, w, state_sconv):
    zc = _matmul(hn, w["c_w_in"], i, name="c_in_proj", row_cap=WIDE_ROW_CAP, col_cap=F32_WEIGHT_COL_CAP)
    y_p, nbuf_p = _sconv_mixer(zc, 0, bp, t_len, jnp.zeros((bp, C_CONV - 1, D_C), F32), w["c_conv_w"][i], BF16,
                               out_rows=mp + bs * ts)
    y_s, nbuf_s = _sconv_mixer(zc, mp, bs, ts, state_sconv[i], w["c_conv_w"][i], F32)
    y = lax.dynamic_update_slice(y_p, y_s.astype(BF16), (mp, 0))
    h = _matmul_residual(y, w["c_w_out_bf16"], i, h, name="c_out_proj")
    return h, nbuf_p, nbuf_s


def kernel(x_prompt, x_sample, cache_kv_cmp, cache_kv_sel, state_win_kv, state_lru_h, state_lru_conv, state_sconv,
           page_table, norm_mix, norm_ffn, norm_final, ab_w_in, ab_conv_w, ab_conv_b, ab_gate_a_w, ab_gate_a_b,
           ab_gate_x_w, ab_gate_x_b, ab_lru_lambda, ab_cmp_pe, ab_cmp_w1, ab_cmp_b1, ab_cmp_w2, ab_w_out,
           c_w_in, c_conv_w, c_w_out, ffn_w_gate, ffn_w_up, ffn_w_down):
    w = dict(ab_w_in_bf16=ab_w_in.astype(BF16), ab_w_gate_bf16=_gate_weight(ab_w_in), ab_conv_w=ab_conv_w,
             ab_conv_b=ab_conv_b, ab_gate_a_w=ab_gate_a_w, ab_gate_a_b=ab_gate_a_b, ab_gate_x_w=ab_gate_x_w,
             ab_gate_x_b=ab_gate_x_b, ab_lru_lambda=ab_lru_lambda, ab_cmp_pe=ab_cmp_pe, ab_cmp_w1=ab_cmp_w1,
             ab_cmp_b1=ab_cmp_b1, ab_cmp_w2=ab_cmp_w2, ab_w_out_bf16=ab_w_out.astype(BF16),
             c_w_in=c_w_in, c_conv_w=c_conv_w, c_w_out_bf16=c_w_out.astype(BF16))
    ffn_w_down_bf16 = ffn_w_down.astype(BF16)
    bp, t_len = x_prompt.shape[:2]
    bs, ts = x_sample.shape[:2]
    mp, ms = bp * t_len, bs * ts
    past_len = page_table.shape[1] * PAGE_SIZE
    h = jnp.concatenate([x_prompt.reshape(mp, D_MODEL), x_sample.reshape(ms, D_MODEL)], axis=0)
    p_ab, s_ab, p_c, s_c = [], [], [], []
    for layer in range(DEPTH):
        hn = _rmsnorm(h, norm_mix[layer], BF16)
        i = layer // 2
        if layer % 2 == 0:
            h, op, os_ = _ab_layer(h, hn, i, mp, bp, t_len, bs, ts, past_len, w, cache_kv_cmp, cache_kv_sel,
                                   state_win_kv, state_lru_h, state_lru_conv, page_table)
            p_ab.append(op)
            s_ab.append(os_)
        else:
            h, nbp, nbs = _c_layer(h, hn, i, mp, bp, t_len, bs, ts, w, state_sconv)
            p_c.append(nbp)
            s_c.append(nbs)
        hf = _rmsnorm(h, norm_ffn[layer], BF16)
        hid = _ffn_up(hf, ffn_w_gate, ffn_w_up, layer)
        h = _matmul_residual(hid, ffn_w_down_bf16, layer, h, name="ffn_down", col_cap=FFN_DOWN_COL_CAP)
    y = _rmsnorm(h, norm_final, F32)
    y_prompt = y[:mp].reshape(bp, t_len, D_MODEL)
    y_sample = y[mp:].reshape(bs, ts, D_MODEL)
    stack = lambda rows, j: jnp.stack([r[j] for r in rows])
    return (y_prompt, y_sample,
            stack(p_ab, 0), stack(p_ab, 1), stack(p_ab, 2), stack(p_ab, 3), stack(p_ab, 4), jnp.stack(p_c),
            stack(s_ab, 0), stack(s_ab, 1), stack(s_ab, 2), stack(s_ab, 3), stack(s_ab, 4), jnp.stack(s_c))
```

```python
import functools

import jax
import jax.numpy as jnp
import numpy as np
from jax import lax
from jax.experimental import pallas as pl
from jax.experimental.pallas import tpu as pltpu

D_MODEL = 4096
DEPTH = 4
PAGE_SIZE = 128
N_AB = (DEPTH + 1) // 2
N_C = DEPTH // 2
D_A = D_MODEL // 2
A_HEADS = 16
A_BLK = D_A // A_HEADS
A_CONV = 4
LRU_C = 8.0
B_HEADS = 16
HEAD_DIM = 128
KV_HEADS = 4
HPG = B_HEADS // KV_HEADS
D_B = B_HEADS * HEAD_DIM
D_KV = KV_HEADS * HEAD_DIM
CMP_LEN = 32
CMP_STRIDE = 16
SUB_PER_CMP = CMP_LEN // CMP_STRIDE
SEL_BLOCK = 64
SUB_PER_SEL = SEL_BLOCK // CMP_STRIDE
SEL_TOP = 16
WINDOW = 512
D_C = D_MODEL
C_CONV = 3
D_FF = -(-8 * D_MODEL // (3 * 256)) * 256
D_IN_MAIN = 2 * D_A + D_B + 6 * D_KV
N_GATES = 3 * B_HEADS
RMS_EPS = 1e-6
NEG_INF = -1e30
FORCE_SCORE = 1e4
QK_SCALE = HEAD_DIM ** -0.5

COL_XA = 0
COL_GA = D_A // 128
COL_Q = 2 * D_A // 128
COL_KVC = (2 * D_A + D_B) // 128
COL_KVS = COL_KVC + 2 * D_KV // 128
COL_KVW = COL_KVS + 2 * D_KV // 128

V7X_VMEM_LIMIT_BYTES = 56 * 1024 * 1024
LANE = 128
ATT_TQ = 256
ATT_CK = 512
F32 = jnp.float32
BF16 = jnp.bfloat16


def _row_tile(m, cap=1024):
    for t in range(cap - cap % 16, 15, -16):
        if m % t == 0:
            return t
    raise ValueError(m)


def _col_tile(n, cap=512):
    for t in range(cap, LANE - 1, -LANE):
        if n % t == 0:
            return t
    raise ValueError(n)


def _round_up(x, m):
    return -(-x // m) * m


def _params(sem):
    return pltpu.CompilerParams(dimension_semantics=sem, vmem_limit_bytes=V7X_VMEM_LIMIT_BYTES)


def _dot(a, b):
    return jnp.dot(a, b, preferred_element_type=F32)


def _dot_nt(a, b):
    return lax.dot_general(a, b, (((1,), (1,)), ((), ())), preferred_element_type=F32)


def _dot_f32_exact(x, w_bf16):
    hi = x.astype(BF16)
    r1 = x - hi.astype(F32)
    mid = r1.astype(BF16)
    lo = (r1 - mid.astype(F32)).astype(BF16)
    return _dot(hi, w_bf16) + (_dot(mid, w_bf16) + _dot(lo, w_bf16))


def _gelu_tanh(x):
    return 0.5 * x * (1.0 + jnp.tanh(0.7978845608028654 * (x + 0.044715 * (x * x * x))))


def _lanes(x128, w):
    return x128 if w == LANE else jnp.concatenate([x128] * (w // LANE), axis=1)


def _rmsnorm_kernel(x_ref, g_ref, o_ref):
    x = x_ref[...]
    y = x * lax.rsqrt(jnp.mean(x * x, axis=-1, keepdims=True) + RMS_EPS)
    o_ref[...] = (y * g_ref[...]).astype(o_ref.dtype)


def _rmsnorm(x, g, out_dtype):
    m, d = x.shape
    tm = _row_tile(m)
    return pl.pallas_call(
        _rmsnorm_kernel,
        out_shape=jax.ShapeDtypeStruct((m, d), out_dtype),
        grid=(m // tm,),
        in_specs=[pl.BlockSpec((tm, d), lambda i: (i, 0)), pl.BlockSpec((1, d), lambda i: (0, 0))],
        out_specs=pl.BlockSpec((tm, d), lambda i: (i, 0)),
        compiler_params=_params(("parallel",)),
        name="rmsnorm",
    )(x, g.reshape(1, d))


def _wspec(layer, k, tn):
    return pl.BlockSpec((None, k, tn), lambda i, j: (layer, 0, j))


def _mm_kernel(x_ref, w_ref, o_ref):
    o_ref[...] = _dot(x_ref[...], w_ref[...].astype(BF16)).astype(o_ref.dtype)


def _matmul(x, w, layer, out_dtype=F32, name="matmul", row_cap=1024, col_cap=512):
    m, k = x.shape
    n = w.shape[2]
    tm, tn = _row_tile(m, row_cap), _col_tile(n, col_cap)
    return pl.pallas_call(
        _mm_kernel,
        out_shape=jax.ShapeDtypeStruct((m, n), out_dtype),
        grid=(m // tm, n // tn),
        in_specs=[pl.BlockSpec((tm, k), lambda i, j: (i, 0)), _wspec(layer, k, tn)],
        out_specs=pl.BlockSpec((tm, tn), lambda i, j: (i, j)),
        compiler_params=_params(("parallel", "arbitrary")),
        name=name,
    )(x, w)


def _in_proj_kernel(x_ref, w_ref, z_ref, kv_ref, *, j_kv0, tiles_per_seg):
    j = pl.program_id(1)
    tm, tn = z_ref.shape
    heads_per_tile = tn // HEAD_DIM
    rows_per_token = 2 * KV_HEADS
    acc = _dot(x_ref[...], w_ref[...].astype(BF16))
    z_ref[...] = acc
    for part in range(tiles_per_seg):
        @pl.when((j >= j_kv0) & ((j - j_kv0) % tiles_per_seg == part))
        def _():
            for c in range(heads_per_tile):
                kv_ref[pl.ds(part * heads_per_tile + c, tm, stride=rows_per_token), :] = (
                    acc[:, c * HEAD_DIM:(c + 1) * HEAD_DIM])


def _in_proj(x, w, layer):
    m, k = x.shape
    n = D_IN_MAIN
    tm, tn = _row_tile(m, WIDE_ROW_CAP), _col_tile(n)
    seg = 2 * D_KV
    assert seg % tn == 0 and (COL_KVC * LANE) % tn == 0 and n == COL_KVC * LANE + 3 * seg and w.shape[2] >= n
    tiles_per_seg = seg // tn
    j_kv0 = COL_KVC * LANE // tn
    rpt = 2 * KV_HEADS
    return pl.pallas_call(
        functools.partial(_in_proj_kernel, j_kv0=j_kv0, tiles_per_seg=tiles_per_seg),
        out_shape=(jax.ShapeDtypeStruct((m, n), F32), jax.ShapeDtypeStruct((3, m * rpt, HEAD_DIM), F32)),
        grid=(m // tm, n // tn),
        in_specs=[pl.BlockSpec((tm, k), lambda i, j: (i, 0)), _wspec(layer, k, tn)],
        out_specs=(pl.BlockSpec((tm, tn), lambda i, j: (i, j)),
                   pl.BlockSpec((None, tm * rpt, HEAD_DIM),
                                lambda i, j: (jnp.maximum(j - j_kv0, 0) // tiles_per_seg, i, 0))),
        compiler_params=_params(("parallel", "arbitrary")),
        name="ab_in_proj",
    )(x, w)


def _mm_res_kernel(x_ref, w_ref, r_ref, o_ref):
    o_ref[...] = r_ref[...] + _dot(x_ref[...], w_ref[...].astype(BF16))


def _matmul_residual(x, w, layer, r, name="matmul_res", row_cap=1024, col_cap=512):
    m, k = x.shape
    n = w.shape[2]
    tm, tn = _row_tile(m, row_cap), _col_tile(n, col_cap)
    return pl.pallas_call(
        _mm_res_kernel,
        out_shape=jax.ShapeDtypeStruct((m, n), F32),
        grid=(m // tm, n // tn),
        in_specs=[pl.BlockSpec((tm, k), lambda i, j: (i, 0)), _wspec(layer, k, tn),
                  pl.BlockSpec((tm, tn), lambda i, j: (i, j))],
        out_specs=pl.BlockSpec((tm, tn), lambda i, j: (i, j)),
        compiler_params=_params(("parallel", "arbitrary")),
        name=name,
    )(x, w, r)


def _mm2_res_kernel(x1_ref, x2_ref, w1_ref, w2_ref, r_ref, o_ref):
    o_ref[...] = r_ref[...] + (_dot(x1_ref[...], w1_ref[...].astype(BF16))
                               + _dot(x2_ref[...], w2_ref[...].astype(BF16)))


def _matmul2_residual(x1, x2, w, layer, r, name="matmul2_res"):
    m, k1 = x1.shape
    k2 = x2.shape[1]
    n = w.shape[2]
    tm, tn = _row_tile(m, WIDE_ROW_CAP), _col_tile(n)
    assert k1 % 8 == 0 and k1 == k2
    return pl.pallas_call(
        _mm2_res_kernel,
        out_shape=jax.ShapeDtypeStruct((m, n), F32),
        grid=(m // tm, n // tn),
        in_specs=[pl.BlockSpec((tm, k1), lambda i, j: (i, 0)), pl.BlockSpec((tm, k2), lambda i, j: (i, 0)),
                  pl.BlockSpec((None, k1, tn), lambda i, j: (layer, 0, j)),
                  pl.BlockSpec((None, k2, tn), lambda i, j: (layer, 1, j)),
                  pl.BlockSpec((tm, tn), lambda i, j: (i, j))],
        out_specs=pl.BlockSpec((tm, tn), lambda i, j: (i, j)),
        compiler_params=_params(("parallel", "arbitrary")),
        name=name,
    )(x1, x2, w, w, r)


def _ffn_up_kernel(x_ref, wg_ref, wu_ref, o_ref):
    x = x_ref[...]
    g = _dot(x, wg_ref[...].astype(BF16))
    u = _dot(x, wu_ref[...].astype(BF16))
    o_ref[...] = (g * jax.nn.sigmoid(g) * u).astype(o_ref.dtype)


WIDE_ROW_CAP = 1536
F32_WEIGHT_COL_CAP = 256


def _ffn_up(x, wg, wu, layer):
    m, k = x.shape
    n = wg.shape[2]
    tm, tn = _row_tile(m, WIDE_ROW_CAP), _col_tile(n, F32_WEIGHT_COL_CAP)
    return pl.pallas_call(
        _ffn_up_kernel,
        out_shape=jax.ShapeDtypeStruct((m, n), BF16),
        grid=(m // tm, n // tn),
        in_specs=[pl.BlockSpec((tm, k), lambda i, j: (i, 0)), _wspec(layer, k, tn), _wspec(layer, k, tn)],
        out_specs=pl.BlockSpec((tm, tn), lambda i, j: (i, j)),
        compiler_params=_params(("parallel", "arbitrary")),
        name="ffn_up",
    )(x, wg, wu)


FFN_DOWN_COL_CAP = 256


def _lru_kernel(xa_ref, ga_ref, h0_ref, buf_ref, cw_ref, cb_ref, wa_ref, ba_ref, wx_ref, bx_ref, lam_ref,
                y_ref, ht_ref, nbuf_ref):
    t_len = xa_ref.shape[0]
    x = xa_ref[...]
    buf = buf_ref[...]
    cw = cw_ref[...]
    row = lax.broadcasted_iota(jnp.int32, x.shape, 0)
    nb = A_CONV - 1
    xc = x * cw[nb:nb + 1]
    for s in range(1, A_CONV):
        xs = pltpu.roll(x, s, 0)
        for r in range(s):
            xs = jnp.where(row == r, buf[nb + r - s:nb + r - s + 1], xs)
        xc = xc + xs * cw[nb - s:nb - s + 1]
    xc = xc + cb_ref[...]
    xcb = xc.astype(BF16)
    r_g = jax.nn.sigmoid(_dot(xcb, wa_ref[...].astype(BF16)) + ba_ref[...])
    i_g = jax.nn.sigmoid(_dot(xcb, wx_ref[...].astype(BF16)) + bx_ref[...])
    nl = -lam_ref[...]
    softplus = jnp.maximum(nl, 0.0) + jnp.log1p(jnp.exp(-jnp.abs(nl)))
    log_a = -LRU_C * r_g * softplus
    a = jnp.exp(log_a)
    u = jnp.sqrt(-jnp.tanh(log_a) * (a * a + 1.0)) * i_g * xc
    d = 1
    while d < t_len:
        if d % 8:
            keep = row >= d
            a_prev = pltpu.roll(a, d, 0)
            u_prev = pltpu.roll(u, d, 0)
            u = jnp.where(keep, a * u_prev + u, u)
            a = jnp.where(keep, a * a_prev, a)
        else:
            a_hi = a[d:]
            u = jnp.concatenate([u[:d], a_hi * u[:t_len - d] + u[d:]], axis=0)
            a = jnp.concatenate([a[:d], a_hi * a[:t_len - d]], axis=0)
        d *= 2
    h = a * h0_ref[...] + u
    y_ref[...] = (h * _gelu_tanh(ga_ref[...])).astype(y_ref.dtype)
    ht_ref[...] = h[t_len - 1:t_len]
    nbuf_ref[...] = x[t_len - nb:t_len]


def _into_zeros(kernel_fn, n_in, out_rows, width, dtype):
    def wrapped(*refs):
        return kernel_fn(*refs[:n_in], *refs[n_in + 1:])
    return wrapped, pl.BlockSpec(memory_space=pl.ANY), jnp.zeros((out_rows, width), dtype), {n_in: 0}


def _lru_mixer(z, row0, n_seq, t_len, h0, buf, conv_w, conv_b, wa, ba, wx, bx, lam, out_dtype, out_rows=None):
    assert row0 % t_len == 0 and t_len >= A_CONV - 1 and t_len % 8 == 0
    rb0 = row0 // t_len
    nb = A_CONV - 1
    out_rows = out_rows or n_seq * t_len
    vec = lambda v: v.reshape(1, D_A)
    cspec = pl.BlockSpec((1, A_BLK), lambda s, h: (0, h))
    wspec = pl.BlockSpec((None, A_BLK, A_BLK), lambda s, h: (h, 0, 0))
    body, base_spec, base, alias = _into_zeros(_lru_kernel, 11, out_rows, D_A, out_dtype)
    return pl.pallas_call(
        body,
        out_shape=(jax.ShapeDtypeStruct((out_rows, D_A), out_dtype),
                   jax.ShapeDtypeStruct((n_seq, 1, D_A), F32),
                   jax.ShapeDtypeStruct((n_seq, nb, D_A), F32)),
        grid=(n_seq, A_HEADS),
        in_specs=[pl.BlockSpec((t_len, A_BLK), lambda s, h: (rb0 + s, COL_XA + h)),
                  pl.BlockSpec((t_len, A_BLK), lambda s, h: (rb0 + s, COL_GA + h)),
                  pl.BlockSpec((None, 1, A_BLK), lambda s, h: (s, 0, h)),
                  pl.BlockSpec((None, nb, A_BLK), lambda s, h: (s, 0, h)),
                  pl.BlockSpec((A_CONV, A_BLK), lambda s, h: (0, h)),
                  cspec, wspec, cspec, wspec, cspec, cspec, base_spec],
        out_specs=(pl.BlockSpec((t_len, A_BLK), lambda s, h: (s, h)),
                   pl.BlockSpec((None, 1, A_BLK), lambda s, h: (s, 0, h)),
                   pl.BlockSpec((None, nb, A_BLK), lambda s, h: (s, 0, h))),
        input_output_aliases=alias,
        compiler_params=_params(("parallel", "parallel")),
        name="rglru_mixer",
    )(z, z, h0.reshape(n_seq, 1, D_A), buf, conv_w, vec(conv_b), wa, vec(ba), wx, vec(bx), vec(lam), base)


def _compress_kernel(x_ref, pe_ref, w1_ref, b1_ref, w2_ref, o_ref):
    n_sub = o_ref.shape[0]
    pe = pe_ref[...]
    y0 = jnp.zeros((n_sub, HEAD_DIM), F32)
    y1 = jnp.zeros((n_sub, HEAD_DIM), F32)
    for j in range(CMP_STRIDE):
        xj = x_ref[pl.ds(j, n_sub, stride=CMP_STRIDE), :]
        y0 = y0 + _dot((xj + pe[j:j + 1]).astype(BF16), w1_ref[j])
        y1 = y1 + _dot((xj + pe[CMP_STRIDE + j:CMP_STRIDE + j + 1]).astype(BF16), w1_ref[CMP_STRIDE + j])
    hid = y0 + pltpu.roll(y1, n_sub - 1, 0)
    hid = _gelu_tanh(hid + b1_ref[...])
    o_ref[...] = _dot(hid.astype(BF16), w2_ref[...]).astype(o_ref.dtype)


def _compress(x2d, x_index, n_seq, t_len, pe, w1, b1, w2):
    assert SUB_PER_CMP == 2 and t_len % (8 * CMP_STRIDE) == 0
    n_sub = t_len // CMP_STRIDE
    nt = 2 * KV_HEADS
    return pl.pallas_call(
        _compress_kernel,
        out_shape=jax.ShapeDtypeStruct((n_seq, nt, n_sub, HEAD_DIM), BF16),
        grid=(n_seq, nt),
        in_specs=[pl.BlockSpec((t_len, HEAD_DIM), x_index),
                  pl.BlockSpec((None, CMP_LEN, HEAD_DIM), lambda s, c: (c // KV_HEADS, 0, 0)),
                  pl.BlockSpec((None, CMP_LEN, HEAD_DIM, HEAD_DIM), lambda s, c: (c // KV_HEADS, 0, 0, 0)),
                  pl.BlockSpec((None, 1, HEAD_DIM), lambda s, c: (c // KV_HEADS, 0, 0)),
                  pl.BlockSpec((None, HEAD_DIM, HEAD_DIM), lambda s, c: (c // KV_HEADS, 0, 0))],
        out_specs=pl.BlockSpec((None, None, n_sub, HEAD_DIM), lambda s, c: (s, c, 0, 0)),
        compiler_params=_params(("parallel", "arbitrary")),
        name="nsa_compress",
    )(x2d, pe, w1.astype(BF16), b1.reshape(2, 1, HEAD_DIM), w2.astype(BF16))


def _head_rows(ref, c, n_tok):
    return ref[pl.ds(c, n_tok, stride=2 * KV_HEADS), :]


def _page_gather_kernel(pt_ref, *refs):
    page_refs, o_ref = refs[:-1], refs[-1]
    for k, c_ref in enumerate(page_refs):
        for c in range(2 * KV_HEADS):
            o_ref[c, k * PAGE_SIZE:(k + 1) * PAGE_SIZE, :] = _head_rows(c_ref, c, PAGE_SIZE)


def _pages_per_step(n_pages, want=4):
    pps = want
    while n_pages % pps:
        pps //= 2
    return pps


def _page_gather(cache_rows, layer, page_table):
    nb, n_pages = page_table.shape
    nt = 2 * KV_HEADS
    pps = _pages_per_step(n_pages, 16)

    def page_spec(k):
        return pl.BlockSpec((None, None, PAGE_SIZE * nt, HEAD_DIM), lambda b, p, pt: (layer, pt[b, p * pps + k], 0, 0))

    return pl.pallas_call(
        _page_gather_kernel,
        out_shape=jax.ShapeDtypeStruct((nb, nt, n_pages * PAGE_SIZE, HEAD_DIM), cache_rows.dtype),
        grid_spec=pltpu.PrefetchScalarGridSpec(
            num_scalar_prefetch=1, grid=(nb, n_pages // pps),
            in_specs=[page_spec(k) for k in range(pps)],
            out_specs=pl.BlockSpec((None, nt, pps * PAGE_SIZE, HEAD_DIM), lambda b, p, pt: (b, 0, p, 0))),
        compiler_params=_params(("parallel", "arbitrary")),
        name="page_gather",
    )(page_table, *([cache_rows] * pps))


def _stack_heads(q, g0):
    return jnp.concatenate([q[:, (g0 + p) * HEAD_DIM:(g0 + p + 1) * HEAD_DIM] for p in range(HPG)], axis=0).astype(BF16)


def _masked_softmax(s, ok):
    sb = jnp.where(ok, s, NEG_INF)
    m = jnp.max(sb, axis=-1, keepdims=True)
    e = jnp.exp(sb - m)
    return e / jnp.sum(e, axis=-1, keepdims=True)


def _cmp_branch(q4, kc, vc, t_rows, slope, n_cmp):
    ncp = kc.shape[0]
    s = _dot_nt(q4, kc) * QK_SCALE
    col = lax.broadcasted_iota(jnp.int32, s.shape, 1)
    dist_i = t_rows - (col * CMP_STRIDE + (CMP_LEN - 1))
    ok = (dist_i >= 0) & (col < n_cmp)
    p = _masked_softmax(s - _lanes(slope, ncp) * dist_i.astype(F32), ok)
    p = jnp.where(ok, p, 0.0)
    return _dot(p.astype(BF16), vc), p


def _select_blocks(imp, pool, t_q, n_sel):
    imp_sel = _dot_f32_exact(imp, pool)
    jcol = lax.broadcasted_iota(jnp.int32, imp_sel.shape, 1)
    cur = lax.shift_right_logical(t_q, int(np.log2(SEL_BLOCK)))
    forced = (jcol == 0) | (jcol == cur) | (jcol == cur - 1)
    valid = jcol * SEL_BLOCK <= t_q
    score = jnp.where(forced, FORCE_SCORE, jnp.where(valid, imp_sel, -1.0))
    score = jnp.where(jcol < n_sel, score, -2.0)
    rank = jnp.zeros(score.shape, F32)
    for i in range(n_sel):
        ci = score[:, i:i + 1]
        before = (ci > score) | ((ci == score) & (jcol > i))
        rank = rank + jnp.where(before, 1.0, 0.0)
    top = min(SEL_TOP, n_sel)
    return jnp.where(rank < float(top), 1.0, 0.0)


def _pool_matrix(n_cmp, ncp, n_sel, nsp):
    assert n_cmp + SUB_PER_CMP - 1 <= n_sel * SUB_PER_SEL
    m = np.zeros((ncp, nsp), np.float32)
    for c in range(n_cmp):
        for n in range(SUB_PER_CMP):
            m[c, (c + n) // SUB_PER_SEL] += 1.0 / SUB_PER_CMP
    return jnp.asarray(m, BF16)


def _slope_rows(t_rep):
    h = np.arange(1, B_HEADS + 1, dtype=np.float64)
    s = np.exp2(-8.0 * h / B_HEADS).astype(np.float32).reshape(KV_HEADS, HPG, 1, 1)
    return jnp.asarray(np.broadcast_to(s, (KV_HEADS, HPG, t_rep, LANE)).reshape(KV_HEADS, HPG * t_rep, LANE))


def _expand_matrix(shape, blk0):
    j = lax.broadcasted_iota(jnp.int32, shape, 0)
    k = lax.broadcasted_iota(jnp.int32, shape, 1)
    return jnp.where(j == blk0 + lax.shift_right_logical(k, int(np.log2(SEL_BLOCK))), 1.0, 0.0).astype(BF16)


def _gate_combine(gs, p, tq, o_c, o_s, o_w):
    rows = slice(p * tq, (p + 1) * tq)
    return (gs[:, p:p + 1] * o_c[rows] + gs[:, HPG + p:HPG + p + 1] * o_s[rows]
            + gs[:, 2 * HPG + p:2 * HPG + p + 1] * o_w[rows])


def _attn_prompt_kernel(q_ref, kc_ref, vc_ref, ks_ref, vs_ref, kw_ref, vw_ref, gt_ref, slope_ref, pool_ref,
                        o_ref, ksb, vsb, kwb, vwb, *, n_cmp, n_sel):
    qi = pl.program_id(2)
    tq = q_ref.shape[0]
    t_total = ks_ref.shape[0]
    rr = HPG * tq

    @pl.when(qi == 0)
    def _():
        ksb[...] = ks_ref[...].astype(BF16)
        vsb[...] = vs_ref[...].astype(BF16)
        kwb[...] = kw_ref[...].astype(BF16)
        vwb[...] = vw_ref[...].astype(BF16)

    t0 = qi * tq
    q4 = _stack_heads(q_ref[...], 0)
    slope = slope_ref[...]

    def t_rows(width):
        r = lax.broadcasted_iota(jnp.int32, (rr, width), 0)
        return t0 + (r & (tq - 1))

    ncp = kc_ref.shape[0]
    o_c, p_c = _cmp_branch(q4, kc_ref[...], vc_ref[...], t_rows(ncp), slope, n_cmp)
    imp = p_c[0:tq]
    for p in range(1, HPG):
        imp = imp + p_c[p * tq:(p + 1) * tq]
    nsp = pool_ref.shape[1]
    t_q = t0 + lax.broadcasted_iota(jnp.int32, (tq, nsp), 0)
    sel = _select_blocks(imp, pool_ref[...], t_q, n_sel).astype(BF16)
    sel4 = jnp.concatenate([sel] * HPG, axis=0)

    ck = ATT_CK
    tr_ck = t_rows(ck)
    slope_ck = _lanes(slope, ck)
    col_ck = lax.broadcasted_iota(jnp.int32, (rr, ck), 1)

    def sel_chunk(c, carry):
        m, l, acc = carry
        k0 = pl.multiple_of(c * ck, ck)
        s = _dot_nt(q4, ksb[pl.ds(k0, ck), :]) * QK_SCALE
        dist_i = tr_ck - (k0 + col_ck)
        selx = _dot(sel4, _expand_matrix((nsp, ck), c * (ck // SEL_BLOCK)))
        ok = (dist_i >= 0) & (selx > 0.5)
        sb = jnp.where(ok, s - slope_ck * dist_i.astype(F32), NEG_INF)
        m_new = jnp.maximum(m, jnp.max(sb, axis=-1, keepdims=True))
        alpha = jnp.exp(m - m_new)
        e = jnp.exp(sb - m_new)
        l = alpha * l + jnp.sum(e, axis=-1, keepdims=True)
        acc = alpha * acc + _dot(e.astype(BF16), vsb[pl.ds(k0, ck), :])
        return m_new, l, acc

    n_chunks = (t0 + tq + ck - 1) // ck
    m0 = jnp.full((rr, 1), NEG_INF, F32)
    _, l_s, acc_s = lax.fori_loop(0, n_chunks, sel_chunk, (m0, jnp.zeros((rr, 1), F32), jnp.zeros((rr, HEAD_DIM), F32)))
    o_s = acc_s / l_s

    wk = WINDOW + tq
    w0 = pl.multiple_of(jnp.maximum(t0 - WINDOW, 0), tq)
    s = _dot_nt(q4, kwb[pl.ds(w0, wk), :]) * QK_SCALE
    dist_i = t_rows(wk) - (w0 + lax.broadcasted_iota(jnp.int32, (rr, wk), 1))
    ok = (dist_i >= 0) & (dist_i < WINDOW)
    p_w = _masked_softmax(s - _lanes(slope, wk) * dist_i.astype(F32), ok)
    o_w = _dot(p_w.astype(BF16), vwb[pl.ds(w0, wk), :])

    gs = jax.nn.sigmoid(gt_ref[...])
    for p in range(HPG):
        o_ref[:, p * HEAD_DIM:(p + 1) * HEAD_DIM] = _gate_combine(gs, p, tq, o_c, o_s, o_w).astype(o_ref.dtype)


def _attn_prompt(z, gt, kvc_cmp, n_seq, t_len, out_rows=None):
    tq = ATT_TQ
    assert t_len % ATT_CK == 0 and t_len >= WINDOW + tq and tq & (tq - 1) == 0
    n_cmp = (t_len - CMP_LEN) // CMP_STRIDE + 1
    n_sel = -(-t_len // SEL_BLOCK)
    ncp = kvc_cmp.shape[2]
    nsp = _round_up(n_sel, LANE)
    nq = t_len // tq
    rr = HPG * tq
    kvspec = lambda col0: pl.BlockSpec((t_len, HEAD_DIM), lambda b, g, i: (b, col0 + g))
    out_rows = out_rows or n_seq * t_len
    body, base_spec, base, alias = _into_zeros(functools.partial(_attn_prompt_kernel, n_cmp=n_cmp, n_sel=n_sel),
                                               10, out_rows, D_B, BF16)
    return pl.pallas_call(
        body,
        out_shape=jax.ShapeDtypeStruct((out_rows, D_B), BF16),
        grid=(n_seq, KV_HEADS, nq),
        in_specs=[pl.BlockSpec((tq, HPG * HEAD_DIM), lambda b, g, i: (b * nq + i, COL_Q // HPG + g)),
                  pl.BlockSpec((None, None, ncp, HEAD_DIM), lambda b, g, i: (b, g, 0, 0)),
                  pl.BlockSpec((None, None, ncp, HEAD_DIM), lambda b, g, i: (b, KV_HEADS + g, 0, 0)),
                  kvspec(COL_KVS), kvspec(COL_KVS + KV_HEADS), kvspec(COL_KVW), kvspec(COL_KVW + KV_HEADS),
                  pl.BlockSpec((tq, LANE), lambda b, g, i: (b * nq + i, g)),
                  pl.BlockSpec((None, rr, LANE), lambda b, g, i: (g, 0, 0)),
                  pl.BlockSpec((ncp, nsp), lambda b, g, i: (0, 0)), base_spec],
        out_specs=pl.BlockSpec((tq, HPG * HEAD_DIM), lambda b, g, i: (b * nq + i, g)),
        scratch_shapes=[pltpu.VMEM((t_len, HEAD_DIM), BF16)] * 4,
        input_output_aliases=alias,
        compiler_params=_params(("parallel", "parallel", "arbitrary")),
        name="nsa_prompt_attention",
    )(z, kvc_cmp, kvc_cmp, z, z, z, z, gt, _slope_rows(tq), _pool_matrix(n_cmp, ncp, n_sel, nsp), base)


def _attn_dec_select_kernel(q_ref, kc_ref, vc_ref, slope_ref, pool_ref, oc_ref, sel_ref, *, n_cmp, n_sel, past_len):
    ts = q_ref.shape[0]
    rr = HPG * ts
    ncp = kc_ref.shape[0]
    q4 = _stack_heads(q_ref[...], 0)
    t_rows = past_len + (lax.broadcasted_iota(jnp.int32, (rr, ncp), 0) & (ts - 1))
    o_c, p_c = _cmp_branch(q4, kc_ref[...], vc_ref[...], t_rows, slope_ref[...], n_cmp)
    imp = p_c[0:ts]
    for p in range(1, HPG):
        imp = imp + p_c[p * ts:(p + 1) * ts]
    nsp = pool_ref.shape[1]
    t_q = past_len + lax.broadcasted_iota(jnp.int32, (ts, nsp), 0)
    oc_ref[...] = o_c
    sel_ref[...] = _select_blocks(imp, pool_ref[...], t_q, n_sel)


def _attn_dec_select(z, row0, kvc_cmp, n_seq, ts, past_len, n_cmp, n_sel):
    assert row0 % ts == 0 and ts & (ts - 1) == 0 and ts % 8 == 0
    rb0 = row0 // ts
    ncp = kvc_cmp.shape[2]
    nsp = _round_up(n_sel, LANE)
    rr = HPG * ts
    return pl.pallas_call(
        functools.partial(_attn_dec_select_kernel, n_cmp=n_cmp, n_sel=n_sel, past_len=past_len),
        out_shape=(jax.ShapeDtypeStruct((n_seq, KV_HEADS, rr, HEAD_DIM), F32),
                   jax.ShapeDtypeStruct((n_seq, KV_HEADS, ts, nsp), F32)),
        grid=(n_seq, KV_HEADS),
        in_specs=[pl.BlockSpec((ts, HPG * HEAD_DIM), lambda b, g: (rb0 + b, COL_Q // HPG + g)),
                  pl.BlockSpec((None, None, ncp, HEAD_DIM), lambda b, g: (b, g, 0, 0)),
                  pl.BlockSpec((None, None, ncp, HEAD_DIM), lambda b, g: (b, KV_HEADS + g, 0, 0)),
                  pl.BlockSpec((None, rr, LANE), lambda b, g: (g, 0, 0)),
                  pl.BlockSpec((ncp, nsp), lambda b, g: (0, 0))],
        out_specs=(pl.BlockSpec((None, None, rr, HEAD_DIM), lambda b, g: (b, g, 0, 0)),
                   pl.BlockSpec((None, None, ts, nsp), lambda b, g: (b, g, 0, 0))),
        compiler_params=_params(("parallel", "parallel")),
        name="nsa_decode_select",
    )(z, kvc_cmp, kvc_cmp, _slope_rows(ts), _pool_matrix(n_cmp, ncp, n_sel, nsp))


def _attn_dec_kernel(pt_ref, q_ref, sel_ref, *rest, past_len, n_steps, pages_per_step):
    page_refs = rest[:pages_per_step]
    kvs_ref, kvw_ref, win_ref, oc_ref, gt_ref, slope_ref, o_ref, nwin_ref, m_sc, l_sc, acc_sc = rest[pages_per_step:]
    pg = pl.program_id(1)
    ts = q_ref.shape[0]
    rr = HPG * ts
    nsp = sel_ref.shape[-1]
    pad_rows = LANE - ts
    row = lax.broadcasted_iota(jnp.int32, (rr, LANE), 0)
    col = lax.broadcasted_iota(jnp.int32, (rr, LANE), 1)
    t_rows = past_len + (row & (ts - 1))

    @pl.when(pg == 0)
    def _():
        m_sc[...] = jnp.full(m_sc.shape, NEG_INF, F32)
        l_sc[...] = jnp.zeros(l_sc.shape, F32)
        acc_sc[...] = jnp.zeros(acc_sc.shape, F32)

    def sel_update(g, q4, kk, vv, dist_i, expand, extra_ok):
        width = kk.shape[0]
        s = _dot_nt(q4, kk) * QK_SCALE
        sel4 = jnp.concatenate([sel_ref[g].astype(BF16)] * HPG, axis=0)
        ok = (dist_i >= 0) & (_dot(sel4, expand) > 0.5) & extra_ok
        sb = jnp.where(ok, s - _lanes(slope_ref[g], width) * dist_i.astype(F32), NEG_INF)
        m_old = m_sc[g]
        m_new = jnp.maximum(m_old, jnp.max(sb, axis=-1, keepdims=True))
        alpha = jnp.exp(m_old - m_new)
        e = jnp.exp(sb - m_new)
        l_sc[g] = alpha * l_sc[g] + jnp.sum(e, axis=-1, keepdims=True)
        acc_sc[g] = alpha * acc_sc[g] + _dot(e.astype(BF16), vv)
        m_sc[g] = m_new

    @pl.when(pg < n_steps)
    def _():
        width = pages_per_step * PAGE_SIZE
        k0 = pg * width
        expand = _expand_matrix((nsp, width), pg * (width // SEL_BLOCK))
        wrow_ = lax.broadcasted_iota(jnp.int32, (rr, width), 0)
        wcol_ = lax.broadcasted_iota(jnp.int32, (rr, width), 1)
        dist_i = (past_len + (wrow_ & (ts - 1))) - (k0 + wcol_)
        for g in range(KV_HEADS):
            q4 = _stack_heads(q_ref[...], g * HPG)
            kk = jnp.concatenate([_head_rows(r, g, PAGE_SIZE) for r in page_refs], axis=0).astype(BF16)
            vv = jnp.concatenate([_head_rows(r, KV_HEADS + g, PAGE_SIZE) for r in page_refs], axis=0).astype(BF16)
            sel_update(g, q4, kk, vv, dist_i, expand, True)

    @pl.when(pg == n_steps)
    def _():
        zpad = jnp.zeros((pad_rows, HEAD_DIM), F32)
        gs = jax.nn.sigmoid(gt_ref[...])
        jj = lax.broadcasted_iota(jnp.int32, (nsp, LANE), 0)
        kk_i = lax.broadcasted_iota(jnp.int32, (nsp, LANE), 1)
        new_blk = lax.shift_right_logical(past_len + kk_i, int(np.log2(SEL_BLOCK)))
        expand_new = jnp.where(jj == new_blk, 1.0, 0.0).astype(BF16)
        is_new = col < ts
        nt = 2 * KV_HEADS
        wlen = win_ref.shape[0] // nt
        wrow = lax.broadcasted_iota(jnp.int32, (rr, wlen), 0)
        wcol = lax.broadcasted_iota(jnp.int32, (rr, wlen), 1)
        dist_w = (past_len + (wrow & (ts - 1))) - (past_len - wlen + wcol)
        ok_w = (dist_w >= 0) & (dist_w < WINDOW)
        dist_n = t_rows - (past_len + col)
        ok_n = (dist_n >= 0) & (dist_n < WINDOW) & is_new
        for g in range(KV_HEADS):
            q4 = _stack_heads(q_ref[...], g * HPG)
            ksl = slice(g * HEAD_DIM, (g + 1) * HEAD_DIM)
            vsl = slice(D_KV + g * HEAD_DIM, D_KV + (g + 1) * HEAD_DIM)
            kn = jnp.concatenate([kvs_ref[:, ksl], zpad], axis=0).astype(BF16)
            vn = jnp.concatenate([kvs_ref[:, vsl], zpad], axis=0).astype(BF16)
            sel_update(g, q4, kn, vn, dist_n, expand_new, is_new)
            o_s = acc_sc[g] / l_sc[g]
            slope = slope_ref[g]
            s1 = _dot_nt(q4, _head_rows(win_ref, g, wlen).astype(BF16)) * QK_SCALE
            s1 = jnp.where(ok_w, s1 - _lanes(slope, wlen) * dist_w.astype(F32), NEG_INF)
            kwn = jnp.concatenate([kvw_ref[:, ksl], zpad], axis=0).astype(BF16)
            vwn = jnp.concatenate([kvw_ref[:, vsl], zpad], axis=0).astype(BF16)
            s2 = _dot_nt(q4, kwn) * QK_SCALE
            s2 = jnp.where(ok_n, s2 - slope * dist_n.astype(F32), NEG_INF)
            m = jnp.maximum(jnp.max(s1, axis=-1, keepdims=True), jnp.max(s2, axis=-1, keepdims=True))
            e1 = jnp.exp(s1 - m)
            e2 = jnp.exp(s2 - m)
            inv = 1.0 / (jnp.sum(e1, axis=-1, keepdims=True) + jnp.sum(e2, axis=-1, keepdims=True))
            o_w = (_dot((e1 * inv).astype(BF16), _head_rows(win_ref, KV_HEADS + g, wlen).astype(BF16))
                   + _dot((e2 * inv).astype(BF16), vwn))
            o_c = oc_ref[g]
            for p in range(HPG):
                c0 = (g * HPG + p) * HEAD_DIM
                o_ref[:, c0:c0 + HEAD_DIM] = _gate_combine(gs[:, g * LANE:(g + 1) * LANE], p, ts, o_c, o_s, o_w)
        keep = (wlen - ts) * nt
        nwin_ref[0:keep, :] = win_ref[ts * nt:wlen * nt, :]
        for c in range(nt):
            nwin_ref[pl.ds(keep + c, ts, stride=nt), :] = kvw_ref[:, c * HEAD_DIM:(c + 1) * HEAD_DIM]


def _attn_decode(z, gt, row0, n_seq, ts, page_table, cache_rows, layer, win_rows, o_c, sel, past_len):
    n_pages = page_table.shape[1]
    pps = _pages_per_step(n_pages, 8)
    n_steps = n_pages // pps
    rb0 = row0 // ts
    rr = HPG * ts
    nsp = sel.shape[-1]
    nt = 2 * KV_HEADS
    wrows = win_rows.shape[2]
    w = 2 * D_KV
    assert wrows == WINDOW * nt and past_len % SEL_BLOCK == 0 and PAGE_SIZE == LANE
    last = n_steps - 1

    def page_spec(k):
        return pl.BlockSpec((None, None, PAGE_SIZE * nt, HEAD_DIM),
                            lambda b, p, pt: (layer, pt[b, jnp.minimum(p, last) * pps + k], 0, 0))

    return pl.pallas_call(
        functools.partial(_attn_dec_kernel, past_len=past_len, n_steps=n_steps, pages_per_step=pps),
        out_shape=(jax.ShapeDtypeStruct((n_seq * ts, D_B), F32),
                   jax.ShapeDtypeStruct((n_seq, wrows, HEAD_DIM), F32)),
        grid_spec=pltpu.PrefetchScalarGridSpec(
            num_scalar_prefetch=1, grid=(n_seq, n_steps + 1),
            in_specs=[pl.BlockSpec((ts, D_B), lambda b, p, pt: (rb0 + b, COL_Q * LANE // D_B)),
                      pl.BlockSpec((None, KV_HEADS, ts, nsp), lambda b, p, pt: (b, 0, 0, 0))]
                     + [page_spec(k) for k in range(pps)]
                     + [pl.BlockSpec((ts, w), lambda b, p, pt: (rb0 + b, COL_KVS * LANE // w)),
                        pl.BlockSpec((ts, w), lambda b, p, pt: (rb0 + b, COL_KVW * LANE // w)),
                        pl.BlockSpec((None, None, wrows, HEAD_DIM), lambda b, p, pt: (layer, b, 0, 0)),
                        pl.BlockSpec((None, KV_HEADS, rr, HEAD_DIM), lambda b, p, pt: (b, 0, 0, 0)),
                        pl.BlockSpec((ts, KV_HEADS * LANE), lambda b, p, pt: (rb0 + b, 0)),
                        pl.BlockSpec((KV_HEADS, rr, LANE), lambda b, p, pt: (0, 0, 0))],
            out_specs=(pl.BlockSpec((ts, D_B), lambda b, p, pt: (b, 0)),
                       pl.BlockSpec((None, wrows, HEAD_DIM), lambda b, p, pt: (b, 0, 0))),
            scratch_shapes=[pltpu.VMEM((KV_HEADS, rr, 1), F32), pltpu.VMEM((KV_HEADS, rr, 1), F32),
                            pltpu.VMEM((KV_HEADS, rr, HEAD_DIM), F32)]),
        compiler_params=_params(("parallel", "arbitrary")),
        name="nsa_decode_attention",
    )(page_table, z, sel, *([cache_rows] * pps), z, z, win_rows, o_c, gt, _slope_rows(ts))


def _sconv_kernel(bg_ref, cg_ref, v_ref, buf_ref, cw_ref, y_ref, nbuf_ref):
    t_len = cg_ref.shape[0]
    nb = C_CONV - 1
    cv = cg_ref[...] * v_ref[...]
    buf = buf_ref[...]
    cw = cw_ref[...]
    row = lax.broadcasted_iota(jnp.int32, cv.shape, 0)
    y = cv * cw[nb:nb + 1]
    for s in range(1, C_CONV):
        xs = pltpu.roll(cv, s, 0)
        for r in range(s):
            xs = jnp.where(row == r, buf[nb + r - s:nb + r - s + 1], xs)
        y = y + xs * cw[nb - s:nb - s + 1]
    y_ref[...] = (bg_ref[...] * y).astype(y_ref.dtype)
    nbuf_ref[...] = cv[t_len - nb:t_len]


def _sconv_mixer(zc, row0, n_seq, t_len, buf, conv_w, out_dtype, out_rows=None):
    assert row0 % t_len == 0 and t_len >= C_CONV - 1 and t_len % 8 == 0
    rb0 = row0 // t_len
    nb = C_CONV - 1
    tc = 256 if t_len > 256 else 2048
    nc = D_C // tc
    zspec = lambda third: pl.BlockSpec((t_len, tc), lambda s, c: (rb0 + s, third * nc + c))
    out_rows = out_rows or n_seq * t_len
    body, base_spec, base, alias = _into_zeros(_sconv_kernel, 5, out_rows, D_C, out_dtype)
    return pl.pallas_call(
        body,
        out_shape=(jax.ShapeDtypeStruct((out_rows, D_C), out_dtype),
                   jax.ShapeDtypeStruct((n_seq, nb, D_C), F32)),
        grid=(n_seq, nc),
        in_specs=[zspec(0), zspec(1), zspec(2),
                  pl.BlockSpec((None, nb, tc), lambda s, c: (s, 0, c)),
                  pl.BlockSpec((C_CONV, tc), lambda s, c: (0, c)), base_spec],
        out_specs=(pl.BlockSpec((t_len, tc), lambda s, c: (s, c)),
                   pl.BlockSpec((None, nb, tc), lambda s, c: (s, 0, c))),
        input_output_aliases=alias,
        compiler_params=_params(("parallel", "parallel")),
        name="sconv_mixer",
    )(zc, zc, zc, buf, conv_w, base)


def _gate_weight(w_in):
    nl, k = w_in.shape[:2]
    wg = w_in[:, :, D_IN_MAIN:].reshape(nl, k, 3, KV_HEADS, HPG).transpose(0, 1, 3, 2, 4)
    wg = jnp.pad(wg.reshape(nl, k, KV_HEADS, 3 * HPG), ((0, 0), (0, 0), (0, 0), (0, LANE - 3 * HPG)))
    return wg.reshape(nl, k, KV_HEADS * LANE).astype(BF16)


def _ab_layer(h, hn, i, mp, bp, t_len, bs, ts, past_len, w, cache_kv_cmp, cache_kv_sel, state_win_kv, state_lru_h,
              state_lru_conv, page_table):
    z, kv_rows = _in_proj(hn, w["ab_w_in_bf16"], i)
    gt = _matmul(hn, w["ab_w_gate_bf16"], i, name="ab_gate_proj")
    nt = 2 * KV_HEADS
    rows_view = lambda a: a.reshape(a.shape[0], a.shape[1], -1, HEAD_DIM)
    lru_w = (w["ab_conv_w"][i], w["ab_conv_b"][i], w["ab_gate_a_w"][i], w["ab_gate_a_b"][i], w["ab_gate_x_w"][i],
             w["ab_gate_x_b"][i], w["ab_lru_lambda"][i])
    cmp_w = (w["ab_cmp_pe"][i], w["ab_cmp_w1"][i], w["ab_cmp_b1"][i], w["ab_cmp_w2"][i])
    n_pages = page_table.shape[1]
    tk = past_len + ts
    n_cmp_s = (tk - CMP_LEN) // CMP_STRIDE + 1
    n_sel_s = -(-tk // SEL_BLOCK)
    assert n_cmp_s == (past_len - CMP_LEN) // CMP_STRIDE + 1

    ya_p, hT_p, nbuf_p = _lru_mixer(z, 0, bp, t_len, jnp.zeros((bp, D_A), F32), jnp.zeros((bp, A_CONV - 1, D_A), F32),
                                    *lru_w, out_dtype=BF16, out_rows=mp + bs * ts)
    kvc_p = _compress(z, lambda s, c: (s, COL_KVC + c), bp, t_len, *cmp_w)
    ob_p = _attn_prompt(z, gt, kvc_p, bp, t_len, out_rows=mp + bs * ts)

    ya_s, hT_s, nbuf_s = _lru_mixer(z, mp, bs, ts, state_lru_h[i], state_lru_conv[i], *lru_w, out_dtype=F32)
    past_cmp = _page_gather(rows_view(cache_kv_cmp), i, page_table)
    kvc_s = _compress(past_cmp.reshape(bs * nt * past_len, HEAD_DIM), lambda s, c: (s * nt + c, 0), bs, past_len, *cmp_w)
    oc_s, sel_s = _attn_dec_select(z, mp, kvc_s, bs, ts, past_len, n_cmp_s, n_sel_s)
    ob_s, nwin_s = _attn_decode(z, gt, mp, bs, ts, page_table, rows_view(cache_kv_sel), i, rows_view(state_win_kv),
                                oc_s, sel_s, past_len)

    ya = lax.dynamic_update_slice(ya_p, ya_s.astype(BF16), (mp, 0))
    ob = lax.dynamic_update_slice(ob_p, ob_s.astype(BF16), (mp, 0))
    h = _matmul2_residual(ya, ob, w["ab_w_out_bf16"], i, h, name="ab_out_proj")

    kv5 = lambda a, nb_, tt: a.reshape(nb_, tt, 2, KV_HEADS, HEAD_DIM)
    wl = min(WINDOW, t_len)
    kvw_p = kv5(kv_rows[2, :mp * nt], bp, t_len)
    outs_p = (kv5(kv_rows[0, :mp * nt], bp, t_len), kv5(kv_rows[1, :mp * nt], bp, t_len),
              kvw_p[:, t_len - wl:], hT_p.reshape(bp, D_A), nbuf_p)
    outs_s = (kv5(kv_rows[0, mp * nt:], bs, ts), kv5(kv_rows[1, mp * nt:], bs, ts),
              kv5(nwin_s, bs, WINDOW), hT_s.reshape(bs, D_A), nbuf_s)
    return h, outs_p, outs_s


def _c_layer(h, hn, i, mp, bp, t_len, bs, ts, w, state_sconv):
    zc = _matmul(hn, w["c_w_in"], i, name="c_in_proj", row_cap=WIDE_ROW_CAP, col_cap=F32_WEIGHT_COL_CAP)
    y_p, nbuf_p = _sconv_mixer(zc, 0, bp, t_len, jnp.zeros((bp, C_CONV - 1, D_C), F32), w["c_conv_w"][i], BF16,
                               out_rows=mp + bs * ts)
    y_s, nbuf_s = _sconv_mixer(zc, mp, bs, ts, state_sconv[i], w["c_conv_w"][i], F32)
    y = lax.dynamic_update_slice(y_p, y_s.astype(BF16), (mp, 0))
    h = _matmul_residual(y, w["c_w_out_bf16"], i, h, name="c_out_proj", row_cap=WIDE_ROW_CAP)
    return h, nbuf_p, nbuf_s


def kernel(x_prompt, x_sample, cache_kv_cmp, cache_kv_sel, state_win_kv, state_lru_h, state_lru_conv, state_sconv,
           page_table, norm_mix, norm_ffn, norm_final, ab_w_in, ab_conv_w, ab_conv_b, ab_gate_a_w, ab_gate_a_b,
           ab_gate_x_w, ab_gate_x_b, ab_lru_lambda, ab_cmp_pe, ab_cmp_w1, ab_cmp_b1, ab_cmp_w2, ab_w_out,
           c_w_in, c_conv_w, c_w_out, ffn_w_gate, ffn_w_up, ffn_w_down):
    w = dict(ab_w_in_bf16=ab_w_in.astype(BF16), ab_w_gate_bf16=_gate_weight(ab_w_in), ab_conv_w=ab_conv_w,
             ab_conv_b=ab_conv_b, ab_gate_a_w=ab_gate_a_w, ab_gate_a_b=ab_gate_a_b, ab_gate_x_w=ab_gate_x_w,
             ab_gate_x_b=ab_gate_x_b, ab_lru_lambda=ab_lru_lambda, ab_cmp_pe=ab_cmp_pe, ab_cmp_w1=ab_cmp_w1,
             ab_cmp_b1=ab_cmp_b1, ab_cmp_w2=ab_cmp_w2, ab_w_out_bf16=ab_w_out.astype(BF16),
             c_w_in=c_w_in, c_conv_w=c_conv_w, c_w_out_bf16=c_w_out.astype(BF16))
    ffn_w_down_bf16 = ffn_w_down.astype(BF16)
    bp, t_len = x_prompt.shape[:2]
    bs, ts = x_sample.shape[:2]
    mp, ms = bp * t_len, bs * ts
    past_len = page_table.shape[1] * PAGE_SIZE
    h = jnp.concatenate([x_prompt.reshape(mp, D_MODEL), x_sample.reshape(ms, D_MODEL)], axis=0)
    p_ab, s_ab, p_c, s_c = [], [], [], []
    for layer in range(DEPTH):
        hn = _rmsnorm(h, norm_mix[layer], BF16)
        i = layer // 2
        if layer % 2 == 0:
            h, op, os_ = _ab_layer(h, hn, i, mp, bp, t_len, bs, ts, past_len, w, cache_kv_cmp, cache_kv_sel,
                                   state_win_kv, state_lru_h, state_lru_conv, page_table)
            p_ab.append(op)
            s_ab.append(os_)
        else:
            h, nbp, nbs = _c_layer(h, hn, i, mp, bp, t_len, bs, ts, w, state_sconv)
            p_c.append(nbp)
            s_c.append(nbs)
        hf = _rmsnorm(h, norm_ffn[layer], BF16)
        hid = _ffn_up(hf, ffn_w_gate, ffn_w_up, layer)
        h = _matmul_residual(hid, ffn_w_down_bf16, layer, h, name="ffn_down", col_cap=FFN_DOWN_COL_CAP)
    y = _rmsnorm(h, norm_final, F32)
    y_prompt = y[:mp].reshape(bp, t_len, D_MODEL)
    y_sample = y[mp:].reshape(bs, ts, D_MODEL)
    stack = lambda rows, j: jnp.stack([r[j] for r in rows])
    return (y_prompt, y_sample,
            stack(p_ab, 0), stack(p_ab, 1), stack(p_ab, 2), stack(p_ab, 3), stack(p_ab, 4), jnp.stack(p_c),
            stack(s_ab, 0), stack(s_ab, 1), stack(s_ab, 2), stack(s_ab, 3), stack(s_ab, 4), jnp.stack(s_c))
```

```python
import functools

import jax
import jax.numpy as jnp
import numpy as np
from jax import lax
from jax.experimental import pallas as pl
from jax.experimental.pallas import tpu as pltpu

D_MODEL = 4096
DEPTH = 4
PAGE_SIZE = 128
N_AB = (DEPTH + 1) // 2
N_C = DEPTH // 2
D_A = D_MODEL // 2
A_HEADS = 16
A_BLK = D_A // A_HEADS
A_CONV = 4
LRU_C = 8.0
B_HEADS = 16
HEAD_DIM = 128
KV_HEADS = 4
HPG = B_HEADS // KV_HEADS
D_B = B_HEADS * HEAD_DIM
D_KV = KV_HEADS * HEAD_DIM
CMP_LEN = 32
CMP_STRIDE = 16
SUB_PER_CMP = CMP_LEN // CMP_STRIDE
SEL_BLOCK = 64
SUB_PER_SEL = SEL_BLOCK // CMP_STRIDE
SEL_TOP = 16
WINDOW = 512
D_C = D_MODEL
C_CONV = 3
D_FF = -(-8 * D_MODEL // (3 * 256)) * 256
D_IN_MAIN = 2 * D_A + D_B + 6 * D_KV
N_GATES = 3 * B_HEADS
RMS_EPS = 1e-6
NEG_INF = -1e30
FORCE_SCORE = 1e4
QK_SCALE = HEAD_DIM ** -0.5

COL_XA = 0
COL_GA = D_A // 128
COL_Q = 2 * D_A // 128
COL_KVC = (2 * D_A + D_B) // 128
COL_KVS = COL_KVC + 2 * D_KV // 128
COL_KVW = COL_KVS + 2 * D_KV // 128

V7X_VMEM_LIMIT_BYTES = 56 * 1024 * 1024
LANE = 128
ATT_TQ = 256
ATT_CK = 512
F32 = jnp.float32
BF16 = jnp.bfloat16


def _row_tile(m, cap=1024):
    for t in range(cap - cap % 16, 15, -16):
        if m % t == 0:
            return t
    raise ValueError(m)


def _col_tile(n, cap=512):
    for t in range(cap, LANE - 1, -LANE):
        if n % t == 0:
            return t
    raise ValueError(n)


def _round_up(x, m):
    return -(-x // m) * m


def _params(sem):
    return pltpu.CompilerParams(dimension_semantics=sem, vmem_limit_bytes=V7X_VMEM_LIMIT_BYTES)


def _dot(a, b):
    return jnp.dot(a, b, preferred_element_type=F32)


def _dot_nt(a, b):
    return lax.dot_general(a, b, (((1,), (1,)), ((), ())), preferred_element_type=F32)


def _dot_f32_exact(x, w_bf16):
    hi = x.astype(BF16)
    r1 = x - hi.astype(F32)
    mid = r1.astype(BF16)
    lo = (r1 - mid.astype(F32)).astype(BF16)
    return _dot(hi, w_bf16) + (_dot(mid, w_bf16) + _dot(lo, w_bf16))


def _gelu_tanh(x):
    return 0.5 * x * (1.0 + jnp.tanh(0.7978845608028654 * (x + 0.044715 * (x * x * x))))


def _lanes(x128, w):
    return x128 if w == LANE else jnp.concatenate([x128] * (w // LANE), axis=1)


def _rmsnorm_kernel(x_ref, g_ref, o_ref):
    x = x_ref[...]
    y = x * lax.rsqrt(jnp.mean(x * x, axis=-1, keepdims=True) + RMS_EPS)
    o_ref[...] = (y * g_ref[...]).astype(o_ref.dtype)


def _rmsnorm(x, g, out_dtype, row0=0, n_rows=None):
    d = x.shape[1]
    m = n_rows or x.shape[0]
    tm = _row_tile(m, 768)
    assert row0 % tm == 0
    rb0 = row0 // tm
    return pl.pallas_call(
        _rmsnorm_kernel,
        out_shape=jax.ShapeDtypeStruct((m, d), out_dtype),
        grid=(m // tm,),
        in_specs=[pl.BlockSpec((tm, d), lambda i: (rb0 + i, 0)), pl.BlockSpec((1, d), lambda i: (0, 0))],
        out_specs=pl.BlockSpec((tm, d), lambda i: (i, 0)),
        compiler_params=_params(("parallel",)),
        name="rmsnorm",
    )(x, g.reshape(1, d))


def _wspec(layer, k, tn):
    return pl.BlockSpec((None, k, tn), lambda i, j: (layer, 0, j))


def _mm_kernel(x_ref, w_ref, o_ref):
    o_ref[...] = _dot(x_ref[...], w_ref[...].astype(BF16)).astype(o_ref.dtype)


def _matmul(x, w, layer, out_dtype=F32, name="matmul", row_cap=1024, col_cap=512):
    m, k = x.shape
    n = w.shape[2]
    tm, tn = _row_tile(m, row_cap), _col_tile(n, col_cap)
    return pl.pallas_call(
        _mm_kernel,
        out_shape=jax.ShapeDtypeStruct((m, n), out_dtype),
        grid=(m // tm, n // tn),
        in_specs=[pl.BlockSpec((tm, k), lambda i, j: (i, 0)), _wspec(layer, k, tn)],
        out_specs=pl.BlockSpec((tm, tn), lambda i, j: (i, j)),
        compiler_params=_params(("parallel", "arbitrary")),
        name=name,
    )(x, w)


def _in_proj_kernel(x_ref, w_ref, z_ref, kv_ref, *, j_kv0, tiles_per_seg):
    j = pl.program_id(1)
    tm, tn = z_ref.shape
    heads_per_tile = tn // HEAD_DIM
    rows_per_token = 2 * KV_HEADS
    acc = _dot(x_ref[...], w_ref[...].astype(BF16))
    z_ref[...] = acc
    for part in range(tiles_per_seg):
        @pl.when((j >= j_kv0) & ((j - j_kv0) % tiles_per_seg == part))
        def _():
            for c in range(heads_per_tile):
                kv_ref[pl.ds(part * heads_per_tile + c, tm, stride=rows_per_token), :] = (
                    acc[:, c * HEAD_DIM:(c + 1) * HEAD_DIM])


def _in_proj(x, w, layer):
    m, k = x.shape
    n = D_IN_MAIN
    tm, tn = _row_tile(m, WIDE_ROW_CAP), _col_tile(n)
    seg = 2 * D_KV
    assert seg % tn == 0 and (COL_KVC * LANE) % tn == 0 and n == COL_KVC * LANE + 3 * seg and w.shape[2] >= n
    tiles_per_seg = seg // tn
    j_kv0 = COL_KVC * LANE // tn
    rpt = 2 * KV_HEADS
    return pl.pallas_call(
        functools.partial(_in_proj_kernel, j_kv0=j_kv0, tiles_per_seg=tiles_per_seg),
        out_shape=(jax.ShapeDtypeStruct((m, n), F32), jax.ShapeDtypeStruct((3, m * rpt, HEAD_DIM), F32)),
        grid=(m // tm, n // tn),
        in_specs=[pl.BlockSpec((tm, k), lambda i, j: (i, 0)), _wspec(layer, k, tn)],
        out_specs=(pl.BlockSpec((tm, tn), lambda i, j: (i, j)),
                   pl.BlockSpec((None, tm * rpt, HEAD_DIM),
                                lambda i, j: (jnp.maximum(j - j_kv0, 0) // tiles_per_seg, i, 0))),
        compiler_params=_params(("parallel", "arbitrary")),
        name="ab_in_proj",
    )(x, w)


def _mm_res_kernel(x_ref, w_ref, r_ref, o_ref):
    o_ref[...] = r_ref[...] + _dot(x_ref[...], w_ref[...].astype(BF16))


def _matmul_residual(x, w, layer, r, name="matmul_res", row_cap=1024, col_cap=512):
    m, k = x.shape
    n = w.shape[2]
    tm, tn = _row_tile(m, row_cap), _col_tile(n, col_cap)
    return pl.pallas_call(
        _mm_res_kernel,
        out_shape=jax.ShapeDtypeStruct((m, n), F32),
        grid=(m // tm, n // tn),
        in_specs=[pl.BlockSpec((tm, k), lambda i, j: (i, 0)), _wspec(layer, k, tn),
                  pl.BlockSpec((tm, tn), lambda i, j: (i, j))],
        out_specs=pl.BlockSpec((tm, tn), lambda i, j: (i, j)),
        compiler_params=_params(("parallel", "arbitrary")),
        name=name,
    )(x, w, r)


def _mm2_res_kernel(x1_ref, x2_ref, w1_ref, w2_ref, r_ref, o_ref):
    o_ref[...] = r_ref[...] + (_dot(x1_ref[...], w1_ref[...].astype(BF16))
                               + _dot(x2_ref[...], w2_ref[...].astype(BF16)))


def _matmul2_residual(x1, x2, w, layer, r, name="matmul2_res"):
    m, k1 = x1.shape
    k2 = x2.shape[1]
    n = w.shape[2]
    tm, tn = _row_tile(m, WIDE_ROW_CAP), _col_tile(n)
    assert k1 % 8 == 0 and k1 == k2
    return pl.pallas_call(
        _mm2_res_kernel,
        out_shape=jax.ShapeDtypeStruct((m, n), F32),
        grid=(m // tm, n // tn),
        in_specs=[pl.BlockSpec((tm, k1), lambda i, j: (i, 0)), pl.BlockSpec((tm, k2), lambda i, j: (i, 0)),
                  pl.BlockSpec((None, k1, tn), lambda i, j: (layer, 0, j)),
                  pl.BlockSpec((None, k2, tn), lambda i, j: (layer, 1, j)),
                  pl.BlockSpec((tm, tn), lambda i, j: (i, j))],
        out_specs=pl.BlockSpec((tm, tn), lambda i, j: (i, j)),
        compiler_params=_params(("parallel", "arbitrary")),
        name=name,
    )(x1, x2, w, w, r)


def _ffn_up_kernel(x_ref, wg_ref, wu_ref, o_ref):
    x = x_ref[...]
    g = _dot(x, wg_ref[...].astype(BF16))
    u = _dot(x, wu_ref[...].astype(BF16))
    o_ref[...] = (g * jax.nn.sigmoid(g) * u).astype(o_ref.dtype)


WIDE_ROW_CAP = 1536
F32_WEIGHT_COL_CAP = 256


def _ffn_up(x, wg, wu, layer):
    m, k = x.shape
    n = wg.shape[2]
    tm, tn = _row_tile(m, WIDE_ROW_CAP), _col_tile(n, F32_WEIGHT_COL_CAP)
    return pl.pallas_call(
        _ffn_up_kernel,
        out_shape=jax.ShapeDtypeStruct((m, n), BF16),
        grid=(m // tm, n // tn),
        in_specs=[pl.BlockSpec((tm, k), lambda i, j: (i, 0)), _wspec(layer, k, tn), _wspec(layer, k, tn)],
        out_specs=pl.BlockSpec((tm, tn), lambda i, j: (i, j)),
        compiler_params=_params(("parallel", "arbitrary")),
        name="ffn_up",
    )(x, wg, wu)


FFN_DOWN_COL_CAP = 256


def _lru_kernel(xa_ref, ga_ref, h0_ref, buf_ref, cw_ref, cb_ref, wa_ref, ba_ref, wx_ref, bx_ref, lam_ref,
                y_ref, ht_ref, nbuf_ref):
    t_len = xa_ref.shape[0]
    x = xa_ref[...]
    buf = buf_ref[...]
    cw = cw_ref[...]
    row = lax.broadcasted_iota(jnp.int32, x.shape, 0)
    nb = A_CONV - 1
    xc = x * cw[nb:nb + 1]
    for s in range(1, A_CONV):
        xs = pltpu.roll(x, s, 0)
        for r in range(s):
            xs = jnp.where(row == r, buf[nb + r - s:nb + r - s + 1], xs)
        xc = xc + xs * cw[nb - s:nb - s + 1]
    xc = xc + cb_ref[...]
    xcb = xc.astype(BF16)
    r_g = jax.nn.sigmoid(_dot(xcb, wa_ref[...].astype(BF16)) + ba_ref[...])
    i_g = jax.nn.sigmoid(_dot(xcb, wx_ref[...].astype(BF16)) + bx_ref[...])
    nl = -lam_ref[...]
    softplus = jnp.maximum(nl, 0.0) + jnp.log1p(jnp.exp(-jnp.abs(nl)))
    log_a = -LRU_C * r_g * softplus
    a = jnp.exp(log_a)
    u = jnp.sqrt(-jnp.tanh(log_a) * (a * a + 1.0)) * i_g * xc
    d = 1
    while d < t_len:
        if d % 8:
            keep = row >= d
            a_prev = pltpu.roll(a, d, 0)
            u_prev = pltpu.roll(u, d, 0)
            u = jnp.where(keep, a * u_prev + u, u)
            a = jnp.where(keep, a * a_prev, a)
        else:
            a_hi = a[d:]
            u = jnp.concatenate([u[:d], a_hi * u[:t_len - d] + u[d:]], axis=0)
            a = jnp.concatenate([a[:d], a_hi * a[:t_len - d]], axis=0)
        d *= 2
    h = a * h0_ref[...] + u
    y_ref[...] = (h * _gelu_tanh(ga_ref[...])).astype(y_ref.dtype)
    ht_ref[...] = h[t_len - 1:t_len]
    nbuf_ref[...] = x[t_len - nb:t_len]


def _into_zeros(kernel_fn, n_in, out_rows, width, dtype):
    def wrapped(*refs):
        return kernel_fn(*refs[:n_in], *refs[n_in + 1:])
    return wrapped, pl.BlockSpec(memory_space=pl.ANY), jnp.zeros((out_rows, width), dtype), {n_in: 0}


def _lru_mixer(z, row0, n_seq, t_len, h0, buf, conv_w, conv_b, wa, ba, wx, bx, lam, out_dtype, out_rows=None):
    assert row0 % t_len == 0 and t_len >= A_CONV - 1 and t_len % 8 == 0
    rb0 = row0 // t_len
    nb = A_CONV - 1
    out_rows = out_rows or n_seq * t_len
    vec = lambda v: v.reshape(1, D_A)
    cspec = pl.BlockSpec((1, A_BLK), lambda s, h: (0, h))
    wspec = pl.BlockSpec((None, A_BLK, A_BLK), lambda s, h: (h, 0, 0))
    body, base_spec, base, alias = _into_zeros(_lru_kernel, 11, out_rows, D_A, out_dtype)
    return pl.pallas_call(
        body,
        out_shape=(jax.ShapeDtypeStruct((out_rows, D_A), out_dtype),
                   jax.ShapeDtypeStruct((n_seq, 1, D_A), F32),
                   jax.ShapeDtypeStruct((n_seq, nb, D_A), F32)),
        grid=(n_seq, A_HEADS),
        in_specs=[pl.BlockSpec((t_len, A_BLK), lambda s, h: (rb0 + s, COL_XA + h)),
                  pl.BlockSpec((t_len, A_BLK), lambda s, h: (rb0 + s, COL_GA + h)),
                  pl.BlockSpec((None, 1, A_BLK), lambda s, h: (s, 0, h)),
                  pl.BlockSpec((None, nb, A_BLK), lambda s, h: (s, 0, h)),
                  pl.BlockSpec((A_CONV, A_BLK), lambda s, h: (0, h)),
                  cspec, wspec, cspec, wspec, cspec, cspec, base_spec],
        out_specs=(pl.BlockSpec((t_len, A_BLK), lambda s, h: (s, h)),
                   pl.BlockSpec((None, 1, A_BLK), lambda s, h: (s, 0, h)),
                   pl.BlockSpec((None, nb, A_BLK), lambda s, h: (s, 0, h))),
        input_output_aliases=alias,
        compiler_params=_params(("parallel", "parallel")),
        name="rglru_mixer",
    )(z, z, h0.reshape(n_seq, 1, D_A), buf, conv_w, vec(conv_b), wa, vec(ba), wx, vec(bx), vec(lam), base)


def _compress_kernel(x_ref, pe_ref, w1_ref, b1_ref, w2_ref, o_ref):
    n_sub = o_ref.shape[0]
    pe = pe_ref[...]
    y0 = jnp.zeros((n_sub, HEAD_DIM), F32)
    y1 = jnp.zeros((n_sub, HEAD_DIM), F32)
    for j in range(CMP_STRIDE):
        xj = x_ref[pl.ds(j, n_sub, stride=CMP_STRIDE), :]
        y0 = y0 + _dot((xj + pe[j:j + 1]).astype(BF16), w1_ref[j])
        y1 = y1 + _dot((xj + pe[CMP_STRIDE + j:CMP_STRIDE + j + 1]).astype(BF16), w1_ref[CMP_STRIDE + j])
    hid = y0 + pltpu.roll(y1, n_sub - 1, 0)
    hid = _gelu_tanh(hid + b1_ref[...])
    o_ref[...] = _dot(hid.astype(BF16), w2_ref[...]).astype(o_ref.dtype)


def _compress(x2d, x_index, n_seq, t_len, pe, w1, b1, w2):
    assert SUB_PER_CMP == 2 and t_len % (8 * CMP_STRIDE) == 0
    n_sub = t_len // CMP_STRIDE
    nt = 2 * KV_HEADS
    return pl.pallas_call(
        _compress_kernel,
        out_shape=jax.ShapeDtypeStruct((n_seq, nt, n_sub, HEAD_DIM), BF16),
        grid=(n_seq, nt),
        in_specs=[pl.BlockSpec((t_len, HEAD_DIM), x_index),
                  pl.BlockSpec((None, CMP_LEN, HEAD_DIM), lambda s, c: (c // KV_HEADS, 0, 0)),
                  pl.BlockSpec((None, CMP_LEN, HEAD_DIM, HEAD_DIM), lambda s, c: (c // KV_HEADS, 0, 0, 0)),
                  pl.BlockSpec((None, 1, HEAD_DIM), lambda s, c: (c // KV_HEADS, 0, 0)),
                  pl.BlockSpec((None, HEAD_DIM, HEAD_DIM), lambda s, c: (c // KV_HEADS, 0, 0))],
        out_specs=pl.BlockSpec((None, None, n_sub, HEAD_DIM), lambda s, c: (s, c, 0, 0)),
        compiler_params=_params(("parallel", "arbitrary")),
        name="nsa_compress",
    )(x2d, pe, w1.astype(BF16), b1.reshape(2, 1, HEAD_DIM), w2.astype(BF16))


def _head_rows(ref, c, n_tok):
    return ref[pl.ds(c, n_tok, stride=2 * KV_HEADS), :]


def _page_gather_kernel(pt_ref, *refs):
    page_refs, o_ref = refs[:-1], refs[-1]
    for k, c_ref in enumerate(page_refs):
        for c in range(2 * KV_HEADS):
            o_ref[c, k * PAGE_SIZE:(k + 1) * PAGE_SIZE, :] = _head_rows(c_ref, c, PAGE_SIZE)


def _pages_per_step(n_pages, want=4):
    pps = want
    while n_pages % pps:
        pps //= 2
    return pps


def _page_gather(cache_rows, layer, page_table):
    nb, n_pages = page_table.shape
    nt = 2 * KV_HEADS
    pps = _pages_per_step(n_pages, 16)

    def page_spec(k):
        return pl.BlockSpec((None, None, PAGE_SIZE * nt, HEAD_DIM), lambda b, p, pt: (layer, pt[b, p * pps + k], 0, 0))

    return pl.pallas_call(
        _page_gather_kernel,
        out_shape=jax.ShapeDtypeStruct((nb, nt, n_pages * PAGE_SIZE, HEAD_DIM), cache_rows.dtype),
        grid_spec=pltpu.PrefetchScalarGridSpec(
            num_scalar_prefetch=1, grid=(nb, n_pages // pps),
            in_specs=[page_spec(k) for k in range(pps)],
            out_specs=pl.BlockSpec((None, nt, pps * PAGE_SIZE, HEAD_DIM), lambda b, p, pt: (b, 0, p, 0))),
        compiler_params=_params(("parallel", "arbitrary")),
        name="page_gather",
    )(page_table, *([cache_rows] * pps))


def _stack_heads(q, g0):
    return jnp.concatenate([q[:, (g0 + p) * HEAD_DIM:(g0 + p + 1) * HEAD_DIM] for p in range(HPG)], axis=0).astype(BF16)


def _softmax(sb):
    m = jnp.max(sb, axis=-1, keepdims=True)
    e = jnp.exp(sb - m)
    return e / jnp.sum(e, axis=-1, keepdims=True)


def _cmp_branch(q4, kc, vc, t_rows, slope, n_cmp):
    ncp = kc.shape[0]
    s = _dot_nt(q4, kc) * QK_SCALE
    col = lax.broadcasted_iota(jnp.int32, s.shape, 1)
    dist_i = t_rows - (col * CMP_STRIDE + (CMP_LEN - 1))
    pick = lambda x, other: jnp.where(dist_i >= 0, jnp.where(col < n_cmp, x, other), other)
    p = _softmax(pick(s - _lanes(slope, ncp) * dist_i.astype(F32), NEG_INF))
    p = pick(p, 0.0)
    return _dot(p.astype(BF16), vc), p


def _select_blocks(imp, pool, t_q, n_sel):
    imp_sel = _dot_f32_exact(imp, pool)
    jcol = lax.broadcasted_iota(jnp.int32, imp_sel.shape, 1)
    cur = lax.shift_right_logical(t_q, int(np.log2(SEL_BLOCK)))
    score = jnp.where(jcol * SEL_BLOCK <= t_q, imp_sel, -1.0)
    for forced_blk in (cur - 1, cur, 0):
        score = jnp.where(jcol == forced_blk, FORCE_SCORE, score)
    score = jnp.where(jcol < n_sel, score, -2.0)
    rank = jnp.zeros(score.shape, F32)
    for i in range(n_sel):
        ci = score[:, i:i + 1]
        tie = jnp.where(jcol > i, 1.0, 0.0)
        rank = rank + jnp.where(ci > score, 1.0, jnp.where(ci == score, tie, 0.0))
    top = min(SEL_TOP, n_sel)
    return jnp.where(rank < float(top), 1.0, 0.0)


def _pool_matrix(n_cmp, ncp, n_sel, nsp):
    assert n_cmp + SUB_PER_CMP - 1 <= n_sel * SUB_PER_SEL
    m = np.zeros((ncp, nsp), np.float32)
    for c in range(n_cmp):
        for n in range(SUB_PER_CMP):
            m[c, (c + n) // SUB_PER_SEL] += 1.0 / SUB_PER_CMP
    return jnp.asarray(m, BF16)


def _slope_rows(t_rep):
    h = np.arange(1, B_HEADS + 1, dtype=np.float64)
    s = np.exp2(-8.0 * h / B_HEADS).astype(np.float32).reshape(KV_HEADS, HPG, 1, 1)
    return jnp.asarray(np.broadcast_to(s, (KV_HEADS, HPG, t_rep, LANE)).reshape(KV_HEADS, HPG * t_rep, LANE))


def _expand_matrix(shape, blk0):
    j = lax.broadcasted_iota(jnp.int32, shape, 0)
    k = lax.broadcasted_iota(jnp.int32, shape, 1)
    return jnp.where(j == blk0 + lax.shift_right_logical(k, int(np.log2(SEL_BLOCK))), 1.0, 0.0).astype(BF16)


def _gate_combine(gs, p, tq, o_c, o_s, o_w):
    rows = slice(p * tq, (p + 1) * tq)
    return (gs[:, p:p + 1] * o_c[rows] + gs[:, HPG + p:HPG + p + 1] * o_s[rows]
            + gs[:, 2 * HPG + p:2 * HPG + p + 1] * o_w[rows])


def _attn_prompt_kernel(q_ref, kc_ref, vc_ref, ks_ref, vs_ref, kw_ref, vw_ref, gt_ref, slope_ref, pool_ref,
                        o_ref, ksb, vsb, kwb, vwb, *, n_cmp, n_sel):
    qi = pl.program_id(2)
    tq = q_ref.shape[0]
    t_total = ks_ref.shape[0]
    rr = HPG * tq

    @pl.when(qi == 0)
    def _():
        ksb[...] = ks_ref[...].astype(BF16)
        vsb[...] = vs_ref[...].astype(BF16)
        kwb[...] = kw_ref[...].astype(BF16)
        vwb[...] = vw_ref[...].astype(BF16)

    t0 = qi * tq
    q4 = _stack_heads(q_ref[...], 0)
    slope = slope_ref[...]

    def t_rows(width):
        r = lax.broadcasted_iota(jnp.int32, (rr, width), 0)
        return t0 + (r & (tq - 1))

    ncp = kc_ref.shape[0]
    o_c, p_c = _cmp_branch(q4, kc_ref[...], vc_ref[...], t_rows(ncp), slope, n_cmp)
    imp = p_c[0:tq]
    for p in range(1, HPG):
        imp = imp + p_c[p * tq:(p + 1) * tq]
    nsp = pool_ref.shape[1]
    t_q = t0 + lax.broadcasted_iota(jnp.int32, (tq, nsp), 0)
    sel = _select_blocks(imp, pool_ref[...], t_q, n_sel).astype(BF16)
    sel4 = jnp.concatenate([sel] * HPG, axis=0)

    ck = ATT_CK
    tr_ck = t_rows(ck)
    slope_ck = _lanes(slope, ck)
    col_ck = lax.broadcasted_iota(jnp.int32, (rr, ck), 1)

    def sel_chunk(c, carry):
        m, l, acc = carry
        k0 = pl.multiple_of(c * ck, ck)
        s = _dot_nt(q4, ksb[pl.ds(k0, ck), :]) * QK_SCALE
        dist_i = tr_ck - (k0 + col_ck)
        selx = _dot(sel4, _expand_matrix((nsp, ck), c * (ck // SEL_BLOCK)))
        sb = jnp.where(dist_i >= 0, jnp.where(selx > 0.5, s - slope_ck * dist_i.astype(F32), NEG_INF), NEG_INF)
        m_new = jnp.maximum(m, jnp.max(sb, axis=-1, keepdims=True))
        alpha = jnp.exp(m - m_new)
        e = jnp.exp(sb - m_new)
        l = alpha * l + jnp.sum(e, axis=-1, keepdims=True)
        acc = alpha * acc + _dot(e.astype(BF16), vsb[pl.ds(k0, ck), :])
        return m_new, l, acc

    n_chunks = (t0 + tq + ck - 1) // ck
    m0 = jnp.full((rr, 1), NEG_INF, F32)
    _, l_s, acc_s = lax.fori_loop(0, n_chunks, sel_chunk, (m0, jnp.zeros((rr, 1), F32), jnp.zeros((rr, HEAD_DIM), F32)))
    o_s = acc_s / l_s

    wk = WINDOW + tq
    w0 = pl.multiple_of(jnp.maximum(t0 - WINDOW, 0), tq)
    s = _dot_nt(q4, kwb[pl.ds(w0, wk), :]) * QK_SCALE
    dist_i = t_rows(wk) - (w0 + lax.broadcasted_iota(jnp.int32, (rr, wk), 1))
    logit = s - _lanes(slope, wk) * dist_i.astype(F32)
    p_w = _softmax(jnp.where(dist_i >= 0, jnp.where(dist_i < WINDOW, logit, NEG_INF), NEG_INF))
    o_w = _dot(p_w.astype(BF16), vwb[pl.ds(w0, wk), :])

    gs = jax.nn.sigmoid(gt_ref[...])
    for p in range(HPG):
        o_ref[:, p * HEAD_DIM:(p + 1) * HEAD_DIM] = _gate_combine(gs, p, tq, o_c, o_s, o_w).astype(o_ref.dtype)


def _attn_prompt(z, gt, kvc_cmp, n_seq, t_len, out_rows=None):
    tq = ATT_TQ
    assert t_len % ATT_CK == 0 and t_len >= WINDOW + tq and tq & (tq - 1) == 0
    n_cmp = (t_len - CMP_LEN) // CMP_STRIDE + 1
    n_sel = -(-t_len // SEL_BLOCK)
    ncp = kvc_cmp.shape[2]
    nsp = _round_up(n_sel, LANE)
    nq = t_len // tq
    rr = HPG * tq
    kvspec = lambda col0: pl.BlockSpec((t_len, HEAD_DIM), lambda b, g, i: (b, col0 + g))
    out_rows = out_rows or n_seq * t_len
    body, base_spec, base, alias = _into_zeros(functools.partial(_attn_prompt_kernel, n_cmp=n_cmp, n_sel=n_sel),
                                               10, out_rows, D_B, BF16)
    return pl.pallas_call(
        body,
        out_shape=jax.ShapeDtypeStruct((out_rows, D_B), BF16),
        grid=(n_seq, KV_HEADS, nq),
        in_specs=[pl.BlockSpec((tq, HPG * HEAD_DIM), lambda b, g, i: (b * nq + i, COL_Q // HPG + g)),
                  pl.BlockSpec((None, None, ncp, HEAD_DIM), lambda b, g, i: (b, g, 0, 0)),
                  pl.BlockSpec((None, None, ncp, HEAD_DIM), lambda b, g, i: (b, KV_HEADS + g, 0, 0)),
                  kvspec(COL_KVS), kvspec(COL_KVS + KV_HEADS), kvspec(COL_KVW), kvspec(COL_KVW + KV_HEADS),
                  pl.BlockSpec((tq, LANE), lambda b, g, i: (b * nq + i, g)),
                  pl.BlockSpec((None, rr, LANE), lambda b, g, i: (g, 0, 0)),
                  pl.BlockSpec((ncp, nsp), lambda b, g, i: (0, 0)), base_spec],
        out_specs=pl.BlockSpec((tq, HPG * HEAD_DIM), lambda b, g, i: (b * nq + i, g)),
        scratch_shapes=[pltpu.VMEM((t_len, HEAD_DIM), BF16)] * 4,
        input_output_aliases=alias,
        compiler_params=_params(("parallel", "parallel", "arbitrary")),
        name="nsa_prompt_attention",
    )(z, kvc_cmp, kvc_cmp, z, z, z, z, gt, _slope_rows(tq), _pool_matrix(n_cmp, ncp, n_sel, nsp), base)


def _attn_dec_select_kernel(q_ref, kc_ref, vc_ref, slope_ref, pool_ref, oc_ref, sel_ref, *, n_cmp, n_sel, past_len):
    ts = q_ref.shape[0]
    rr = HPG * ts
    ncp = kc_ref.shape[0]
    q4 = _stack_heads(q_ref[...], 0)
    t_rows = past_len + (lax.broadcasted_iota(jnp.int32, (rr, ncp), 0) & (ts - 1))
    o_c, p_c = _cmp_branch(q4, kc_ref[...], vc_ref[...], t_rows, slope_ref[...], n_cmp)
    imp = p_c[0:ts]
    for p in range(1, HPG):
        imp = imp + p_c[p * ts:(p + 1) * ts]
    nsp = pool_ref.shape[1]
    t_q = past_len + lax.broadcasted_iota(jnp.int32, (ts, nsp), 0)
    oc_ref[...] = o_c
    sel_ref[...] = _select_blocks(imp, pool_ref[...], t_q, n_sel)


def _attn_dec_select(z, row0, kvc_cmp, n_seq, ts, past_len, n_cmp, n_sel):
    assert row0 % ts == 0 and ts & (ts - 1) == 0 and ts % 8 == 0
    rb0 = row0 // ts
    ncp = kvc_cmp.shape[2]
    nsp = _round_up(n_sel, LANE)
    rr = HPG * ts
    return pl.pallas_call(
        functools.partial(_attn_dec_select_kernel, n_cmp=n_cmp, n_sel=n_sel, past_len=past_len),
        out_shape=(jax.ShapeDtypeStruct((n_seq, KV_HEADS, rr, HEAD_DIM), F32),
                   jax.ShapeDtypeStruct((n_seq, KV_HEADS, ts, nsp), F32)),
        grid=(n_seq, KV_HEADS),
        in_specs=[pl.BlockSpec((ts, HPG * HEAD_DIM), lambda b, g: (rb0 + b, COL_Q // HPG + g)),
                  pl.BlockSpec((None, None, ncp, HEAD_DIM), lambda b, g: (b, g, 0, 0)),
                  pl.BlockSpec((None, None, ncp, HEAD_DIM), lambda b, g: (b, KV_HEADS + g, 0, 0)),
                  pl.BlockSpec((None, rr, LANE), lambda b, g: (g, 0, 0)),
                  pl.BlockSpec((ncp, nsp), lambda b, g: (0, 0))],
        out_specs=(pl.BlockSpec((None, None, rr, HEAD_DIM), lambda b, g: (b, g, 0, 0)),
                   pl.BlockSpec((None, None, ts, nsp), lambda b, g: (b, g, 0, 0))),
        compiler_params=_params(("parallel", "parallel")),
        name="nsa_decode_select",
    )(z, kvc_cmp, kvc_cmp, _slope_rows(ts), _pool_matrix(n_cmp, ncp, n_sel, nsp))


def _attn_dec_kernel(pt_ref, q_ref, sel_ref, *rest, past_len, n_steps, pages_per_step):
    page_refs = rest[:pages_per_step]
    kvs_ref, kvw_ref, win_ref, oc_ref, gt_ref, slope_ref, o_ref, nwin_ref, m_sc, l_sc, acc_sc = rest[pages_per_step:]
    pg = pl.program_id(1)
    ts = q_ref.shape[0]
    rr = HPG * ts
    nsp = sel_ref.shape[-1]
    pad_rows = LANE - ts
    row = lax.broadcasted_iota(jnp.int32, (rr, LANE), 0)
    col = lax.broadcasted_iota(jnp.int32, (rr, LANE), 1)
    t_rows = past_len + (row & (ts - 1))

    @pl.when(pg == 0)
    def _():
        m_sc[...] = jnp.full(m_sc.shape, NEG_INF, F32)
        l_sc[...] = jnp.zeros(l_sc.shape, F32)
        acc_sc[...] = jnp.zeros(acc_sc.shape, F32)

    def sel_update(g, q4, kk, vv, dist_i, expand, real_key):
        width = kk.shape[0]
        s = _dot_nt(q4, kk) * QK_SCALE
        sel4 = jnp.concatenate([sel_ref[g].astype(BF16)] * HPG, axis=0)
        sb = s - _lanes(slope_ref[g], width) * dist_i.astype(F32)
        sb = jnp.where(dist_i >= 0, jnp.where(_dot(sel4, expand) > 0.5, sb, NEG_INF), NEG_INF)
        if real_key is not None:
            sb = jnp.where(real_key, sb, NEG_INF)
        m_old = m_sc[g]
        m_new = jnp.maximum(m_old, jnp.max(sb, axis=-1, keepdims=True))
        alpha = jnp.exp(m_old - m_new)
        e = jnp.exp(sb - m_new)
        l_sc[g] = alpha * l_sc[g] + jnp.sum(e, axis=-1, keepdims=True)
        acc_sc[g] = alpha * acc_sc[g] + _dot(e.astype(BF16), vv)
        m_sc[g] = m_new

    @pl.when(pg < n_steps)
    def _():
        width = pages_per_step * PAGE_SIZE
        k0 = pg * width
        expand = _expand_matrix((nsp, width), pg * (width // SEL_BLOCK))
        wrow_ = lax.broadcasted_iota(jnp.int32, (rr, width), 0)
        wcol_ = lax.broadcasted_iota(jnp.int32, (rr, width), 1)
        dist_i = (past_len + (wrow_ & (ts - 1))) - (k0 + wcol_)
        for g in range(KV_HEADS):
            q4 = _stack_heads(q_ref[...], g * HPG)
            kk = jnp.concatenate([_head_rows(r, g, PAGE_SIZE) for r in page_refs], axis=0).astype(BF16)
            vv = jnp.concatenate([_head_rows(r, KV_HEADS + g, PAGE_SIZE) for r in page_refs], axis=0).astype(BF16)
            sel_update(g, q4, kk, vv, dist_i, expand, None)

    @pl.when(pg == n_steps)
    def _():
        zpad = jnp.zeros((pad_rows, HEAD_DIM), F32)
        gs = jax.nn.sigmoid(gt_ref[...])
        jj = lax.broadcasted_iota(jnp.int32, (nsp, LANE), 0)
        kk_i = lax.broadcasted_iota(jnp.int32, (nsp, LANE), 1)
        new_blk = lax.shift_right_logical(past_len + kk_i, int(np.log2(SEL_BLOCK)))
        expand_new = jnp.where(jj == new_blk, 1.0, 0.0).astype(BF16)
        is_new = col < ts
        nt = 2 * KV_HEADS
        wlen = win_ref.shape[0] // nt
        wrow = lax.broadcasted_iota(jnp.int32, (rr, wlen), 0)
        wcol = lax.broadcasted_iota(jnp.int32, (rr, wlen), 1)
        dist_w = (past_len + (wrow & (ts - 1))) - (past_len - wlen + wcol)
        ok_w = (dist_w >= 0) & (dist_w < WINDOW)
        dist_n = t_rows - (past_len + col)
        ok_n = (dist_n >= 0) & (dist_n < WINDOW) & is_new
        for g in range(KV_HEADS):
            q4 = _stack_heads(q_ref[...], g * HPG)
            ksl = slice(g * HEAD_DIM, (g + 1) * HEAD_DIM)
            vsl = slice(D_KV + g * HEAD_DIM, D_KV + (g + 1) * HEAD_DIM)
            kn = jnp.concatenate([kvs_ref[:, ksl], zpad], axis=0).astype(BF16)
            vn = jnp.concatenate([kvs_ref[:, vsl], zpad], axis=0).astype(BF16)
            sel_update(g, q4, kn, vn, dist_n, expand_new, is_new)
            o_s = acc_sc[g] / l_sc[g]
            slope = slope_ref[g]
            s1 = _dot_nt(q4, _head_rows(win_ref, g, wlen).astype(BF16)) * QK_SCALE
            s1 = jnp.where(ok_w, s1 - _lanes(slope, wlen) * dist_w.astype(F32), NEG_INF)
            kwn = jnp.concatenate([kvw_ref[:, ksl], zpad], axis=0).astype(BF16)
            vwn = jnp.concatenate([kvw_ref[:, vsl], zpad], axis=0).astype(BF16)
            s2 = _dot_nt(q4, kwn) * QK_SCALE
            s2 = jnp.where(ok_n, s2 - slope * dist_n.astype(F32), NEG_INF)
            m = jnp.maximum(jnp.max(s1, axis=-1, keepdims=True), jnp.max(s2, axis=-1, keepdims=True))
            e1 = jnp.exp(s1 - m)
            e2 = jnp.exp(s2 - m)
            inv = 1.0 / (jnp.sum(e1, axis=-1, keepdims=True) + jnp.sum(e2, axis=-1, keepdims=True))
            o_w = (_dot((e1 * inv).astype(BF16), _head_rows(win_ref, KV_HEADS + g, wlen).astype(BF16))
                   + _dot((e2 * inv).astype(BF16), vwn))
            o_c = oc_ref[g]
            for p in range(HPG):
                c0 = (g * HPG + p) * HEAD_DIM
                o_ref[:, c0:c0 + HEAD_DIM] = _gate_combine(gs[:, g * LANE:(g + 1) * LANE], p, ts, o_c, o_s, o_w)
        keep = (wlen - ts) * nt
        nwin_ref[0:keep, :] = win_ref[ts * nt:wlen * nt, :]
        for c in range(nt):
            nwin_ref[pl.ds(keep + c, ts, stride=nt), :] = kvw_ref[:, c * HEAD_DIM:(c + 1) * HEAD_DIM]


def _attn_decode(z, gt, row0, n_seq, ts, page_table, cache_rows, layer, win_rows, o_c, sel, past_len):
    n_pages = page_table.shape[1]
    pps = _pages_per_step(n_pages, 8)
    n_steps = n_pages // pps
    rb0 = row0 // ts
    rr = HPG * ts
    nsp = sel.shape[-1]
    nt = 2 * KV_HEADS
    wrows = win_rows.shape[2]
    w = 2 * D_KV
    assert wrows == WINDOW * nt and past_len % SEL_BLOCK == 0 and PAGE_SIZE == LANE
    last = n_steps - 1

    def page_spec(k):
        return pl.BlockSpec((None, None, PAGE_SIZE * nt, HEAD_DIM),
                            lambda b, p, pt: (layer, pt[b, jnp.minimum(p, last) * pps + k], 0, 0))

    return pl.pallas_call(
        functools.partial(_attn_dec_kernel, past_len=past_len, n_steps=n_steps, pages_per_step=pps),
        out_shape=(jax.ShapeDtypeStruct((n_seq * ts, D_B), F32),
                   jax.ShapeDtypeStruct((n_seq, wrows, HEAD_DIM), F32)),
        grid_spec=pltpu.PrefetchScalarGridSpec(
            num_scalar_prefetch=1, grid=(n_seq, n_steps + 1),
            in_specs=[pl.BlockSpec((ts, D_B), lambda b, p, pt: (rb0 + b, COL_Q * LANE // D_B)),
                      pl.BlockSpec((None, KV_HEADS, ts, nsp), lambda b, p, pt: (b, 0, 0, 0))]
                     + [page_spec(k) for k in range(pps)]
                     + [pl.BlockSpec((ts, w), lambda b, p, pt: (rb0 + b, COL_KVS * LANE // w)),
                        pl.BlockSpec((ts, w), lambda b, p, pt: (rb0 + b, COL_KVW * LANE // w)),
                        pl.BlockSpec((None, None, wrows, HEAD_DIM), lambda b, p, pt: (layer, b, 0, 0)),
                        pl.BlockSpec((None, KV_HEADS, rr, HEAD_DIM), lambda b, p, pt: (b, 0, 0, 0)),
                        pl.BlockSpec((ts, KV_HEADS * LANE), lambda b, p, pt: (rb0 + b, 0)),
                        pl.BlockSpec((KV_HEADS, rr, LANE), lambda b, p, pt: (0, 0, 0))],
            out_specs=(pl.BlockSpec((ts, D_B), lambda b, p, pt: (b, 0)),
                       pl.BlockSpec((None, wrows, HEAD_DIM), lambda b, p, pt: (b, 0, 0))),
            scratch_shapes=[pltpu.VMEM((KV_HEADS, rr, 1), F32), pltpu.VMEM((KV_HEADS, rr, 1), F32),
                            pltpu.VMEM((KV_HEADS, rr, HEAD_DIM), F32)]),
        compiler_params=_params(("parallel", "arbitrary")),
        name="nsa_decode_attention",
    )(page_table, z, sel, *([cache_rows] * pps), z, z, win_rows, o_c, gt, _slope_rows(ts))


def _sconv_kernel(bg_ref, cg_ref, v_ref, buf_ref, cw_ref, y_ref, nbuf_ref):
    t_len = cg_ref.shape[0]
    nb = C_CONV - 1
    cv = cg_ref[...] * v_ref[...]
    buf = buf_ref[...]
    cw = cw_ref[...]
    row = lax.broadcasted_iota(jnp.int32, cv.shape, 0)
    y = cv * cw[nb:nb + 1]
    for s in range(1, C_CONV):
        xs = pltpu.roll(cv, s, 0)
        for r in range(s):
            xs = jnp.where(row == r, buf[nb + r - s:nb + r - s + 1], xs)
        y = y + xs * cw[nb - s:nb - s + 1]
    y_ref[...] = (bg_ref[...] * y).astype(y_ref.dtype)
    nbuf_ref[...] = cv[t_len - nb:t_len]


def _sconv_mixer(zc, row0, n_seq, t_len, buf, conv_w, out_dtype, out_rows=None):
    assert row0 % t_len == 0 and t_len >= C_CONV - 1 and t_len % 8 == 0
    rb0 = row0 // t_len
    nb = C_CONV - 1
    tc = 256 if t_len > 256 else 2048
    nc = D_C // tc
    zspec = lambda third: pl.BlockSpec((t_len, tc), lambda s, c: (rb0 + s, third * nc + c))
    out_rows = out_rows or n_seq * t_len
    body, base_spec, base, alias = _into_zeros(_sconv_kernel, 5, out_rows, D_C, out_dtype)
    return pl.pallas_call(
        body,
        out_shape=(jax.ShapeDtypeStruct((out_rows, D_C), out_dtype),
                   jax.ShapeDtypeStruct((n_seq, nb, D_C), F32)),
        grid=(n_seq, nc),
        in_specs=[zspec(0), zspec(1), zspec(2),
                  pl.BlockSpec((None, nb, tc), lambda s, c: (s, 0, c)),
                  pl.BlockSpec((C_CONV, tc), lambda s, c: (0, c)), base_spec],
        out_specs=(pl.BlockSpec((t_len, tc), lambda s, c: (s, c)),
                   pl.BlockSpec((None, nb, tc), lambda s, c: (s, 0, c))),
        input_output_aliases=alias,
        compiler_params=_params(("parallel", "parallel")),
        name="sconv_mixer",
    )(zc, zc, zc, buf, conv_w, base)


def _gate_weight(w_in):
    nl, k = w_in.shape[:2]
    wg = w_in[:, :, D_IN_MAIN:].reshape(nl, k, 3, KV_HEADS, HPG).transpose(0, 1, 3, 2, 4)
    wg = jnp.pad(wg.reshape(nl, k, KV_HEADS, 3 * HPG), ((0, 0), (0, 0), (0, 0), (0, LANE - 3 * HPG)))
    return wg.reshape(nl, k, KV_HEADS * LANE).astype(BF16)


def _ab_layer(h, hn, i, mp, bp, t_len, bs, ts, past_len, w, cache_kv_cmp, cache_kv_sel, state_win_kv, state_lru_h,
              state_lru_conv, page_table):
    z, kv_rows = _in_proj(hn, w["ab_w_in_bf16"], i)
    gt = _matmul(hn, w["ab_w_gate_bf16"], i, name="ab_gate_proj")
    nt = 2 * KV_HEADS
    rows_view = lambda a: a.reshape(a.shape[0], a.shape[1], -1, HEAD_DIM)
    lru_w = (w["ab_conv_w"][i], w["ab_conv_b"][i], w["ab_gate_a_w"][i], w["ab_gate_a_b"][i], w["ab_gate_x_w"][i],
             w["ab_gate_x_b"][i], w["ab_lru_lambda"][i])
    cmp_w = (w["ab_cmp_pe"][i], w["ab_cmp_w1"][i], w["ab_cmp_b1"][i], w["ab_cmp_w2"][i])
    n_pages = page_table.shape[1]
    tk = past_len + ts
    n_cmp_s = (tk - CMP_LEN) // CMP_STRIDE + 1
    n_sel_s = -(-tk // SEL_BLOCK)
    assert n_cmp_s == (past_len - CMP_LEN) // CMP_STRIDE + 1

    ya_p, hT_p, nbuf_p = _lru_mixer(z, 0, bp, t_len, jnp.zeros((bp, D_A), F32), jnp.zeros((bp, A_CONV - 1, D_A), F32),
                                    *lru_w, out_dtype=BF16, out_rows=mp + bs * ts)
    kvc_p = _compress(z, lambda s, c: (s, COL_KVC + c), bp, t_len, *cmp_w)
    ob_p = _attn_prompt(z, gt, kvc_p, bp, t_len, out_rows=mp + bs * ts)

    ya_s, hT_s, nbuf_s = _lru_mixer(z, mp, bs, ts, state_lru_h[i], state_lru_conv[i], *lru_w, out_dtype=F32)
    past_cmp = _page_gather(rows_view(cache_kv_cmp), i, page_table)
    kvc_s = _compress(past_cmp.reshape(bs * nt * past_len, HEAD_DIM), lambda s, c: (s * nt + c, 0), bs, past_len, *cmp_w)
    oc_s, sel_s = _attn_dec_select(z, mp, kvc_s, bs, ts, past_len, n_cmp_s, n_sel_s)
    ob_s, nwin_s = _attn_decode(z, gt, mp, bs, ts, page_table, rows_view(cache_kv_sel), i, rows_view(state_win_kv),
                                oc_s, sel_s, past_len)

    ya = lax.dynamic_update_slice(ya_p, ya_s.astype(BF16), (mp, 0))
    ob = lax.dynamic_update_slice(ob_p, ob_s.astype(BF16), (mp, 0))
    h = _matmul2_residual(ya, ob, w["ab_w_out_bf16"], i, h, name="ab_out_proj")

    kv5 = lambda a, nb_, tt: a.reshape(nb_, tt, 2, KV_HEADS, HEAD_DIM)
    wl = min(WINDOW, t_len)
    kvw_p = kv5(kv_rows[2, :mp * nt], bp, t_len)
    outs_p = (kv5(kv_rows[0, :mp * nt], bp, t_len), kv5(kv_rows[1, :mp * nt], bp, t_len),
              kvw_p[:, t_len - wl:], hT_p.reshape(bp, D_A), nbuf_p)
    outs_s = (kv5(kv_rows[0, mp * nt:], bs, ts), kv5(kv_rows[1, mp * nt:], bs, ts),
              kv5(nwin_s, bs, WINDOW), hT_s.reshape(bs, D_A), nbuf_s)
    return h, outs_p, outs_s


def _c_layer(h, hn, i, mp, bp, t_len, bs, ts, w, state_sconv):
    zc = _matmul(hn, w["c_w_in"], i, name="c_in_proj", row_cap=WIDE_ROW_CAP, col_cap=F32_WEIGHT_COL_CAP)
    y_p, nbuf_p = _sconv_mixer(zc, 0, bp, t_len, jnp.zeros((bp, C_CONV - 1, D_C), F32), w["c_conv_w"][i], BF16,
                               out_rows=mp + bs * ts)
    y_s, nbuf_s = _sconv_mixer(zc, mp, bs, ts, state_sconv[i], w["c_conv_w"][i], F32)
    y = lax.dynamic_update_slice(y_p, y_s.astype(BF16), (mp, 0))
    h = _matmul_residual(y, w["c_w_out_bf16"], i, h, name="c_out_proj", row_cap=WIDE_ROW_CAP)
    return h, nbuf_p, nbuf_s


def kernel(x_prompt, x_sample, cache_kv_cmp, cache_kv_sel, state_win_kv, state_lru_h, state_lru_conv, state_sconv,
           page_table, norm_mix, norm_ffn, norm_final, ab_w_in, ab_conv_w, ab_conv_b, ab_gate_a_w, ab_gate_a_b,
           ab_gate_x_w, ab_gate_x_b, ab_lru_lambda, ab_cmp_pe, ab_cmp_w1, ab_cmp_b1, ab_cmp_w2, ab_w_out,
           c_w_in, c_conv_w, c_w_out, ffn_w_gate, ffn_w_up, ffn_w_down):
    w = dict(ab_w_in_bf16=ab_w_in.astype(BF16), ab_w_gate_bf16=_gate_weight(ab_w_in), ab_conv_w=ab_conv_w,
             ab_conv_b=ab_conv_b, ab_gate_a_w=ab_gate_a_w, ab_gate_a_b=ab_gate_a_b, ab_gate_x_w=ab_gate_x_w,
             ab_gate_x_b=ab_gate_x_b, ab_lru_lambda=ab_lru_lambda, ab_cmp_pe=ab_cmp_pe, ab_cmp_w1=ab_cmp_w1,
             ab_cmp_b1=ab_cmp_b1, ab_cmp_w2=ab_cmp_w2, ab_w_out_bf16=ab_w_out.astype(BF16),
             c_w_in=c_w_in, c_conv_w=c_conv_w, c_w_out_bf16=c_w_out.astype(BF16))
    ffn_w_down_bf16 = ffn_w_down.astype(BF16)
    bp, t_len = x_prompt.shape[:2]
    bs, ts = x_sample.shape[:2]
    mp, ms = bp * t_len, bs * ts
    past_len = page_table.shape[1] * PAGE_SIZE
    h = jnp.concatenate([x_prompt.reshape(mp, D_MODEL), x_sample.reshape(ms, D_MODEL)], axis=0)
    p_ab, s_ab, p_c, s_c = [], [], [], []
    for layer in range(DEPTH):
        hn = _rmsnorm(h, norm_mix[layer], BF16)
        i = layer // 2
        if layer % 2 == 0:
            h, op, os_ = _ab_layer(h, hn, i, mp, bp, t_len, bs, ts, past_len, w, cache_kv_cmp, cache_kv_sel,
                                   state_win_kv, state_lru_h, state_lru_conv, page_table)
            p_ab.append(op)
            s_ab.append(os_)
        else:
            h, nbp, nbs = _c_layer(h, hn, i, mp, bp, t_len, bs, ts, w, state_sconv)
            p_c.append(nbp)
            s_c.append(nbs)
        hf = _rmsnorm(h, norm_ffn[layer], BF16)
        hid = _ffn_up(hf, ffn_w_gate, ffn_w_up, layer)
        h = _matmul_residual(hid, ffn_w_down_bf16, layer, h, name="ffn_down", col_cap=FFN_DOWN_COL_CAP)
    y_prompt = _rmsnorm(h, norm_final, F32, 0, mp).reshape(bp, t_len, D_MODEL)
    y_sample = _rmsnorm(h, norm_final, F32, mp, ms).reshape(bs, ts, D_MODEL)
    stack = lambda rows, j: jnp.stack([r[j] for r in rows])
    return (y_prompt, y_sample,
            stack(p_ab, 0), stack(p_ab, 1), stack(p_ab, 2), stack(p_ab, 3), stack(p_ab, 4), jnp.stack(p_c),
            stack(s_ab, 0), stack(s_ab, 1), stack(s_ab, 2), stack(s_ab, 3), stack(s_ab, 4), jnp.stack(s_c))
```

```python
import functools

import jax
import jax.numpy as jnp
import numpy as np
from jax import lax
from jax.experimental import pallas as pl
from jax.experimental.pallas import tpu as pltpu

D_MODEL = 4096
DEPTH = 4
PAGE_SIZE = 128
N_AB = (DEPTH + 1) // 2
N_C = DEPTH // 2
D_A = D_MODEL // 2
A_HEADS = 16
A_BLK = D_A // A_HEADS
A_CONV = 4
LRU_C = 8.0
B_HEADS = 16
HEAD_DIM = 128
KV_HEADS = 4
HPG = B_HEADS // KV_HEADS
D_B = B_HEADS * HEAD_DIM
D_KV = KV_HEADS * HEAD_DIM
CMP_LEN = 32
CMP_STRIDE = 16
SUB_PER_CMP = CMP_LEN // CMP_STRIDE
SEL_BLOCK = 64
SUB_PER_SEL = SEL_BLOCK // CMP_STRIDE
SEL_TOP = 16
WINDOW = 512
D_C = D_MODEL
C_CONV = 3
D_FF = -(-8 * D_MODEL // (3 * 256)) * 256
D_IN_MAIN = 2 * D_A + D_B + 6 * D_KV
N_GATES = 3 * B_HEADS
RMS_EPS = 1e-6
NEG_INF = -1e30
FORCE_SCORE = 1e4
QK_SCALE = HEAD_DIM ** -0.5

COL_XA = 0
COL_GA = D_A // 128
COL_Q = 2 * D_A // 128
COL_KVC = (2 * D_A + D_B) // 128
COL_KVS = COL_KVC + 2 * D_KV // 128
COL_KVW = COL_KVS + 2 * D_KV // 128

V7X_VMEM_LIMIT_BYTES = 56 * 1024 * 1024
LANE = 128
ATT_TQ = 256
ATT_CK = 512
F32 = jnp.float32
BF16 = jnp.bfloat16


def _row_tile(m, cap=1024):
    for t in range(cap - cap % 16, 15, -16):
        if m % t == 0:
            return t
    raise ValueError(m)


def _col_tile(n, cap=512):
    for t in range(cap, LANE - 1, -LANE):
        if n % t == 0:
            return t
    raise ValueError(n)


def _round_up(x, m):
    return -(-x // m) * m


def _params(sem):
    return pltpu.CompilerParams(dimension_semantics=sem, vmem_limit_bytes=V7X_VMEM_LIMIT_BYTES)


def _dot(a, b):
    return jnp.dot(a, b, preferred_element_type=F32)


def _dot_nt(a, b):
    return lax.dot_general(a, b, (((1,), (1,)), ((), ())), preferred_element_type=F32)


def _dot_f32_exact(x, w_bf16):
    hi = x.astype(BF16)
    r1 = x - hi.astype(F32)
    mid = r1.astype(BF16)
    lo = (r1 - mid.astype(F32)).astype(BF16)
    return _dot(hi, w_bf16) + (_dot(mid, w_bf16) + _dot(lo, w_bf16))


def _gelu_tanh(x):
    return 0.5 * x * (1.0 + jnp.tanh(0.7978845608028654 * (x + 0.044715 * (x * x * x))))


def _lanes(x128, w):
    return x128 if w == LANE else jnp.concatenate([x128] * (w // LANE), axis=1)


def _rmsnorm_kernel(x_ref, g_ref, o_ref):
    x = x_ref[...]
    y = x * lax.rsqrt(jnp.mean(x * x, axis=-1, keepdims=True) + RMS_EPS)
    o_ref[...] = (y * g_ref[...]).astype(o_ref.dtype)


def _rmsnorm(x, g, out_dtype, row0=0, n_rows=None):
    d = x.shape[1]
    m = n_rows or x.shape[0]
    tm = _row_tile(m, 768)
    assert row0 % tm == 0
    rb0 = row0 // tm
    return pl.pallas_call(
        _rmsnorm_kernel,
        out_shape=jax.ShapeDtypeStruct((m, d), out_dtype),
        grid=(m // tm,),
        in_specs=[pl.BlockSpec((tm, d), lambda i: (rb0 + i, 0)), pl.BlockSpec((1, d), lambda i: (0, 0))],
        out_specs=pl.BlockSpec((tm, d), lambda i: (i, 0)),
        compiler_params=_params(("parallel",)),
        name="rmsnorm",
    )(x, g.reshape(1, d))


def _wspec(layer, k, tn):
    return pl.BlockSpec((None, k, tn), lambda i, j: (layer, 0, j))


def _mm_kernel(x_ref, w_ref, o_ref):
    o_ref[...] = _dot(x_ref[...], w_ref[...].astype(BF16)).astype(o_ref.dtype)


def _matmul(x, w, layer, out_dtype=F32, name="matmul", row_cap=1024, col_cap=512):
    m, k = x.shape
    n = w.shape[2]
    tm, tn = _row_tile(m, row_cap), _col_tile(n, col_cap)
    return pl.pallas_call(
        _mm_kernel,
        out_shape=jax.ShapeDtypeStruct((m, n), out_dtype),
        grid=(m // tm, n // tn),
        in_specs=[pl.BlockSpec((tm, k), lambda i, j: (i, 0)), _wspec(layer, k, tn)],
        out_specs=pl.BlockSpec((tm, tn), lambda i, j: (i, j)),
        compiler_params=_params(("parallel", "arbitrary")),
        name=name,
    )(x, w)


def _in_proj_kernel(x_ref, w_ref, z_ref, kv_ref, *, j_kv0, tiles_per_seg):
    j = pl.program_id(1)
    tm, tn = z_ref.shape
    heads_per_tile = tn // HEAD_DIM
    rows_per_token = 2 * KV_HEADS
    acc = _dot(x_ref[...], w_ref[...].astype(BF16))
    z_ref[...] = acc
    for part in range(tiles_per_seg):
        @pl.when((j >= j_kv0) & ((j - j_kv0) % tiles_per_seg == part))
        def _():
            for c in range(heads_per_tile):
                kv_ref[pl.ds(part * heads_per_tile + c, tm, stride=rows_per_token), :] = (
                    acc[:, c * HEAD_DIM:(c + 1) * HEAD_DIM])


def _in_proj(x, w, layer):
    m, k = x.shape
    n = D_IN_MAIN
    tm, tn = _row_tile(m, WIDE_ROW_CAP), _col_tile(n)
    seg = 2 * D_KV
    assert seg % tn == 0 and (COL_KVC * LANE) % tn == 0 and n == COL_KVC * LANE + 3 * seg and w.shape[2] >= n
    tiles_per_seg = seg // tn
    j_kv0 = COL_KVC * LANE // tn
    rpt = 2 * KV_HEADS
    return pl.pallas_call(
        functools.partial(_in_proj_kernel, j_kv0=j_kv0, tiles_per_seg=tiles_per_seg),
        out_shape=(jax.ShapeDtypeStruct((m, n), F32), jax.ShapeDtypeStruct((3, m * rpt, HEAD_DIM), F32)),
        grid=(m // tm, n // tn),
        in_specs=[pl.BlockSpec((tm, k), lambda i, j: (i, 0)), _wspec(layer, k, tn)],
        out_specs=(pl.BlockSpec((tm, tn), lambda i, j: (i, j)),
                   pl.BlockSpec((None, tm * rpt, HEAD_DIM),
                                lambda i, j: (jnp.maximum(j - j_kv0, 0) // tiles_per_seg, i, 0))),
        compiler_params=_params(("parallel", "arbitrary")),
        name="ab_in_proj",
    )(x, w)


def _mm_res_kernel(x_ref, w_ref, r_ref, o_ref):
    o_ref[...] = r_ref[...] + _dot(x_ref[...], w_ref[...].astype(BF16))


def _matmul_residual(x, w, layer, r, name="matmul_res", row_cap=1024, col_cap=512):
    m, k = x.shape
    n = w.shape[2]
    tm, tn = _row_tile(m, row_cap), _col_tile(n, col_cap)
    return pl.pallas_call(
        _mm_res_kernel,
        out_shape=jax.ShapeDtypeStruct((m, n), F32),
        grid=(m // tm, n // tn),
        in_specs=[pl.BlockSpec((tm, k), lambda i, j: (i, 0)), _wspec(layer, k, tn),
                  pl.BlockSpec((tm, tn), lambda i, j: (i, j))],
        out_specs=pl.BlockSpec((tm, tn), lambda i, j: (i, j)),
        compiler_params=_params(("parallel", "arbitrary")),
        name=name,
    )(x, w, r)


def _mm2_res_kernel(x1_ref, x2_ref, w1_ref, w2_ref, r_ref, o_ref):
    o_ref[...] = r_ref[...] + (_dot(x1_ref[...], w1_ref[...].astype(BF16))
                               + _dot(x2_ref[...], w2_ref[...].astype(BF16)))


def _matmul2_residual(x1, x2, w, layer, r, name="matmul2_res"):
    m, k1 = x1.shape
    k2 = x2.shape[1]
    n = w.shape[2]
    tm, tn = _row_tile(m, WIDE_ROW_CAP), _col_tile(n)
    assert k1 % 8 == 0 and k1 == k2
    return pl.pallas_call(
        _mm2_res_kernel,
        out_shape=jax.ShapeDtypeStruct((m, n), F32),
        grid=(m // tm, n // tn),
        in_specs=[pl.BlockSpec((tm, k1), lambda i, j: (i, 0)), pl.BlockSpec((tm, k2), lambda i, j: (i, 0)),
                  pl.BlockSpec((None, k1, tn), lambda i, j: (layer, 0, j)),
                  pl.BlockSpec((None, k2, tn), lambda i, j: (layer, 1, j)),
                  pl.BlockSpec((tm, tn), lambda i, j: (i, j))],
        out_specs=pl.BlockSpec((tm, tn), lambda i, j: (i, j)),
        compiler_params=_params(("parallel", "arbitrary")),
        name=name,
    )(x1, x2, w, w, r)


def _ffn_up_kernel(x_ref, wg_ref, wu_ref, wd_ref, o_ref, wd_out_ref, *, cast_row_tiles):
    x = x_ref[...]
    g = _dot(x, wg_ref[...].astype(BF16))
    u = _dot(x, wu_ref[...].astype(BF16))
    o_ref[...] = (g * jax.nn.sigmoid(g) * u).astype(o_ref.dtype)

    @pl.when(pl.program_id(0) < cast_row_tiles)
    def _():
        wd_out_ref[...] = wd_ref[...].astype(BF16)


WIDE_ROW_CAP = 1536
F32_WEIGHT_COL_CAP = 256


def _ffn_up(x, wg, wu, wd, layer):
    m, k = x.shape
    n = wg.shape[2]
    tm, tn = _row_tile(m, WIDE_ROW_CAP), _col_tile(n, F32_WEIGHT_COL_CAP)
    ni, nj = m // tm, n // tn
    kd, nd = wd.shape[1:]
    cast_row_tiles = min(2, ni)
    n_blocks = cast_row_tiles * nj
    rb = kd // n_blocks
    assert rb * n_blocks == kd and rb % 16 == 0
    blk = lambda i, j: jnp.where(i < cast_row_tiles, i * nj + j, n_blocks - 1)
    return pl.pallas_call(
        functools.partial(_ffn_up_kernel, cast_row_tiles=cast_row_tiles),
        out_shape=(jax.ShapeDtypeStruct((m, n), BF16), jax.ShapeDtypeStruct((kd, nd), BF16)),
        grid=(ni, nj),
        in_specs=[pl.BlockSpec((tm, k), lambda i, j: (i, 0)), _wspec(layer, k, tn), _wspec(layer, k, tn),
                  pl.BlockSpec((None, rb, nd), lambda i, j: (layer, blk(i, j), 0))],
        out_specs=(pl.BlockSpec((tm, tn), lambda i, j: (i, j)),
                   pl.BlockSpec((rb, nd), lambda i, j: (blk(i, j), 0))),
        compiler_params=_params(("arbitrary", "arbitrary")),
        name="ffn_up",
    )(x, wg, wu, wd)


FFN_DOWN_COL_CAP = 256


def _lru_kernel(xa_ref, ga_ref, h0_ref, buf_ref, cw_ref, cb_ref, wa_ref, ba_ref, wx_ref, bx_ref, lam_ref,
                y_ref, ht_ref, nbuf_ref):
    t_len = xa_ref.shape[0]
    x = xa_ref[...]
    buf = buf_ref[...]
    cw = cw_ref[...]
    row = lax.broadcasted_iota(jnp.int32, x.shape, 0)
    nb = A_CONV - 1
    xc = x * cw[nb:nb + 1]
    for s in range(1, A_CONV):
        xs = pltpu.roll(x, s, 0)
        for r in range(s):
            xs = jnp.where(row == r, buf[nb + r - s:nb + r - s + 1], xs)
        xc = xc + xs * cw[nb - s:nb - s + 1]
    xc = xc + cb_ref[...]
    xcb = xc.astype(BF16)
    r_g = jax.nn.sigmoid(_dot(xcb, wa_ref[...].astype(BF16)) + ba_ref[...])
    i_g = jax.nn.sigmoid(_dot(xcb, wx_ref[...].astype(BF16)) + bx_ref[...])
    nl = -lam_ref[...]
    softplus = jnp.maximum(nl, 0.0) + jnp.log1p(jnp.exp(-jnp.abs(nl)))
    log_a = -LRU_C * r_g * softplus
    a = jnp.exp(log_a)
    u = jnp.sqrt(-jnp.tanh(log_a) * (a * a + 1.0)) * i_g * xc
    d = 1
    while d < t_len:
        if d % 8:
            keep = row >= d
            a_prev = pltpu.roll(a, d, 0)
            u_prev = pltpu.roll(u, d, 0)
            u = jnp.where(keep, a * u_prev + u, u)
            a = jnp.where(keep, a * a_prev, a)
        else:
            a_hi = a[d:]
            u = jnp.concatenate([u[:d], a_hi * u[:t_len - d] + u[d:]], axis=0)
            a = jnp.concatenate([a[:d], a_hi * a[:t_len - d]], axis=0)
        d *= 2
    h = a * h0_ref[...] + u
    y_ref[...] = (h * _gelu_tanh(ga_ref[...])).astype(y_ref.dtype)
    ht_ref[...] = h[t_len - 1:t_len]
    nbuf_ref[...] = x[t_len - nb:t_len]


def _into_zeros(kernel_fn, n_in, out_rows, width, dtype):
    def wrapped(*refs):
        return kernel_fn(*refs[:n_in], *refs[n_in + 1:])
    return wrapped, pl.BlockSpec(memory_space=pl.ANY), jnp.zeros((out_rows, width), dtype), {n_in: 0}


def _lru_mixer(z, row0, n_seq, t_len, h0, buf, conv_w, conv_b, wa, ba, wx, bx, lam, out_dtype, out_rows=None):
    assert row0 % t_len == 0 and t_len >= A_CONV - 1 and t_len % 8 == 0
    rb0 = row0 // t_len
    nb = A_CONV - 1
    out_rows = out_rows or n_seq * t_len
    vec = lambda v: v.reshape(1, D_A)
    cspec = pl.BlockSpec((1, A_BLK), lambda s, h: (0, h))
    wspec = pl.BlockSpec((None, A_BLK, A_BLK), lambda s, h: (h, 0, 0))
    body, base_spec, base, alias = _into_zeros(_lru_kernel, 11, out_rows, D_A, out_dtype)
    return pl.pallas_call(
        body,
        out_shape=(jax.ShapeDtypeStruct((out_rows, D_A), out_dtype),
                   jax.ShapeDtypeStruct((n_seq, 1, D_A), F32),
                   jax.ShapeDtypeStruct((n_seq, nb, D_A), F32)),
        grid=(n_seq, A_HEADS),
        in_specs=[pl.BlockSpec((t_len, A_BLK), lambda s, h: (rb0 + s, COL_XA + h)),
                  pl.BlockSpec((t_len, A_BLK), lambda s, h: (rb0 + s, COL_GA + h)),
                  pl.BlockSpec((None, 1, A_BLK), lambda s, h: (s, 0, h)),
                  pl.BlockSpec((None, nb, A_BLK), lambda s, h: (s, 0, h)),
                  pl.BlockSpec((A_CONV, A_BLK), lambda s, h: (0, h)),
                  cspec, wspec, cspec, wspec, cspec, cspec, base_spec],
        out_specs=(pl.BlockSpec((t_len, A_BLK), lambda s, h: (s, h)),
                   pl.BlockSpec((None, 1, A_BLK), lambda s, h: (s, 0, h)),
                   pl.BlockSpec((None, nb, A_BLK), lambda s, h: (s, 0, h))),
        input_output_aliases=alias,
        compiler_params=_params(("parallel", "parallel")),
        name="rglru_mixer",
    )(z, z, h0.reshape(n_seq, 1, D_A), buf, conv_w, vec(conv_b), wa, vec(ba), wx, vec(bx), vec(lam), base)


def _compress_kernel(x_ref, pe_ref, w1_ref, b1_ref, w2_ref, o_ref):
    n_sub = o_ref.shape[0]
    pe = pe_ref[...]
    y0 = jnp.zeros((n_sub, HEAD_DIM), F32)
    y1 = jnp.zeros((n_sub, HEAD_DIM), F32)
    for j in range(CMP_STRIDE):
        xj = x_ref[pl.ds(j, n_sub, stride=CMP_STRIDE), :]
        y0 = y0 + _dot((xj + pe[j:j + 1]).astype(BF16), w1_ref[j])
        y1 = y1 + _dot((xj + pe[CMP_STRIDE + j:CMP_STRIDE + j + 1]).astype(BF16), w1_ref[CMP_STRIDE + j])
    hid = y0 + pltpu.roll(y1, n_sub - 1, 0)
    hid = _gelu_tanh(hid + b1_ref[...])
    o_ref[...] = _dot(hid.astype(BF16), w2_ref[...]).astype(o_ref.dtype)


def _compress(x2d, x_index, n_seq, t_len, pe, w1, b1, w2):
    assert SUB_PER_CMP == 2 and t_len % (8 * CMP_STRIDE) == 0
    n_sub = t_len // CMP_STRIDE
    nt = 2 * KV_HEADS
    return pl.pallas_call(
        _compress_kernel,
        out_shape=jax.ShapeDtypeStruct((n_seq, nt, n_sub, HEAD_DIM), BF16),
        grid=(n_seq, nt),
        in_specs=[pl.BlockSpec((t_len, HEAD_DIM), x_index),
                  pl.BlockSpec((None, CMP_LEN, HEAD_DIM), lambda s, c: (c // KV_HEADS, 0, 0)),
                  pl.BlockSpec((None, CMP_LEN, HEAD_DIM, HEAD_DIM), lambda s, c: (c // KV_HEADS, 0, 0, 0)),
                  pl.BlockSpec((None, 1, HEAD_DIM), lambda s, c: (c // KV_HEADS, 0, 0)),
                  pl.BlockSpec((None, HEAD_DIM, HEAD_DIM), lambda s, c: (c // KV_HEADS, 0, 0))],
        out_specs=pl.BlockSpec((None, None, n_sub, HEAD_DIM), lambda s, c: (s, c, 0, 0)),
        compiler_params=_params(("parallel", "arbitrary")),
        name="nsa_compress",
    )(x2d, pe, w1.astype(BF16), b1.reshape(2, 1, HEAD_DIM), w2.astype(BF16))


def _head_rows(ref, c, n_tok):
    return ref[pl.ds(c, n_tok, stride=2 * KV_HEADS), :]


def _page_gather_kernel(pt_ref, *refs):
    page_refs, o_ref = refs[:-1], refs[-1]
    for k, c_ref in enumerate(page_refs):
        for c in range(2 * KV_HEADS):
            o_ref[c, k * PAGE_SIZE:(k + 1) * PAGE_SIZE, :] = _head_rows(c_ref, c, PAGE_SIZE)


def _pages_per_step(n_pages, want=4):
    pps = want
    while n_pages % pps:
        pps //= 2
    return pps


def _page_gather(cache_rows, layer, page_table):
    nb, n_pages = page_table.shape
    nt = 2 * KV_HEADS
    pps = _pages_per_step(n_pages, 16)

    def page_spec(k):
        return pl.BlockSpec((None, None, PAGE_SIZE * nt, HEAD_DIM), lambda b, p, pt: (layer, pt[b, p * pps + k], 0, 0))

    return pl.pallas_call(
        _page_gather_kernel,
        out_shape=jax.ShapeDtypeStruct((nb, nt, n_pages * PAGE_SIZE, HEAD_DIM), cache_rows.dtype),
        grid_spec=pltpu.PrefetchScalarGridSpec(
            num_scalar_prefetch=1, grid=(nb, n_pages // pps),
            in_specs=[page_spec(k) for k in range(pps)],
            out_specs=pl.BlockSpec((None, nt, pps * PAGE_SIZE, HEAD_DIM), lambda b, p, pt: (b, 0, p, 0))),
        compiler_params=_params(("parallel", "arbitrary")),
        name="page_gather",
    )(page_table, *([cache_rows] * pps))


def _stack_heads(q, g0):
    return jnp.concatenate([q[:, (g0 + p) * HEAD_DIM:(g0 + p + 1) * HEAD_DIM] for p in range(HPG)], axis=0).astype(BF16)


def _softmax(sb):
    m = jnp.max(sb, axis=-1, keepdims=True)
    e = jnp.exp(sb - m)
    return e / jnp.sum(e, axis=-1, keepdims=True)


def _cmp_branch(q4, kc, vc, t_rows, slope, n_cmp):
    ncp = kc.shape[0]
    s = _dot_nt(q4, kc) * QK_SCALE
    col = lax.broadcasted_iota(jnp.int32, s.shape, 1)
    dist_i = t_rows - (col * CMP_STRIDE + (CMP_LEN - 1))
    pick = lambda x, other: jnp.where(dist_i >= 0, jnp.where(col < n_cmp, x, other), other)
    p = _softmax(pick(s - _lanes(slope, ncp) * dist_i.astype(F32), NEG_INF))
    p = pick(p, 0.0)
    return _dot(p.astype(BF16), vc), p


def _select_blocks(imp, pool, t_q, n_sel):
    imp_sel = _dot_f32_exact(imp, pool)
    jcol = lax.broadcasted_iota(jnp.int32, imp_sel.shape, 1)
    cur = lax.shift_right_logical(t_q, int(np.log2(SEL_BLOCK)))
    score = jnp.where(jcol * SEL_BLOCK <= t_q, imp_sel, -1.0)
    for forced_blk in (cur - 1, cur, 0):
        score = jnp.where(jcol == forced_blk, FORCE_SCORE, score)
    score = jnp.where(jcol < n_sel, score, -2.0)
    rank = jnp.zeros(score.shape, F32)
    for i in range(n_sel):
        ci = score[:, i:i + 1]
        tie = jnp.where(jcol > i, 1.0, 0.0)
        rank = rank + jnp.where(ci > score, 1.0, jnp.where(ci == score, tie, 0.0))
    top = min(SEL_TOP, n_sel)
    return jnp.where(rank < float(top), 1.0, 0.0)


def _pool_matrix(n_cmp, ncp, n_sel, nsp):
    assert n_cmp + SUB_PER_CMP - 1 <= n_sel * SUB_PER_SEL
    m = np.zeros((ncp, nsp), np.float32)
    for c in range(n_cmp):
        for n in range(SUB_PER_CMP):
            m[c, (c + n) // SUB_PER_SEL] += 1.0 / SUB_PER_CMP
    return jnp.asarray(m, BF16)


def _slope_rows(t_rep):
    h = np.arange(1, B_HEADS + 1, dtype=np.float64)
    s = np.exp2(-8.0 * h / B_HEADS).astype(np.float32).reshape(KV_HEADS, HPG, 1, 1)
    return jnp.asarray(np.broadcast_to(s, (KV_HEADS, HPG, t_rep, LANE)).reshape(KV_HEADS, HPG * t_rep, LANE))


def _expand_matrix(shape, blk0):
    j = lax.broadcasted_iota(jnp.int32, shape, 0)
    k = lax.broadcasted_iota(jnp.int32, shape, 1)
    return jnp.where(j == blk0 + lax.shift_right_logical(k, int(np.log2(SEL_BLOCK))), 1.0, 0.0).astype(BF16)


def _gate_combine(gs, p, tq, o_c, o_s, o_w):
    rows = slice(p * tq, (p + 1) * tq)
    return (gs[:, p:p + 1] * o_c[rows] + gs[:, HPG + p:HPG + p + 1] * o_s[rows]
            + gs[:, 2 * HPG + p:2 * HPG + p + 1] * o_w[rows])


def _attn_prompt_kernel(q_ref, kc_ref, vc_ref, ks_ref, vs_ref, kw_ref, vw_ref, gt_ref, slope_ref, pool_ref,
                        o_ref, ksb, vsb, kwb, vwb, *, n_cmp, n_sel):
    qi = pl.program_id(2)
    tq = q_ref.shape[0]
    t_total = ks_ref.shape[0]
    rr = HPG * tq

    @pl.when(qi == 0)
    def _():
        ksb[...] = ks_ref[...].astype(BF16)
        vsb[...] = vs_ref[...].astype(BF16)
        kwb[...] = kw_ref[...].astype(BF16)
        vwb[...] = vw_ref[...].astype(BF16)

    t0 = qi * tq
    q4 = _stack_heads(q_ref[...], 0)
    slope = slope_ref[...]

    def t_rows(width):
        r = lax.broadcasted_iota(jnp.int32, (rr, width), 0)
        return t0 + (r & (tq - 1))

    ncp = kc_ref.shape[0]
    o_c, p_c = _cmp_branch(q4, kc_ref[...], vc_ref[...], t_rows(ncp), slope, n_cmp)
    imp = p_c[0:tq]
    for p in range(1, HPG):
        imp = imp + p_c[p * tq:(p + 1) * tq]
    nsp = pool_ref.shape[1]
    t_q = t0 + lax.broadcasted_iota(jnp.int32, (tq, nsp), 0)
    sel = _select_blocks(imp, pool_ref[...], t_q, n_sel).astype(BF16)
    sel4 = jnp.concatenate([sel] * HPG, axis=0)

    ck = ATT_CK
    tr_ck = t_rows(ck)
    slope_ck = _lanes(slope, ck)
    col_ck = lax.broadcasted_iota(jnp.int32, (rr, ck), 1)

    def sel_chunk(c, carry):
        m, l, acc = carry
        k0 = pl.multiple_of(c * ck, ck)
        s = _dot_nt(q4, ksb[pl.ds(k0, ck), :]) * QK_SCALE
        dist_i = tr_ck - (k0 + col_ck)
        selx = _dot(sel4, _expand_matrix((nsp, ck), c * (ck // SEL_BLOCK)))
        sb = jnp.where(dist_i >= 0, jnp.where(selx > 0.5, s - slope_ck * dist_i.astype(F32), NEG_INF), NEG_INF)
        m_new = jnp.maximum(m, jnp.max(sb, axis=-1, keepdims=True))
        alpha = jnp.exp(m - m_new)
        e = jnp.exp(sb - m_new)
        l = alpha * l + jnp.sum(e, axis=-1, keepdims=True)
        acc = alpha * acc + _dot(e.astype(BF16), vsb[pl.ds(k0, ck), :])
        return m_new, l, acc

    n_chunks = (t0 + tq + ck - 1) // ck
    m0 = jnp.full((rr, 1), NEG_INF, F32)
    _, l_s, acc_s = lax.fori_loop(0, n_chunks, sel_chunk, (m0, jnp.zeros((rr, 1), F32), jnp.zeros((rr, HEAD_DIM), F32)))
    o_s = acc_s / l_s

    wk = WINDOW + tq
    w0 = pl.multiple_of(jnp.maximum(t0 - WINDOW, 0), tq)
    s = _dot_nt(q4, kwb[pl.ds(w0, wk), :]) * QK_SCALE
    dist_i = t_rows(wk) - (w0 + lax.broadcasted_iota(jnp.int32, (rr, wk), 1))
    logit = s - _lanes(slope, wk) * dist_i.astype(F32)
    p_w = _softmax(jnp.where(dist_i >= 0, jnp.where(dist_i < WINDOW, logit, NEG_INF), NEG_INF))
    o_w = _dot(p_w.astype(BF16), vwb[pl.ds(w0, wk), :])

    gs = jax.nn.sigmoid(gt_ref[...])
    for p in range(HPG):
        o_ref[:, p * HEAD_DIM:(p + 1) * HEAD_DIM] = _gate_combine(gs, p, tq, o_c, o_s, o_w).astype(o_ref.dtype)


def _attn_prompt(z, gt, kvc_cmp, n_seq, t_len, out_rows=None):
    tq = ATT_TQ
    assert t_len % ATT_CK == 0 and t_len >= WINDOW + tq and tq & (tq - 1) == 0
    n_cmp = (t_len - CMP_LEN) // CMP_STRIDE + 1
    n_sel = -(-t_len // SEL_BLOCK)
    ncp = kvc_cmp.shape[2]
    nsp = _round_up(n_sel, LANE)
    nq = t_len // tq
    rr = HPG * tq
    kvspec = lambda col0: pl.BlockSpec((t_len, HEAD_DIM), lambda b, g, i: (b, col0 + g))
    out_rows = out_rows or n_seq * t_len
    body, base_spec, base, alias = _into_zeros(functools.partial(_attn_prompt_kernel, n_cmp=n_cmp, n_sel=n_sel),
                                               10, out_rows, D_B, BF16)
    return pl.pallas_call(
        body,
        out_shape=jax.ShapeDtypeStruct((out_rows, D_B), BF16),
        grid=(n_seq, KV_HEADS, nq),
        in_specs=[pl.BlockSpec((tq, HPG * HEAD_DIM), lambda b, g, i: (b * nq + i, COL_Q // HPG + g)),
                  pl.BlockSpec((None, None, ncp, HEAD_DIM), lambda b, g, i: (b, g, 0, 0)),
                  pl.BlockSpec((None, None, ncp, HEAD_DIM), lambda b, g, i: (b, KV_HEADS + g, 0, 0)),
                  kvspec(COL_KVS), kvspec(COL_KVS + KV_HEADS), kvspec(COL_KVW), kvspec(COL_KVW + KV_HEADS),
                  pl.BlockSpec((tq, LANE), lambda b, g, i: (b * nq + i, g)),
                  pl.BlockSpec((None, rr, LANE), lambda b, g, i: (g, 0, 0)),
                  pl.BlockSpec((ncp, nsp), lambda b, g, i: (0, 0)), base_spec],
        out_specs=pl.BlockSpec((tq, HPG * HEAD_DIM), lambda b, g, i: (b * nq + i, g)),
        scratch_shapes=[pltpu.VMEM((t_len, HEAD_DIM), BF16)] * 4,
        input_output_aliases=alias,
        compiler_params=_params(("parallel", "parallel", "arbitrary")),
        name="nsa_prompt_attention",
    )(z, kvc_cmp, kvc_cmp, z, z, z, z, gt, _slope_rows(tq), _pool_matrix(n_cmp, ncp, n_sel, nsp), base)


def _attn_dec_select_kernel(q_ref, kc_ref, vc_ref, slope_ref, pool_ref, oc_ref, sel_ref, *, n_cmp, n_sel, past_len):
    ts = q_ref.shape[0]
    rr = HPG * ts
    ncp = kc_ref.shape[0]
    q4 = _stack_heads(q_ref[...], 0)
    t_rows = past_len + (lax.broadcasted_iota(jnp.int32, (rr, ncp), 0) & (ts - 1))
    o_c, p_c = _cmp_branch(q4, kc_ref[...], vc_ref[...], t_rows, slope_ref[...], n_cmp)
    imp = p_c[0:ts]
    for p in range(1, HPG):
        imp = imp + p_c[p * ts:(p + 1) * ts]
    nsp = pool_ref.shape[1]
    t_q = past_len + lax.broadcasted_iota(jnp.int32, (ts, nsp), 0)
    oc_ref[...] = o_c
    sel_ref[...] = _select_blocks(imp, pool_ref[...], t_q, n_sel)


def _attn_dec_select(z, row0, kvc_cmp, n_seq, ts, past_len, n_cmp, n_sel):
    assert row0 % ts == 0 and ts & (ts - 1) == 0 and ts % 8 == 0
    rb0 = row0 // ts
    ncp = kvc_cmp.shape[2]
    nsp = _round_up(n_sel, LANE)
    rr = HPG * ts
    return pl.pallas_call(
        functools.partial(_attn_dec_select_kernel, n_cmp=n_cmp, n_sel=n_sel, past_len=past_len),
        out_shape=(jax.ShapeDtypeStruct((n_seq, KV_HEADS, rr, HEAD_DIM), F32),
                   jax.ShapeDtypeStruct((n_seq, KV_HEADS, ts, nsp), F32)),
        grid=(n_seq, KV_HEADS),
        in_specs=[pl.BlockSpec((ts, HPG * HEAD_DIM), lambda b, g: (rb0 + b, COL_Q // HPG + g)),
                  pl.BlockSpec((None, None, ncp, HEAD_DIM), lambda b, g: (b, g, 0, 0)),
                  pl.BlockSpec((None, None, ncp, HEAD_DIM), lambda b, g: (b, KV_HEADS + g, 0, 0)),
                  pl.BlockSpec((None, rr, LANE), lambda b, g: (g, 0, 0)),
                  pl.BlockSpec((ncp, nsp), lambda b, g: (0, 0))],
        out_specs=(pl.BlockSpec((None, None, rr, HEAD_DIM), lambda b, g: (b, g, 0, 0)),
                   pl.BlockSpec((None, None, ts, nsp), lambda b, g: (b, g, 0, 0))),
        compiler_params=_params(("parallel", "parallel")),
        name="nsa_decode_select",
    )(z, kvc_cmp, kvc_cmp, _slope_rows(ts), _pool_matrix(n_cmp, ncp, n_sel, nsp))


def _attn_dec_kernel(pt_ref, q_ref, sel_ref, *rest, past_len, n_steps, pages_per_step):
    page_refs = rest[:pages_per_step]
    kvs_ref, kvw_ref, win_ref, oc_ref, gt_ref, slope_ref, o_ref, nwin_ref, m_sc, l_sc, acc_sc = rest[pages_per_step:]
    pg = pl.program_id(1)
    ts = q_ref.shape[0]
    rr = HPG * ts
    nsp = sel_ref.shape[-1]
    pad_rows = LANE - ts
    row = lax.broadcasted_iota(jnp.int32, (rr, LANE), 0)
    col = lax.broadcasted_iota(jnp.int32, (rr, LANE), 1)
    t_rows = past_len + (row & (ts - 1))

    @pl.when(pg == 0)
    def _():
        m_sc[...] = jnp.full(m_sc.shape, NEG_INF, F32)
        l_sc[...] = jnp.zeros(l_sc.shape, F32)
        acc_sc[...] = jnp.zeros(acc_sc.shape, F32)

    def sel_update(g, q4, kk, vv, dist_i, expand, real_key):
        width = kk.shape[0]
        s = _dot_nt(q4, kk) * QK_SCALE
        sel4 = jnp.concatenate([sel_ref[g].astype(BF16)] * HPG, axis=0)
        sb = s - _lanes(slope_ref[g], width) * dist_i.astype(F32)
        sb = jnp.where(dist_i >= 0, jnp.where(_dot(sel4, expand) > 0.5, sb, NEG_INF), NEG_INF)
        if real_key is not None:
            sb = jnp.where(real_key, sb, NEG_INF)
        m_old = m_sc[g]
        m_new = jnp.maximum(m_old, jnp.max(sb, axis=-1, keepdims=True))
        alpha = jnp.exp(m_old - m_new)
        e = jnp.exp(sb - m_new)
        l_sc[g] = alpha * l_sc[g] + jnp.sum(e, axis=-1, keepdims=True)
        acc_sc[g] = alpha * acc_sc[g] + _dot(e.astype(BF16), vv)
        m_sc[g] = m_new

    @pl.when(pg < n_steps)
    def _():
        width = pages_per_step * PAGE_SIZE
        k0 = pg * width
        expand = _expand_matrix((nsp, width), pg * (width // SEL_BLOCK))
        wrow_ = lax.broadcasted_iota(jnp.int32, (rr, width), 0)
        wcol_ = lax.broadcasted_iota(jnp.int32, (rr, width), 1)
        dist_i = (past_len + (wrow_ & (ts - 1))) - (k0 + wcol_)
        for g in range(KV_HEADS):
            q4 = _stack_heads(q_ref[...], g * HPG)
            kk = jnp.concatenate([_head_rows(r, g, PAGE_SIZE) for r in page_refs], axis=0).astype(BF16)
            vv = jnp.concatenate([_head_rows(r, KV_HEADS + g, PAGE_SIZE) for r in page_refs], axis=0).astype(BF16)
            sel_update(g, q4, kk, vv, dist_i, expand, None)

    @pl.when(pg == n_steps)
    def _():
        zpad = jnp.zeros((pad_rows, HEAD_DIM), F32)
        gs = jax.nn.sigmoid(gt_ref[...])
        jj = lax.broadcasted_iota(jnp.int32, (nsp, LANE), 0)
        kk_i = lax.broadcasted_iota(jnp.int32, (nsp, LANE), 1)
        new_blk = lax.shift_right_logical(past_len + kk_i, int(np.log2(SEL_BLOCK)))
        expand_new = jnp.where(jj == new_blk, 1.0, 0.0).astype(BF16)
        is_new = col < ts
        nt = 2 * KV_HEADS
        wlen = win_ref.shape[0] // nt
        wrow = lax.broadcasted_iota(jnp.int32, (rr, wlen), 0)
        wcol = lax.broadcasted_iota(jnp.int32, (rr, wlen), 1)
        dist_w = (past_len + (wrow & (ts - 1))) - (past_len - wlen + wcol)
        ok_w = (dist_w >= 0) & (dist_w < WINDOW)
        dist_n = t_rows - (past_len + col)
        ok_n = (dist_n >= 0) & (dist_n < WINDOW) & is_new
        for g in range(KV_HEADS):
            q4 = _stack_heads(q_ref[...], g * HPG)
            ksl = slice(g * HEAD_DIM, (g + 1) * HEAD_DIM)
            vsl = slice(D_KV + g * HEAD_DIM, D_KV + (g + 1) * HEAD_DIM)
            kn = jnp.concatenate([kvs_ref[:, ksl], zpad], axis=0).astype(BF16)
            vn = jnp.concatenate([kvs_ref[:, vsl], zpad], axis=0).astype(BF16)
            sel_update(g, q4, kn, vn, dist_n, expand_new, is_new)
            o_s = acc_sc[g] / l_sc[g]
            slope = slope_ref[g]
            s1 = _dot_nt(q4, _head_rows(win_ref, g, wlen).astype(BF16)) * QK_SCALE
            s1 = jnp.where(ok_w, s1 - _lanes(slope, wlen) * dist_w.astype(F32), NEG_INF)
            kwn = jnp.concatenate([kvw_ref[:, ksl], zpad], axis=0).astype(BF16)
            vwn = jnp.concatenate([kvw_ref[:, vsl], zpad], axis=0).astype(BF16)
            s2 = _dot_nt(q4, kwn) * QK_SCALE
            s2 = jnp.where(ok_n, s2 - slope * dist_n.astype(F32), NEG_INF)
            m = jnp.maximum(jnp.max(s1, axis=-1, keepdims=True), jnp.max(s2, axis=-1, keepdims=True))
            e1 = jnp.exp(s1 - m)
            e2 = jnp.exp(s2 - m)
            inv = 1.0 / (jnp.sum(e1, axis=-1, keepdims=True) + jnp.sum(e2, axis=-1, keepdims=True))
            o_w = (_dot((e1 * inv).astype(BF16), _head_rows(win_ref, KV_HEADS + g, wlen).astype(BF16))
                   + _dot((e2 * inv).astype(BF16), vwn))
            o_c = oc_ref[g]
            for p in range(HPG):
                c0 = (g * HPG + p) * HEAD_DIM
                o_ref[:, c0:c0 + HEAD_DIM] = _gate_combine(gs[:, g * LANE:(g + 1) * LANE], p, ts, o_c, o_s, o_w)
        keep = (wlen - ts) * nt
        nwin_ref[0:keep, :] = win_ref[ts * nt:wlen * nt, :]
        for c in range(nt):
            nwin_ref[pl.ds(keep + c, ts, stride=nt), :] = kvw_ref[:, c * HEAD_DIM:(c + 1) * HEAD_DIM]


def _attn_decode(z, gt, row0, n_seq, ts, page_table, cache_rows, layer, win_rows, o_c, sel, past_len):
    n_pages = page_table.shape[1]
    pps = _pages_per_step(n_pages, 8)
    n_steps = n_pages // pps
    rb0 = row0 // ts
    rr = HPG * ts
    nsp = sel.shape[-1]
    nt = 2 * KV_HEADS
    wrows = win_rows.shape[2]
    w = 2 * D_KV
    assert wrows == WINDOW * nt and past_len % SEL_BLOCK == 0 and PAGE_SIZE == LANE
    last = n_steps - 1

    def page_spec(k):
        return pl.BlockSpec((None, None, PAGE_SIZE * nt, HEAD_DIM),
                            lambda b, p, pt: (layer, pt[b, jnp.minimum(p, last) * pps + k], 0, 0))

    return pl.pallas_call(
        functools.partial(_attn_dec_kernel, past_len=past_len, n_steps=n_steps, pages_per_step=pps),
        out_shape=(jax.ShapeDtypeStruct((n_seq * ts, D_B), F32),
                   jax.ShapeDtypeStruct((n_seq, wrows, HEAD_DIM), F32)),
        grid_spec=pltpu.PrefetchScalarGridSpec(
            num_scalar_prefetch=1, grid=(n_seq, n_steps + 1),
            in_specs=[pl.BlockSpec((ts, D_B), lambda b, p, pt: (rb0 + b, COL_Q * LANE // D_B)),
                      pl.BlockSpec((None, KV_HEADS, ts, nsp), lambda b, p, pt: (b, 0, 0, 0))]
                     + [page_spec(k) for k in range(pps)]
                     + [pl.BlockSpec((ts, w), lambda b, p, pt: (rb0 + b, COL_KVS * LANE // w)),
                        pl.BlockSpec((ts, w), lambda b, p, pt: (rb0 + b, COL_KVW * LANE // w)),
                        pl.BlockSpec((None, None, wrows, HEAD_DIM), lambda b, p, pt: (layer, b, 0, 0)),
                        pl.BlockSpec((None, KV_HEADS, rr, HEAD_DIM), lambda b, p, pt: (b, 0, 0, 0)),
                        pl.BlockSpec((ts, KV_HEADS * LANE), lambda b, p, pt: (rb0 + b, 0)),
                        pl.BlockSpec((KV_HEADS, rr, LANE), lambda b, p, pt: (0, 0, 0))],
            out_specs=(pl.BlockSpec((ts, D_B), lambda b, p, pt: (b, 0)),
                       pl.BlockSpec((None, wrows, HEAD_DIM), lambda b, p, pt: (b, 0, 0))),
            scratch_shapes=[pltpu.VMEM((KV_HEADS, rr, 1), F32), pltpu.VMEM((KV_HEADS, rr, 1), F32),
                            pltpu.VMEM((KV_HEADS, rr, HEAD_DIM), F32)]),
        compiler_params=_params(("parallel", "arbitrary")),
        name="nsa_decode_attention",
    )(page_table, z, sel, *([cache_rows] * pps), z, z, win_rows, o_c, gt, _slope_rows(ts))


def _sconv_kernel(bg_ref, cg_ref, v_ref, buf_ref, cw_ref, y_ref, nbuf_ref):
    t_len = cg_ref.shape[0]
    nb = C_CONV - 1
    cv = cg_ref[...] * v_ref[...]
    buf = buf_ref[...]
    cw = cw_ref[...]
    row = lax.broadcasted_iota(jnp.int32, cv.shape, 0)
    y = cv * cw[nb:nb + 1]
    for s in range(1, C_CONV):
        xs = pltpu.roll(cv, s, 0)
        for r in range(s):
            xs = jnp.where(row == r, buf[nb + r - s:nb + r - s + 1], xs)
        y = y + xs * cw[nb - s:nb - s + 1]
    y_ref[...] = (bg_ref[...] * y).astype(y_ref.dtype)
    nbuf_ref[...] = cv[t_len - nb:t_len]


def _sconv_mixer(zc, row0, n_seq, t_len, buf, conv_w, out_dtype, out_rows=None):
    assert row0 % t_len == 0 and t_len >= C_CONV - 1 and t_len % 8 == 0
    rb0 = row0 // t_len
    nb = C_CONV - 1
    tc = 256 if t_len > 256 else 2048
    nc = D_C // tc
    zspec = lambda third: pl.BlockSpec((t_len, tc), lambda s, c: (rb0 + s, third * nc + c))
    out_rows = out_rows or n_seq * t_len
    body, base_spec, base, alias = _into_zeros(_sconv_kernel, 5, out_rows, D_C, out_dtype)
    return pl.pallas_call(
        body,
        out_shape=(jax.ShapeDtypeStruct((out_rows, D_C), out_dtype),
                   jax.ShapeDtypeStruct((n_seq, nb, D_C), F32)),
        grid=(n_seq, nc),
        in_specs=[zspec(0), zspec(1), zspec(2),
                  pl.BlockSpec((None, nb, tc), lambda s, c: (s, 0, c)),
                  pl.BlockSpec((C_CONV, tc), lambda s, c: (0, c)), base_spec],
        out_specs=(pl.BlockSpec((t_len, tc), lambda s, c: (s, c)),
                   pl.BlockSpec((None, nb, tc), lambda s, c: (s, 0, c))),
        input_output_aliases=alias,
        compiler_params=_params(("parallel", "parallel")),
        name="sconv_mixer",
    )(zc, zc, zc, buf, conv_w, base)


def _gate_weight(w_in):
    nl, k = w_in.shape[:2]
    wg = w_in[:, :, D_IN_MAIN:].reshape(nl, k, 3, KV_HEADS, HPG).transpose(0, 1, 3, 2, 4)
    wg = jnp.pad(wg.reshape(nl, k, KV_HEADS, 3 * HPG), ((0, 0), (0, 0), (0, 0), (0, LANE - 3 * HPG)))
    return wg.reshape(nl, k, KV_HEADS * LANE).astype(BF16)


def _ab_layer(h, hn, i, mp, bp, t_len, bs, ts, past_len, w, cache_kv_cmp, cache_kv_sel, state_win_kv, state_lru_h,
              state_lru_conv, page_table):
    z, kv_rows = _in_proj(hn, w["ab_w_in_bf16"], i)
    gt = _matmul(hn, w["ab_w_gate_bf16"], i, name="ab_gate_proj")
    nt = 2 * KV_HEADS
    rows_view = lambda a: a.reshape(a.shape[0], a.shape[1], -1, HEAD_DIM)
    lru_w = (w["ab_conv_w"][i], w["ab_conv_b"][i], w["ab_gate_a_w"][i], w["ab_gate_a_b"][i], w["ab_gate_x_w"][i],
             w["ab_gate_x_b"][i], w["ab_lru_lambda"][i])
    cmp_w = (w["ab_cmp_pe"][i], w["ab_cmp_w1"][i], w["ab_cmp_b1"][i], w["ab_cmp_w2"][i])
    n_pages = page_table.shape[1]
    tk = past_len + ts
    n_cmp_s = (tk - CMP_LEN) // CMP_STRIDE + 1
    n_sel_s = -(-tk // SEL_BLOCK)
    assert n_cmp_s == (past_len - CMP_LEN) // CMP_STRIDE + 1

    ya_p, hT_p, nbuf_p = _lru_mixer(z, 0, bp, t_len, jnp.zeros((bp, D_A), F32), jnp.zeros((bp, A_CONV - 1, D_A), F32),
                                    *lru_w, out_dtype=BF16, out_rows=mp + bs * ts)
    kvc_p = _compress(z, lambda s, c: (s, COL_KVC + c), bp, t_len, *cmp_w)
    ob_p = _attn_prompt(z, gt, kvc_p, bp, t_len, out_rows=mp + bs * ts)

    ya_s, hT_s, nbuf_s = _lru_mixer(z, mp, bs, ts, state_lru_h[i], state_lru_conv[i], *lru_w, out_dtype=F32)
    past_cmp = _page_gather(rows_view(cache_kv_cmp), i, page_table)
    kvc_s = _compress(past_cmp.reshape(bs * nt * past_len, HEAD_DIM), lambda s, c: (s * nt + c, 0), bs, past_len, *cmp_w)
    oc_s, sel_s = _attn_dec_select(z, mp, kvc_s, bs, ts, past_len, n_cmp_s, n_sel_s)
    ob_s, nwin_s = _attn_decode(z, gt, mp, bs, ts, page_table, rows_view(cache_kv_sel), i, rows_view(state_win_kv),
                                oc_s, sel_s, past_len)

    ya = lax.dynamic_update_slice(ya_p, ya_s.astype(BF16), (mp, 0))
    ob = lax.dynamic_update_slice(ob_p, ob_s.astype(BF16), (mp, 0))
    h = _matmul2_residual(ya, ob, w["ab_w_out_bf16"], i, h, name="ab_out_proj")

    kv5 = lambda a, nb_, tt: a.reshape(nb_, tt, 2, KV_HEADS, HEAD_DIM)
    wl = min(WINDOW, t_len)
    kvw_p = kv5(kv_rows[2, :mp * nt], bp, t_len)
    outs_p = (kv5(kv_rows[0, :mp * nt], bp, t_len), kv5(kv_rows[1, :mp * nt], bp, t_len),
              kvw_p[:, t_len - wl:], hT_p.reshape(bp, D_A), nbuf_p)
    outs_s = (kv5(kv_rows[0, mp * nt:], bs, ts), kv5(kv_rows[1, mp * nt:], bs, ts),
              kv5(nwin_s, bs, WINDOW), hT_s.reshape(bs, D_A), nbuf_s)
    return h, outs_p, outs_s


def _c_layer(h, hn, i, mp, bp, t_len, bs, ts, w, state_sconv):
    zc = _matmul(hn, w["c_w_in"], i, name="c_in_proj", row_cap=WIDE_ROW_CAP, col_cap=F32_WEIGHT_COL_CAP)
    y_p, nbuf_p = _sconv_mixer(zc, 0, bp, t_len, jnp.zeros((bp, C_CONV - 1, D_C), F32), w["c_conv_w"][i], BF16,
                               out_rows=mp + bs * ts)
    y_s, nbuf_s = _sconv_mixer(zc, mp, bs, ts, state_sconv[i], w["c_conv_w"][i], F32)
    y = lax.dynamic_update_slice(y_p, y_s.astype(BF16), (mp, 0))
    h = _matmul_residual(y, w["c_w_out_bf16"], i, h, name="c_out_proj", row_cap=WIDE_ROW_CAP)
    return h, nbuf_p, nbuf_s


def kernel(x_prompt, x_sample, cache_kv_cmp, cache_kv_sel, state_win_kv, state_lru_h, state_lru_conv, state_sconv,
           page_table, norm_mix, norm_ffn, norm_final, ab_w_in, ab_conv_w, ab_conv_b, ab_gate_a_w, ab_gate_a_b,
           ab_gate_x_w, ab_gate_x_b, ab_lru_lambda, ab_cmp_pe, ab_cmp_w1, ab_cmp_b1, ab_cmp_w2, ab_w_out,
           c_w_in, c_conv_w, c_w_out, ffn_w_gate, ffn_w_up, ffn_w_down):
    w = dict(ab_w_in_bf16=ab_w_in.astype(BF16), ab_w_gate_bf16=_gate_weight(ab_w_in), ab_conv_w=ab_conv_w,
             ab_conv_b=ab_conv_b, ab_gate_a_w=ab_gate_a_w, ab_gate_a_b=ab_gate_a_b, ab_gate_x_w=ab_gate_x_w,
             ab_gate_x_b=ab_gate_x_b, ab_lru_lambda=ab_lru_lambda, ab_cmp_pe=ab_cmp_pe, ab_cmp_w1=ab_cmp_w1,
             ab_cmp_b1=ab_cmp_b1, ab_cmp_w2=ab_cmp_w2, ab_w_out_bf16=ab_w_out.astype(BF16),
             c_w_in=c_w_in, c_conv_w=c_conv_w, c_w_out_bf16=c_w_out.astype(BF16))
    bp, t_len = x_prompt.shape[:2]
    bs, ts = x_sample.shape[:2]
    mp, ms = bp * t_len, bs * ts
    past_len = page_table.shape[1] * PAGE_SIZE
    h = jnp.concatenate([x_prompt.reshape(mp, D_MODEL), x_sample.reshape(ms, D_MODEL)], axis=0)
    p_ab, s_ab, p_c, s_c = [], [], [], []
    for layer in range(DEPTH):
        hn = _rmsnorm(h, norm_mix[layer], BF16)
        i = layer // 2
        if layer % 2 == 0:
            h, op, os_ = _ab_layer(h, hn, i, mp, bp, t_len, bs, ts, past_len, w, cache_kv_cmp, cache_kv_sel,
                                   state_win_kv, state_lru_h, state_lru_conv, page_table)
            p_ab.append(op)
            s_ab.append(os_)
        else:
            h, nbp, nbs = _c_layer(h, hn, i, mp, bp, t_len, bs, ts, w, state_sconv)
            p_c.append(nbp)
            s_c.append(nbs)
        hf = _rmsnorm(h, norm_ffn[layer], BF16)
        hid, wd_bf16 = _ffn_up(hf, ffn_w_gate, ffn_w_up, ffn_w_down, layer)
        h = _matmul_residual(hid, wd_bf16[None], 0, h, name="ffn_down", col_cap=FFN_DOWN_COL_CAP)
    y_prompt = _rmsnorm(h, norm_final, F32, 0, mp).reshape(bp, t_len, D_MODEL)
    y_sample = _rmsnorm(h, norm_final, F32, mp, ms).reshape(bs, ts, D_MODEL)
    stack = lambda rows, j: jnp.stack([r[j] for r in rows])
    return (y_prompt, y_sample,
            stack(p_ab, 0), stack(p_ab, 1), stack(p_ab, 2), stack(p_ab, 3), stack(p_ab, 4), jnp.stack(p_c),
            stack(s_ab, 0), stack(s_ab, 1), stack(s_ab, 2), stack(s_ab, 3), stack(s_ab, 4), jnp.stack(s_c))
```

```python
import functools

import jax
import jax.numpy as jnp
import numpy as np
from jax import lax
from jax.experimental import pallas as pl
from jax.experimental.pallas import tpu as pltpu

D_MODEL = 4096
DEPTH = 4
PAGE_SIZE = 128
D_A = D_MODEL // 2
A_HEADS = 16
A_BLK = D_A // A_HEADS
A_CONV = 4
LRU_C = 8.0
B_HEADS = 16
HEAD_DIM = 128
KV_HEADS = 4
HPG = B_HEADS // KV_HEADS
D_B = B_HEADS * HEAD_DIM
D_KV = KV_HEADS * HEAD_DIM
CMP_LEN = 32
CMP_STRIDE = 16
SUB_PER_CMP = CMP_LEN // CMP_STRIDE
SEL_BLOCK = 64
SUB_PER_SEL = SEL_BLOCK // CMP_STRIDE
SEL_TOP = 16
WINDOW = 512
D_C = D_MODEL
C_CONV = 3
D_IN_MAIN = 2 * D_A + D_B + 6 * D_KV
RMS_EPS = 1e-6
NEG_INF = -1e30
FORCE_SCORE = 1e4
QK_SCALE = HEAD_DIM ** -0.5

COL_XA = 0
COL_GA = D_A // 128
COL_Q = 2 * D_A // 128
COL_KVC = (2 * D_A + D_B) // 128
COL_KVS = COL_KVC + 2 * D_KV // 128
COL_KVW = COL_KVS + 2 * D_KV // 128

V7X_VMEM_LIMIT_BYTES = 56 * 1024 * 1024
LANE = 128
ATT_TQ = 256
ATT_CK = 512
F32 = jnp.float32
BF16 = jnp.bfloat16


def _row_tile(m, cap=1024):
    for t in range(cap - cap % 16, 15, -16):
        if m % t == 0:
            return t
    raise ValueError(m)


def _col_tile(n, cap=512):
    for t in range(cap, LANE - 1, -LANE):
        if n % t == 0:
            return t
    raise ValueError(n)


def _round_up(x, m):
    return -(-x // m) * m


def _params(sem):
    return pltpu.CompilerParams(dimension_semantics=sem, vmem_limit_bytes=V7X_VMEM_LIMIT_BYTES)


def _dot(a, b):
    return jnp.dot(a, b, preferred_element_type=F32)


def _dot_nt(a, b):
    return lax.dot_general(a, b, (((1,), (1,)), ((), ())), preferred_element_type=F32)


def _dot_f32_exact(x, w_bf16):
    hi = x.astype(BF16)
    r1 = x - hi.astype(F32)
    mid = r1.astype(BF16)
    lo = (r1 - mid.astype(F32)).astype(BF16)
    return _dot(hi, w_bf16) + (_dot(mid, w_bf16) + _dot(lo, w_bf16))


def _gelu_tanh(x):
    return 0.5 * x * (1.0 + jnp.tanh(0.7978845608028654 * (x + 0.044715 * (x * x * x))))


def _lanes(x128, w):
    return x128 if w == LANE else jnp.concatenate([x128] * (w // LANE), axis=1)


def _rmsnorm_kernel(x_ref, g_ref, o_ref):
    x = x_ref[...]
    y = x * lax.rsqrt(jnp.mean(x * x, axis=-1, keepdims=True) + RMS_EPS)
    o_ref[...] = (y * g_ref[...]).astype(o_ref.dtype)


def _rmsnorm(x, g, out_dtype, row0=0, n_rows=None):
    d = x.shape[1]
    m = n_rows or x.shape[0]
    tm = _row_tile(m, 768)
    assert row0 % tm == 0
    rb0 = row0 // tm
    return pl.pallas_call(
        _rmsnorm_kernel,
        out_shape=jax.ShapeDtypeStruct((m, d), out_dtype),
        grid=(m // tm,),
        in_specs=[pl.BlockSpec((tm, d), lambda i: (rb0 + i, 0)), pl.BlockSpec((1, d), lambda i: (0, 0))],
        out_specs=pl.BlockSpec((tm, d), lambda i: (i, 0)),
        compiler_params=_params(("parallel",)),
        name="rmsnorm",
    )(x, g.reshape(1, d))


def _wspec(layer, k, tn):
    return pl.BlockSpec((None, k, tn), lambda i, j: (layer, 0, j))


def _mm_kernel(x_ref, w_ref, o_ref):
    o_ref[...] = _dot(x_ref[...], w_ref[...].astype(BF16)).astype(o_ref.dtype)


def _matmul(x, w, layer, out_dtype=F32, name="matmul", row_cap=1024, col_cap=512):
    m, k = x.shape
    n = w.shape[2]
    tm, tn = _row_tile(m, row_cap), _col_tile(n, col_cap)
    return pl.pallas_call(
        _mm_kernel,
        out_shape=jax.ShapeDtypeStruct((m, n), out_dtype),
        grid=(m // tm, n // tn),
        in_specs=[pl.BlockSpec((tm, k), lambda i, j: (i, 0)), _wspec(layer, k, tn)],
        out_specs=pl.BlockSpec((tm, tn), lambda i, j: (i, j)),
        compiler_params=_params(("parallel", "arbitrary")),
        name=name,
    )(x, w)


def _in_proj_kernel(x_ref, w_ref, z_ref, kv_ref, *, j_kv0, tiles_per_seg):
    j = pl.program_id(1)
    tm, tn = z_ref.shape
    heads_per_tile = tn // HEAD_DIM
    rows_per_token = 2 * KV_HEADS
    acc = _dot(x_ref[...], w_ref[...].astype(BF16))
    z_ref[...] = acc
    for part in range(tiles_per_seg):
        @pl.when((j >= j_kv0) & ((j - j_kv0) % tiles_per_seg == part))
        def _():
            for c in range(heads_per_tile):
                kv_ref[pl.ds(part * heads_per_tile + c, tm, stride=rows_per_token), :] = (
                    acc[:, c * HEAD_DIM:(c + 1) * HEAD_DIM])


def _in_proj(x, w, layer):
    m, k = x.shape
    n = D_IN_MAIN
    tm, tn = _row_tile(m, WIDE_ROW_CAP), _col_tile(n)
    seg = 2 * D_KV
    assert seg % tn == 0 and (COL_KVC * LANE) % tn == 0 and n == COL_KVC * LANE + 3 * seg and w.shape[2] >= n
    tiles_per_seg = seg // tn
    j_kv0 = COL_KVC * LANE // tn
    rpt = 2 * KV_HEADS
    return pl.pallas_call(
        functools.partial(_in_proj_kernel, j_kv0=j_kv0, tiles_per_seg=tiles_per_seg),
        out_shape=(jax.ShapeDtypeStruct((m, n), F32), jax.ShapeDtypeStruct((3, m * rpt, HEAD_DIM), F32)),
        grid=(m // tm, n // tn),
        in_specs=[pl.BlockSpec((tm, k), lambda i, j: (i, 0)), _wspec(layer, k, tn)],
        out_specs=(pl.BlockSpec((tm, tn), lambda i, j: (i, j)),
                   pl.BlockSpec((None, tm * rpt, HEAD_DIM),
                                lambda i, j: (jnp.maximum(j - j_kv0, 0) // tiles_per_seg, i, 0))),
        compiler_params=_params(("parallel", "arbitrary")),
        name="ab_in_proj",
    )(x, w)


def _mm_res_kernel(x_ref, w_ref, r_ref, o_ref):
    o_ref[...] = r_ref[...] + _dot(x_ref[...], w_ref[...].astype(BF16))


def _matmul_residual(x, w, layer, r, name="matmul_res", row_cap=1024, col_cap=512):
    m, k = x.shape
    n = w.shape[2]
    tm, tn = _row_tile(m, row_cap), _col_tile(n, col_cap)
    return pl.pallas_call(
        _mm_res_kernel,
        out_shape=jax.ShapeDtypeStruct((m, n), F32),
        grid=(m // tm, n // tn),
        in_specs=[pl.BlockSpec((tm, k), lambda i, j: (i, 0)), _wspec(layer, k, tn),
                  pl.BlockSpec((tm, tn), lambda i, j: (i, j))],
        out_specs=pl.BlockSpec((tm, tn), lambda i, j: (i, j)),
        compiler_params=_params(("parallel", "arbitrary")),
        name=name,
    )(x, w, r)


def _mm2_res_kernel(x1_ref, x2_ref, w1_ref, w2_ref, r_ref, o_ref):
    o_ref[...] = r_ref[...] + (_dot(x1_ref[...], w1_ref[...].astype(BF16))
                               + _dot(x2_ref[...], w2_ref[...].astype(BF16)))


def _matmul2_residual(x1, x2, w, layer, r, name="matmul2_res"):
    m, k1 = x1.shape
    k2 = x2.shape[1]
    n = w.shape[2]
    tm, tn = _row_tile(m, WIDE_ROW_CAP), _col_tile(n)
    assert k1 % 8 == 0 and k1 == k2
    return pl.pallas_call(
        _mm2_res_kernel,
        out_shape=jax.ShapeDtypeStruct((m, n), F32),
        grid=(m // tm, n // tn),
        in_specs=[pl.BlockSpec((tm, k1), lambda i, j: (i, 0)), pl.BlockSpec((tm, k2), lambda i, j: (i, 0)),
                  pl.BlockSpec((None, k1, tn), lambda i, j: (layer, 0, j)),
                  pl.BlockSpec((None, k2, tn), lambda i, j: (layer, 1, j)),
                  pl.BlockSpec((tm, tn), lambda i, j: (i, j))],
        out_specs=pl.BlockSpec((tm, tn), lambda i, j: (i, j)),
        compiler_params=_params(("parallel", "arbitrary")),
        name=name,
    )(x1, x2, w, w, r)


def _ffn_up_kernel(x_ref, wg_ref, wu_ref, wd_ref, o_ref, wd_out_ref, *, cast_row_tiles):
    x = x_ref[...]
    g = _dot(x, wg_ref[...].astype(BF16))
    u = _dot(x, wu_ref[...].astype(BF16))
    o_ref[...] = (g * jax.nn.sigmoid(g) * u).astype(o_ref.dtype)

    @pl.when(pl.program_id(0) < cast_row_tiles)
    def _():
        wd_out_ref[...] = wd_ref[...].astype(BF16)


WIDE_ROW_CAP = 1536
F32_WEIGHT_COL_CAP = 256


def _ffn_up(x, wg, wu, wd, layer):
    m, k = x.shape
    n = wg.shape[2]
    tm, tn = _row_tile(m, WIDE_ROW_CAP), _col_tile(n, F32_WEIGHT_COL_CAP)
    ni, nj = m // tm, n // tn
    kd, nd = wd.shape[1:]
    cast_row_tiles = min(2, ni)
    n_blocks = cast_row_tiles * nj
    rb = kd // n_blocks
    assert rb * n_blocks == kd and rb % 16 == 0
    blk = lambda i, j: jnp.where(i < cast_row_tiles, i * nj + j, n_blocks - 1)
    return pl.pallas_call(
        functools.partial(_ffn_up_kernel, cast_row_tiles=cast_row_tiles),
        out_shape=(jax.ShapeDtypeStruct((m, n), BF16), jax.ShapeDtypeStruct((kd, nd), BF16)),
        grid=(ni, nj),
        in_specs=[pl.BlockSpec((tm, k), lambda i, j: (i, 0)), _wspec(layer, k, tn), _wspec(layer, k, tn),
                  pl.BlockSpec((None, rb, nd), lambda i, j: (layer, blk(i, j), 0))],
        out_specs=(pl.BlockSpec((tm, tn), lambda i, j: (i, j)),
                   pl.BlockSpec((rb, nd), lambda i, j: (blk(i, j), 0))),
        compiler_params=_params(("arbitrary", "arbitrary")),
        name="ffn_up",
    )(x, wg, wu, wd)


FFN_DOWN_COL_CAP = 256


def _lru_kernel(xa_ref, ga_ref, h0_ref, buf_ref, cw_ref, cb_ref, wa_ref, ba_ref, wx_ref, bx_ref, lam_ref,
                y_ref, ht_ref, nbuf_ref):
    t_len = xa_ref.shape[0]
    x = xa_ref[...]
    buf = buf_ref[...]
    cw = cw_ref[...]
    row = lax.broadcasted_iota(jnp.int32, x.shape, 0)
    nb = A_CONV - 1
    xc = x * cw[nb:nb + 1]
    for s in range(1, A_CONV):
        xs = pltpu.roll(x, s, 0)
        for r in range(s):
            xs = jnp.where(row == r, buf[nb + r - s:nb + r - s + 1], xs)
        xc = xc + xs * cw[nb - s:nb - s + 1]
    xc = xc + cb_ref[...]
    xcb = xc.astype(BF16)
    r_g = jax.nn.sigmoid(_dot(xcb, wa_ref[...].astype(BF16)) + ba_ref[...])
    i_g = jax.nn.sigmoid(_dot(xcb, wx_ref[...].astype(BF16)) + bx_ref[...])
    nl = -lam_ref[...]
    softplus = jnp.maximum(nl, 0.0) + jnp.log1p(jnp.exp(-jnp.abs(nl)))
    log_a = -LRU_C * r_g * softplus
    a = jnp.exp(log_a)
    u = jnp.sqrt(-jnp.tanh(log_a) * (a * a + 1.0)) * i_g * xc
    d = 1
    while d < t_len:
        if d % 8:
            keep = row >= d
            a_prev = pltpu.roll(a, d, 0)
            u_prev = pltpu.roll(u, d, 0)
            u = jnp.where(keep, a * u_prev + u, u)
            a = jnp.where(keep, a * a_prev, a)
        else:
            a_hi = a[d:]
            u = jnp.concatenate([u[:d], a_hi * u[:t_len - d] + u[d:]], axis=0)
            a = jnp.concatenate([a[:d], a_hi * a[:t_len - d]], axis=0)
        d *= 2
    h = a * h0_ref[...] + u
    y_ref[...] = (h * _gelu_tanh(ga_ref[...])).astype(y_ref.dtype)
    ht_ref[...] = h[t_len - 1:t_len]
    nbuf_ref[...] = x[t_len - nb:t_len]


def _into_zeros(kernel_fn, n_in, out_rows, width, dtype):
    def wrapped(*refs):
        return kernel_fn(*refs[:n_in], *refs[n_in + 1:])
    return wrapped, pl.BlockSpec(memory_space=pl.ANY), jnp.zeros((out_rows, width), dtype), {n_in: 0}


def _lru_mixer(z, row0, n_seq, t_len, h0, buf, conv_w, conv_b, wa, ba, wx, bx, lam, out_dtype, out_rows=None):
    assert row0 % t_len == 0 and t_len >= A_CONV - 1 and t_len % 8 == 0
    rb0 = row0 // t_len
    nb = A_CONV - 1
    out_rows = out_rows or n_seq * t_len
    vec = lambda v: v.reshape(1, D_A)
    cspec = pl.BlockSpec((1, A_BLK), lambda s, h: (0, h))
    wspec = pl.BlockSpec((None, A_BLK, A_BLK), lambda s, h: (h, 0, 0))
    body, base_spec, base, alias = _into_zeros(_lru_kernel, 11, out_rows, D_A, out_dtype)
    return pl.pallas_call(
        body,
        out_shape=(jax.ShapeDtypeStruct((out_rows, D_A), out_dtype),
                   jax.ShapeDtypeStruct((n_seq, 1, D_A), F32),
                   jax.ShapeDtypeStruct((n_seq, nb, D_A), F32)),
        grid=(n_seq, A_HEADS),
        in_specs=[pl.BlockSpec((t_len, A_BLK), lambda s, h: (rb0 + s, COL_XA + h)),
                  pl.BlockSpec((t_len, A_BLK), lambda s, h: (rb0 + s, COL_GA + h)),
                  pl.BlockSpec((None, 1, A_BLK), lambda s, h: (s, 0, h)),
                  pl.BlockSpec((None, nb, A_BLK), lambda s, h: (s, 0, h)),
                  pl.BlockSpec((A_CONV, A_BLK), lambda s, h: (0, h)),
                  cspec, wspec, cspec, wspec, cspec, cspec, base_spec],
        out_specs=(pl.BlockSpec((t_len, A_BLK), lambda s, h: (s, h)),
                   pl.BlockSpec((None, 1, A_BLK), lambda s, h: (s, 0, h)),
                   pl.BlockSpec((None, nb, A_BLK), lambda s, h: (s, 0, h))),
        input_output_aliases=alias,
        compiler_params=_params(("parallel", "parallel")),
        name="rglru_mixer",
    )(z, z, h0.reshape(n_seq, 1, D_A), buf, conv_w, vec(conv_b), wa, vec(ba), wx, vec(bx), vec(lam), base)


def _compress_kernel(x_ref, pe_ref, w1_ref, b1_ref, w2_ref, o_ref):
    n_sub = o_ref.shape[0]
    pe = pe_ref[...]
    y0 = jnp.zeros((n_sub, HEAD_DIM), F32)
    y1 = jnp.zeros((n_sub, HEAD_DIM), F32)
    for j in range(CMP_STRIDE):
        xj = x_ref[pl.ds(j, n_sub, stride=CMP_STRIDE), :]
        y0 = y0 + _dot((xj + pe[j:j + 1]).astype(BF16), w1_ref[j])
        y1 = y1 + _dot((xj + pe[CMP_STRIDE + j:CMP_STRIDE + j + 1]).astype(BF16), w1_ref[CMP_STRIDE + j])
    hid = y0 + pltpu.roll(y1, n_sub - 1, 0)
    hid = _gelu_tanh(hid + b1_ref[...])
    o_ref[...] = _dot(hid.astype(BF16), w2_ref[...]).astype(o_ref.dtype)


def _compress(x2d, x_index, n_seq, t_len, pe, w1, b1, w2):
    assert SUB_PER_CMP == 2 and t_len % (8 * CMP_STRIDE) == 0
    n_sub = t_len // CMP_STRIDE
    nt = 2 * KV_HEADS
    return pl.pallas_call(
        _compress_kernel,
        out_shape=jax.ShapeDtypeStruct((n_seq, nt, n_sub, HEAD_DIM), BF16),
        grid=(n_seq, nt),
        in_specs=[pl.BlockSpec((t_len, HEAD_DIM), x_index),
                  pl.BlockSpec((None, CMP_LEN, HEAD_DIM), lambda s, c: (c // KV_HEADS, 0, 0)),
                  pl.BlockSpec((None, CMP_LEN, HEAD_DIM, HEAD_DIM), lambda s, c: (c // KV_HEADS, 0, 0, 0)),
                  pl.BlockSpec((None, 1, HEAD_DIM), lambda s, c: (c // KV_HEADS, 0, 0)),
                  pl.BlockSpec((None, HEAD_DIM, HEAD_DIM), lambda s, c: (c // KV_HEADS, 0, 0))],
        out_specs=pl.BlockSpec((None, None, n_sub, HEAD_DIM), lambda s, c: (s, c, 0, 0)),
        compiler_params=_params(("parallel", "arbitrary")),
        name="nsa_compress",
    )(x2d, pe, w1.astype(BF16), b1.reshape(2, 1, HEAD_DIM), w2.astype(BF16))


def _head_rows(ref, c, n_tok):
    return ref[pl.ds(c, n_tok, stride=2 * KV_HEADS), :]


def _page_gather_kernel(pt_ref, *refs):
    page_refs, o_ref = refs[:-1], refs[-1]
    for k, c_ref in enumerate(page_refs):
        for c in range(2 * KV_HEADS):
            o_ref[c, k * PAGE_SIZE:(k + 1) * PAGE_SIZE, :] = _head_rows(c_ref, c, PAGE_SIZE)


def _pages_per_step(n_pages, want=4):
    pps = want
    while n_pages % pps:
        pps //= 2
    return pps


def _page_gather(cache_rows, layer, page_table):
    nb, n_pages = page_table.shape
    nt = 2 * KV_HEADS
    pps = _pages_per_step(n_pages, 16)

    def page_spec(k):
        return pl.BlockSpec((None, None, PAGE_SIZE * nt, HEAD_DIM), lambda b, p, pt: (layer, pt[b, p * pps + k], 0, 0))

    return pl.pallas_call(
        _page_gather_kernel,
        out_shape=jax.ShapeDtypeStruct((nb, nt, n_pages * PAGE_SIZE, HEAD_DIM), cache_rows.dtype),
        grid_spec=pltpu.PrefetchScalarGridSpec(
            num_scalar_prefetch=1, grid=(nb, n_pages // pps),
            in_specs=[page_spec(k) for k in range(pps)],
            out_specs=pl.BlockSpec((None, nt, pps * PAGE_SIZE, HEAD_DIM), lambda b, p, pt: (b, 0, p, 0))),
        compiler_params=_params(("parallel", "arbitrary")),
        name="page_gather",
    )(page_table, *([cache_rows] * pps))


def _stack_heads(q, g0):
    return jnp.concatenate([q[:, (g0 + p) * HEAD_DIM:(g0 + p + 1) * HEAD_DIM] for p in range(HPG)], axis=0).astype(BF16)


def _softmax(sb):
    m = jnp.max(sb, axis=-1, keepdims=True)
    e = jnp.exp(sb - m)
    return e / jnp.sum(e, axis=-1, keepdims=True)


def _cmp_branch(q4, kc, vc, t_rows, slope, n_cmp):
    ncp = kc.shape[0]
    s = _dot_nt(q4, kc) * QK_SCALE
    col = lax.broadcasted_iota(jnp.int32, s.shape, 1)
    dist_i = t_rows - (col * CMP_STRIDE + (CMP_LEN - 1))
    pick = lambda x, other: jnp.where(dist_i >= 0, jnp.where(col < n_cmp, x, other), other)
    p = _softmax(pick(s - _lanes(slope, ncp) * dist_i.astype(F32), NEG_INF))
    p = pick(p, 0.0)
    return _dot(p.astype(BF16), vc), p


def _select_blocks(imp, pool, t_q, n_sel):
    imp_sel = _dot_f32_exact(imp, pool)
    jcol = lax.broadcasted_iota(jnp.int32, imp_sel.shape, 1)
    cur = lax.shift_right_logical(t_q, int(np.log2(SEL_BLOCK)))
    score = jnp.where(jcol * SEL_BLOCK <= t_q, imp_sel, -1.0)
    for forced_blk in (cur - 1, cur, 0):
        score = jnp.where(jcol == forced_blk, FORCE_SCORE, score)
    score = jnp.where(jcol < n_sel, score, -2.0)
    rank = jnp.zeros(score.shape, F32)
    for i in range(n_sel):
        ci = score[:, i:i + 1]
        tie = jnp.where(jcol > i, 1.0, 0.0)
        rank = rank + jnp.where(ci > score, 1.0, jnp.where(ci == score, tie, 0.0))
    top = min(SEL_TOP, n_sel)
    return jnp.where(rank < float(top), 1.0, 0.0)


def _pool_matrix(n_cmp, ncp, n_sel, nsp):
    assert n_cmp + SUB_PER_CMP - 1 <= n_sel * SUB_PER_SEL
    m = np.zeros((ncp, nsp), np.float32)
    for c in range(n_cmp):
        for n in range(SUB_PER_CMP):
            m[c, (c + n) // SUB_PER_SEL] += 1.0 / SUB_PER_CMP
    return jnp.asarray(m, BF16)


def _slope_rows(t_rep):
    h = np.arange(1, B_HEADS + 1, dtype=np.float64)
    s = np.exp2(-8.0 * h / B_HEADS).astype(np.float32).reshape(KV_HEADS, HPG, 1, 1)
    return jnp.asarray(np.broadcast_to(s, (KV_HEADS, HPG, t_rep, LANE)).reshape(KV_HEADS, HPG * t_rep, LANE))


def _expand_matrix(shape, blk0):
    j = lax.broadcasted_iota(jnp.int32, shape, 0)
    k = lax.broadcasted_iota(jnp.int32, shape, 1)
    return jnp.where(j == blk0 + lax.shift_right_logical(k, int(np.log2(SEL_BLOCK))), 1.0, 0.0).astype(BF16)


def _gate_combine(gs, p, tq, o_c, o_s, o_w):
    rows = slice(p * tq, (p + 1) * tq)
    return (gs[:, p:p + 1] * o_c[rows] + gs[:, HPG + p:HPG + p + 1] * o_s[rows]
            + gs[:, 2 * HPG + p:2 * HPG + p + 1] * o_w[rows])


def _attn_prompt_kernel(q_ref, kc_ref, vc_ref, ks_ref, vs_ref, kw_ref, vw_ref, gt_ref, slope_ref, pool_ref,
                        o_ref, ksb, vsb, kwb, vwb, *, n_cmp, n_sel):
    qi = pl.program_id(2)
    tq = q_ref.shape[0]
    rr = HPG * tq

    @pl.when(qi == 0)
    def _():
        ksb[...] = ks_ref[...].astype(BF16)
        vsb[...] = vs_ref[...].astype(BF16)
        kwb[...] = kw_ref[...].astype(BF16)
        vwb[...] = vw_ref[...].astype(BF16)

    t0 = qi * tq
    q4 = _stack_heads(q_ref[...], 0)
    slope = slope_ref[...]

    def t_rows(width):
        r = lax.broadcasted_iota(jnp.int32, (rr, width), 0)
        return t0 + (r & (tq - 1))

    ncp = kc_ref.shape[0]
    o_c, p_c = _cmp_branch(q4, kc_ref[...], vc_ref[...], t_rows(ncp), slope, n_cmp)
    imp = p_c[0:tq]
    for p in range(1, HPG):
        imp = imp + p_c[p * tq:(p + 1) * tq]
    nsp = pool_ref.shape[1]
    t_q = t0 + lax.broadcasted_iota(jnp.int32, (tq, nsp), 0)
    sel = _select_blocks(imp, pool_ref[...], t_q, n_sel).astype(BF16)
    sel4 = jnp.concatenate([sel] * HPG, axis=0)

    ck = ATT_CK
    tr_ck = t_rows(ck)
    slope_ck = _lanes(slope, ck)
    col_ck = lax.broadcasted_iota(jnp.int32, (rr, ck), 1)

    def sel_chunk(c, carry):
        m, l, acc = carry
        k0 = pl.multiple_of(c * ck, ck)
        s = _dot_nt(q4, ksb[pl.ds(k0, ck), :]) * QK_SCALE
        dist_i = tr_ck - (k0 + col_ck)
        selx = _dot(sel4, _expand_matrix((nsp, ck), c * (ck // SEL_BLOCK)))
        sb = jnp.where(dist_i >= 0, jnp.where(selx > 0.5, s - slope_ck * dist_i.astype(F32), NEG_INF), NEG_INF)
        m_new = jnp.maximum(m, jnp.max(sb, axis=-1, keepdims=True))
        alpha = jnp.exp(m - m_new)
        e = jnp.exp(sb - m_new)
        l = alpha * l + jnp.sum(e, axis=-1, keepdims=True)
        acc = alpha * acc + _dot(e.astype(BF16), vsb[pl.ds(k0, ck), :])
        return m_new, l, acc

    n_chunks = (t0 + tq + ck - 1) // ck
    m0 = jnp.full((rr, 1), NEG_INF, F32)
    _, l_s, acc_s = lax.fori_loop(0, n_chunks, sel_chunk, (m0, jnp.zeros((rr, 1), F32), jnp.zeros((rr, HEAD_DIM), F32)))
    o_s = acc_s / l_s

    wk = WINDOW + tq
    w0 = pl.multiple_of(jnp.maximum(t0 - WINDOW, 0), tq)
    s = _dot_nt(q4, kwb[pl.ds(w0, wk), :]) * QK_SCALE
    dist_i = t_rows(wk) - (w0 + lax.broadcasted_iota(jnp.int32, (rr, wk), 1))
    logit = s - _lanes(slope, wk) * dist_i.astype(F32)
    p_w = _softmax(jnp.where(dist_i >= 0, jnp.where(dist_i < WINDOW, logit, NEG_INF), NEG_INF))
    o_w = _dot(p_w.astype(BF16), vwb[pl.ds(w0, wk), :])

    gs = jax.nn.sigmoid(gt_ref[...])
    for p in range(HPG):
        o_ref[:, p * HEAD_DIM:(p + 1) * HEAD_DIM] = _gate_combine(gs, p, tq, o_c, o_s, o_w).astype(o_ref.dtype)


def _attn_prompt(z, gt, kvc_cmp, n_seq, t_len, out_rows=None):
    tq = ATT_TQ
    assert t_len % ATT_CK == 0 and t_len >= WINDOW + tq and tq & (tq - 1) == 0
    n_cmp = (t_len - CMP_LEN) // CMP_STRIDE + 1
    n_sel = -(-t_len // SEL_BLOCK)
    ncp = kvc_cmp.shape[2]
    nsp = _round_up(n_sel, LANE)
    nq = t_len // tq
    rr = HPG * tq
    kvspec = lambda col0: pl.BlockSpec((t_len, HEAD_DIM), lambda b, g, i: (b, col0 + g))
    out_rows = out_rows or n_seq * t_len
    body, base_spec, base, alias = _into_zeros(functools.partial(_attn_prompt_kernel, n_cmp=n_cmp, n_sel=n_sel),
                                               10, out_rows, D_B, BF16)
    return pl.pallas_call(
        body,
        out_shape=jax.ShapeDtypeStruct((out_rows, D_B), BF16),
        grid=(n_seq, KV_HEADS, nq),
        in_specs=[pl.BlockSpec((tq, HPG * HEAD_DIM), lambda b, g, i: (b * nq + i, COL_Q // HPG + g)),
                  pl.BlockSpec((None, None, ncp, HEAD_DIM), lambda b, g, i: (b, g, 0, 0)),
                  pl.BlockSpec((None, None, ncp, HEAD_DIM), lambda b, g, i: (b, KV_HEADS + g, 0, 0)),
                  kvspec(COL_KVS), kvspec(COL_KVS + KV_HEADS), kvspec(COL_KVW), kvspec(COL_KVW + KV_HEADS),
                  pl.BlockSpec((tq, LANE), lambda b, g, i: (b * nq + i, g)),
                  pl.BlockSpec((None, rr, LANE), lambda b, g, i: (g, 0, 0)),
                  pl.BlockSpec((ncp, nsp), lambda b, g, i: (0, 0)), base_spec],
        out_specs=pl.BlockSpec((tq, HPG * HEAD_DIM), lambda b, g, i: (b * nq + i, g)),
        scratch_shapes=[pltpu.VMEM((t_len, HEAD_DIM), BF16)] * 4,
        input_output_aliases=alias,
        compiler_params=_params(("parallel", "parallel", "arbitrary")),
        name="nsa_prompt_attention",
    )(z, kvc_cmp, kvc_cmp, z, z, z, z, gt, _slope_rows(tq), _pool_matrix(n_cmp, ncp, n_sel, nsp), base)


def _attn_dec_select_kernel(q_ref, kc_ref, vc_ref, slope_ref, pool_ref, oc_ref, sel_ref, *, n_cmp, n_sel, past_len):
    ts = q_ref.shape[0]
    rr = HPG * ts
    ncp = kc_ref.shape[0]
    q4 = _stack_heads(q_ref[...], 0)
    t_rows = past_len + (lax.broadcasted_iota(jnp.int32, (rr, ncp), 0) & (ts - 1))
    o_c, p_c = _cmp_branch(q4, kc_ref[...], vc_ref[...], t_rows, slope_ref[...], n_cmp)
    imp = p_c[0:ts]
    for p in range(1, HPG):
        imp = imp + p_c[p * ts:(p + 1) * ts]
    nsp = pool_ref.shape[1]
    t_q = past_len + lax.broadcasted_iota(jnp.int32, (ts, nsp), 0)
    oc_ref[...] = o_c
    sel_ref[...] = _select_blocks(imp, pool_ref[...], t_q, n_sel)


def _attn_dec_select(z, row0, kvc_cmp, n_seq, ts, past_len, n_cmp, n_sel):
    assert row0 % ts == 0 and ts & (ts - 1) == 0 and ts % 8 == 0
    rb0 = row0 // ts
    ncp = kvc_cmp.shape[2]
    nsp = _round_up(n_sel, LANE)
    rr = HPG * ts
    return pl.pallas_call(
        functools.partial(_attn_dec_select_kernel, n_cmp=n_cmp, n_sel=n_sel, past_len=past_len),
        out_shape=(jax.ShapeDtypeStruct((n_seq, KV_HEADS, rr, HEAD_DIM), F32),
                   jax.ShapeDtypeStruct((n_seq, KV_HEADS, ts, nsp), F32)),
        grid=(n_seq, KV_HEADS),
        in_specs=[pl.BlockSpec((ts, HPG * HEAD_DIM), lambda b, g: (rb0 + b, COL_Q // HPG + g)),
                  pl.BlockSpec((None, None, ncp, HEAD_DIM), lambda b, g: (b, g, 0, 0)),
                  pl.BlockSpec((None, None, ncp, HEAD_DIM), lambda b, g: (b, KV_HEADS + g, 0, 0)),
                  pl.BlockSpec((None, rr, LANE), lambda b, g: (g, 0, 0)),
                  pl.BlockSpec((ncp, nsp), lambda b, g: (0, 0))],
        out_specs=(pl.BlockSpec((None, None, rr, HEAD_DIM), lambda b, g: (b, g, 0, 0)),
                   pl.BlockSpec((None, None, ts, nsp), lambda b, g: (b, g, 0, 0))),
        compiler_params=_params(("parallel", "parallel")),
        name="nsa_decode_select",
    )(z, kvc_cmp, kvc_cmp, _slope_rows(ts), _pool_matrix(n_cmp, ncp, n_sel, nsp))


def _attn_dec_kernel(pt_ref, q_ref, sel_ref, *rest, past_len, n_steps, pages_per_step):
    page_refs = rest[:pages_per_step]
    kvs_ref, kvw_ref, win_ref, oc_ref, gt_ref, slope_ref, o_ref, nwin_ref, m_sc, l_sc, acc_sc = rest[pages_per_step:]
    pg = pl.program_id(1)
    ts = q_ref.shape[0]
    rr = HPG * ts
    nsp = sel_ref.shape[-1]
    pad_rows = LANE - ts
    row = lax.broadcasted_iota(jnp.int32, (rr, LANE), 0)
    col = lax.broadcasted_iota(jnp.int32, (rr, LANE), 1)
    t_rows = past_len + (row & (ts - 1))

    @pl.when(pg == 0)
    def _():
        m_sc[...] = jnp.full(m_sc.shape, NEG_INF, F32)
        l_sc[...] = jnp.zeros(l_sc.shape, F32)
        acc_sc[...] = jnp.zeros(acc_sc.shape, F32)

    def sel_update(g, q4, kk, vv, dist_i, expand, real_key):
        width = kk.shape[0]
        s = _dot_nt(q4, kk) * QK_SCALE
        sel4 = jnp.concatenate([sel_ref[g].astype(BF16)] * HPG, axis=0)
        sb = s - _lanes(slope_ref[g], width) * dist_i.astype(F32)
        sb = jnp.where(dist_i >= 0, jnp.where(_dot(sel4, expand) > 0.5, sb, NEG_INF), NEG_INF)
        if real_key is not None:
            sb = jnp.where(real_key, sb, NEG_INF)
        m_old = m_sc[g]
        m_new = jnp.maximum(m_old, jnp.max(sb, axis=-1, keepdims=True))
        alpha = jnp.exp(m_old - m_new)
        e = jnp.exp(sb - m_new)
        l_sc[g] = alpha * l_sc[g] + jnp.sum(e, axis=-1, keepdims=True)
        acc_sc[g] = alpha * acc_sc[g] + _dot(e.astype(BF16), vv)
        m_sc[g] = m_new

    @pl.when(pg < n_steps)
    def _():
        width = pages_per_step * PAGE_SIZE
        k0 = pg * width
        expand = _expand_matrix((nsp, width), pg * (width // SEL_BLOCK))
        wrow_ = lax.broadcasted_iota(jnp.int32, (rr, width), 0)
        wcol_ = lax.broadcasted_iota(jnp.int32, (rr, width), 1)
        dist_i = (past_len + (wrow_ & (ts - 1))) - (k0 + wcol_)
        for g in range(KV_HEADS):
            q4 = _stack_heads(q_ref[...], g * HPG)
            kk = jnp.concatenate([_head_rows(r, g, PAGE_SIZE) for r in page_refs], axis=0).astype(BF16)
            vv = jnp.concatenate([_head_rows(r, KV_HEADS + g, PAGE_SIZE) for r in page_refs], axis=0).astype(BF16)
            sel_update(g, q4, kk, vv, dist_i, expand, None)

    @pl.when(pg == n_steps)
    def _():
        zpad = jnp.zeros((pad_rows, HEAD_DIM), F32)
        gs = jax.nn.sigmoid(gt_ref[...])
        jj = lax.broadcasted_iota(jnp.int32, (nsp, LANE), 0)
        kk_i = lax.broadcasted_iota(jnp.int32, (nsp, LANE), 1)
        new_blk = lax.shift_right_logical(past_len + kk_i, int(np.log2(SEL_BLOCK)))
        expand_new = jnp.where(jj == new_blk, 1.0, 0.0).astype(BF16)
        is_new = col < ts
        nt = 2 * KV_HEADS
        wlen = win_ref.shape[0] // nt
        wrow = lax.broadcasted_iota(jnp.int32, (rr, wlen), 0)
        wcol = lax.broadcasted_iota(jnp.int32, (rr, wlen), 1)
        dist_w = (past_len + (wrow & (ts - 1))) - (past_len - wlen + wcol)
        ok_w = (dist_w >= 0) & (dist_w < WINDOW)
        dist_n = t_rows - (past_len + col)
        ok_n = (dist_n >= 0) & (dist_n < WINDOW) & is_new
        for g in range(KV_HEADS):
            q4 = _stack_heads(q_ref[...], g * HPG)
            ksl = slice(g * HEAD_DIM, (g + 1) * HEAD_DIM)
            vsl = slice(D_KV + g * HEAD_DIM, D_KV + (g + 1) * HEAD_DIM)
            kn = jnp.concatenate([kvs_ref[:, ksl], zpad], axis=0).astype(BF16)
            vn = jnp.concatenate([kvs_ref[:, vsl], zpad], axis=0).astype(BF16)
            sel_update(g, q4, kn, vn, dist_n, expand_new, is_new)
            o_s = acc_sc[g] / l_sc[g]
            slope = slope_ref[g]
            s1 = _dot_nt(q4, _head_rows(win_ref, g, wlen).astype(BF16)) * QK_SCALE
            s1 = jnp.where(ok_w, s1 - _lanes(slope, wlen) * dist_w.astype(F32), NEG_INF)
            kwn = jnp.concatenate([kvw_ref[:, ksl], zpad], axis=0).astype(BF16)
            vwn = jnp.concatenate([kvw_ref[:, vsl], zpad], axis=0).astype(BF16)
            s2 = _dot_nt(q4, kwn) * QK_SCALE
            s2 = jnp.where(ok_n, s2 - slope * dist_n.astype(F32), NEG_INF)
            m = jnp.maximum(jnp.max(s1, axis=-1, keepdims=True), jnp.max(s2, axis=-1, keepdims=True))
            e1 = jnp.exp(s1 - m)
            e2 = jnp.exp(s2 - m)
            inv = 1.0 / (jnp.sum(e1, axis=-1, keepdims=True) + jnp.sum(e2, axis=-1, keepdims=True))
            o_w = (_dot((e1 * inv).astype(BF16), _head_rows(win_ref, KV_HEADS + g, wlen).astype(BF16))
                   + _dot((e2 * inv).astype(BF16), vwn))
            o_c = oc_ref[g]
            for p in range(HPG):
                c0 = (g * HPG + p) * HEAD_DIM
                o_ref[:, c0:c0 + HEAD_DIM] = _gate_combine(gs[:, g * LANE:(g + 1) * LANE], p, ts, o_c, o_s, o_w)
        keep = (wlen - ts) * nt
        nwin_ref[0:keep, :] = win_ref[ts * nt:wlen * nt, :]
        for c in range(nt):
            nwin_ref[pl.ds(keep + c, ts, stride=nt), :] = kvw_ref[:, c * HEAD_DIM:(c + 1) * HEAD_DIM]


def _attn_decode(z, gt, row0, n_seq, ts, page_table, cache_rows, layer, win_rows, o_c, sel, past_len):
    n_pages = page_table.shape[1]
    pps = _pages_per_step(n_pages, 16)
    n_steps = n_pages // pps
    rb0 = row0 // ts
    rr = HPG * ts
    nsp = sel.shape[-1]
    nt = 2 * KV_HEADS
    wrows = win_rows.shape[2]
    w = 2 * D_KV
    assert wrows == WINDOW * nt and past_len % SEL_BLOCK == 0 and PAGE_SIZE == LANE
    last = n_steps - 1

    def page_spec(k):
        return pl.BlockSpec((None, None, PAGE_SIZE * nt, HEAD_DIM),
                            lambda b, p, pt: (layer, pt[b, jnp.minimum(p, last) * pps + k], 0, 0))

    return pl.pallas_call(
        functools.partial(_attn_dec_kernel, past_len=past_len, n_steps=n_steps, pages_per_step=pps),
        out_shape=(jax.ShapeDtypeStruct((n_seq * ts, D_B), F32),
                   jax.ShapeDtypeStruct((n_seq, wrows, HEAD_DIM), F32)),
        grid_spec=pltpu.PrefetchScalarGridSpec(
            num_scalar_prefetch=1, grid=(n_seq, n_steps + 1),
            in_specs=[pl.BlockSpec((ts, D_B), lambda b, p, pt: (rb0 + b, COL_Q * LANE // D_B)),
                      pl.BlockSpec((None, KV_HEADS, ts, nsp), lambda b, p, pt: (b, 0, 0, 0))]
                     + [page_spec(k) for k in range(pps)]
                     + [pl.BlockSpec((ts, w), lambda b, p, pt: (rb0 + b, COL_KVS * LANE // w)),
                        pl.BlockSpec((ts, w), lambda b, p, pt: (rb0 + b, COL_KVW * LANE // w)),
                        pl.BlockSpec((None, None, wrows, HEAD_DIM), lambda b, p, pt: (layer, b, 0, 0)),
                        pl.BlockSpec((None, KV_HEADS, rr, HEAD_DIM), lambda b, p, pt: (b, 0, 0, 0)),
                        pl.BlockSpec((ts, KV_HEADS * LANE), lambda b, p, pt: (rb0 + b, 0)),
                        pl.BlockSpec((KV_HEADS, rr, LANE), lambda b, p, pt: (0, 0, 0))],
            out_specs=(pl.BlockSpec((ts, D_B), lambda b, p, pt: (b, 0)),
                       pl.BlockSpec((None, wrows, HEAD_DIM), lambda b, p, pt: (b, 0, 0))),
            scratch_shapes=[pltpu.VMEM((KV_HEADS, rr, 1), F32), pltpu.VMEM((KV_HEADS, rr, 1), F32),
                            pltpu.VMEM((KV_HEADS, rr, HEAD_DIM), F32)]),
        compiler_params=_params(("parallel", "arbitrary")),
        name="nsa_decode_attention",
    )(page_table, z, sel, *([cache_rows] * pps), z, z, win_rows, o_c, gt, _slope_rows(ts))


def _sconv_kernel(bg_ref, cg_ref, v_ref, buf_ref, cw_ref, y_ref, nbuf_ref):
    t_len = cg_ref.shape[0]
    nb = C_CONV - 1
    cv = cg_ref[...] * v_ref[...]
    buf = buf_ref[...]
    cw = cw_ref[...]
    row = lax.broadcasted_iota(jnp.int32, cv.shape, 0)
    y = cv * cw[nb:nb + 1]
    for s in range(1, C_CONV):
        xs = pltpu.roll(cv, s, 0)
        for r in range(s):
            xs = jnp.where(row == r, buf[nb + r - s:nb + r - s + 1], xs)
        y = y + xs * cw[nb - s:nb - s + 1]
    y_ref[...] = (bg_ref[...] * y).astype(y_ref.dtype)
    nbuf_ref[...] = cv[t_len - nb:t_len]


def _sconv_mixer(zc, row0, n_seq, t_len, buf, conv_w, out_dtype, out_rows=None):
    assert row0 % t_len == 0 and t_len >= C_CONV - 1 and t_len % 8 == 0
    rb0 = row0 // t_len
    nb = C_CONV - 1
    tc = 256 if t_len > 256 else 2048
    nc = D_C // tc
    zspec = lambda third: pl.BlockSpec((t_len, tc), lambda s, c: (rb0 + s, third * nc + c))
    out_rows = out_rows or n_seq * t_len
    body, base_spec, base, alias = _into_zeros(_sconv_kernel, 5, out_rows, D_C, out_dtype)
    return pl.pallas_call(
        body,
        out_shape=(jax.ShapeDtypeStruct((out_rows, D_C), out_dtype),
                   jax.ShapeDtypeStruct((n_seq, nb, D_C), F32)),
        grid=(n_seq, nc),
        in_specs=[zspec(0), zspec(1), zspec(2),
                  pl.BlockSpec((None, nb, tc), lambda s, c: (s, 0, c)),
                  pl.BlockSpec((C_CONV, tc), lambda s, c: (0, c)), base_spec],
        out_specs=(pl.BlockSpec((t_len, tc), lambda s, c: (s, c)),
                   pl.BlockSpec((None, nb, tc), lambda s, c: (s, 0, c))),
        input_output_aliases=alias,
        compiler_params=_params(("parallel", "parallel")),
        name="sconv_mixer",
    )(zc, zc, zc, buf, conv_w, base)


def _gate_weight(w_in):
    nl, k = w_in.shape[:2]
    wg = w_in[:, :, D_IN_MAIN:].reshape(nl, k, 3, KV_HEADS, HPG).transpose(0, 1, 3, 2, 4)
    wg = jnp.pad(wg.reshape(nl, k, KV_HEADS, 3 * HPG), ((0, 0), (0, 0), (0, 0), (0, LANE - 3 * HPG)))
    return wg.reshape(nl, k, KV_HEADS * LANE).astype(BF16)


def _ab_layer(h, hn, i, mp, bp, t_len, bs, ts, past_len, w, cache_kv_cmp, cache_kv_sel, state_win_kv, state_lru_h,
              state_lru_conv, page_table):
    z, kv_rows = _in_proj(hn, w["ab_w_in_bf16"], i)
    gt = _matmul(hn, w["ab_w_gate_bf16"], i, name="ab_gate_proj")
    nt = 2 * KV_HEADS
    rows_view = lambda a: a.reshape(a.shape[0], a.shape[1], -1, HEAD_DIM)
    lru_w = (w["ab_conv_w"][i], w["ab_conv_b"][i], w["ab_gate_a_w"][i], w["ab_gate_a_b"][i], w["ab_gate_x_w"][i],
             w["ab_gate_x_b"][i], w["ab_lru_lambda"][i])
    cmp_w = (w["ab_cmp_pe"][i], w["ab_cmp_w1"][i], w["ab_cmp_b1"][i], w["ab_cmp_w2"][i])
    tk = past_len + ts
    n_cmp_s = (tk - CMP_LEN) // CMP_STRIDE + 1
    n_sel_s = -(-tk // SEL_BLOCK)
    assert n_cmp_s == (past_len - CMP_LEN) // CMP_STRIDE + 1

    ya_p, hT_p, nbuf_p = _lru_mixer(z, 0, bp, t_len, jnp.zeros((bp, D_A), F32), jnp.zeros((bp, A_CONV - 1, D_A), F32),
                                    *lru_w, out_dtype=BF16, out_rows=mp + bs * ts)
    kvc_p = _compress(z, lambda s, c: (s, COL_KVC + c), bp, t_len, *cmp_w)
    ob_p = _attn_prompt(z, gt, kvc_p, bp, t_len, out_rows=mp + bs * ts)

    ya_s, hT_s, nbuf_s = _lru_mixer(z, mp, bs, ts, state_lru_h[i], state_lru_conv[i], *lru_w, out_dtype=F32)
    past_cmp = _page_gather(rows_view(cache_kv_cmp), i, page_table)
    kvc_s = _compress(past_cmp.reshape(bs * nt * past_len, HEAD_DIM), lambda s, c: (s * nt + c, 0), bs, past_len, *cmp_w)
    oc_s, sel_s = _attn_dec_select(z, mp, kvc_s, bs, ts, past_len, n_cmp_s, n_sel_s)
    ob_s, nwin_s = _attn_decode(z, gt, mp, bs, ts, page_table, rows_view(cache_kv_sel), i, rows_view(state_win_kv),
                                oc_s, sel_s, past_len)

    ya = lax.dynamic_update_slice(ya_p, ya_s.astype(BF16), (mp, 0))
    ob = lax.dynamic_update_slice(ob_p, ob_s.astype(BF16), (mp, 0))
    h = _matmul2_residual(ya, ob, w["ab_w_out_bf16"], i, h, name="ab_out_proj")

    kv5 = lambda a, nb_, tt: a.reshape(nb_, tt, 2, KV_HEADS, HEAD_DIM)
    wl = min(WINDOW, t_len)
    kvw_p = kv5(kv_rows[2, :mp * nt], bp, t_len)
    outs_p = (kv5(kv_rows[0, :mp * nt], bp, t_len), kv5(kv_rows[1, :mp * nt], bp, t_len),
              kvw_p[:, t_len - wl:], hT_p.reshape(bp, D_A), nbuf_p)
    outs_s = (kv5(kv_rows[0, mp * nt:], bs, ts), kv5(kv_rows[1, mp * nt:], bs, ts),
              kv5(nwin_s, bs, WINDOW), hT_s.reshape(bs, D_A), nbuf_s)
    return h, outs_p, outs_s


def _c_layer(h, hn, i, mp, bp, t_len, bs, ts, w, state_sconv):
    zc = _matmul(hn, w["c_w_in"], i, name="c_in_proj", row_cap=WIDE_ROW_CAP, col_cap=F32_WEIGHT_COL_CAP)
    y_p, nbuf_p = _sconv_mixer(zc, 0, bp, t_len, jnp.zeros((bp, C_CONV - 1, D_C), F32), w["c_conv_w"][i], BF16,
                               out_rows=mp + bs * ts)
    y_s, nbuf_s = _sconv_mixer(zc, mp, bs, ts, state_sconv[i], w["c_conv_w"][i], F32)
    y = lax.dynamic_update_slice(y_p, y_s.astype(BF16), (mp, 0))
    h = _matmul_residual(y, w["c_w_out_bf16"], i, h, name="c_out_proj", row_cap=WIDE_ROW_CAP)
    return h, nbuf_p, nbuf_s


def kernel(x_prompt, x_sample, cache_kv_cmp, cache_kv_sel, state_win_kv, state_lru_h, state_lru_conv, state_sconv,
           page_table, norm_mix, norm_ffn, norm_final, ab_w_in, ab_conv_w, ab_conv_b, ab_gate_a_w, ab_gate_a_b,
           ab_gate_x_w, ab_gate_x_b, ab_lru_lambda, ab_cmp_pe, ab_cmp_w1, ab_cmp_b1, ab_cmp_w2, ab_w_out,
           c_w_in, c_conv_w, c_w_out, ffn_w_gate, ffn_w_up, ffn_w_down):
    w = dict(ab_w_in_bf16=ab_w_in.astype(BF16), ab_w_gate_bf16=_gate_weight(ab_w_in), ab_conv_w=ab_conv_w,
             ab_conv_b=ab_conv_b, ab_gate_a_w=ab_gate_a_w, ab_gate_a_b=ab_gate_a_b, ab_gate_x_w=ab_gate_x_w,
             ab_gate_x_b=ab_gate_x_b, ab_lru_lambda=ab_lru_lambda, ab_cmp_pe=ab_cmp_pe, ab_cmp_w1=ab_cmp_w1,
             ab_cmp_b1=ab_cmp_b1, ab_cmp_w2=ab_cmp_w2, ab_w_out_bf16=ab_w_out.astype(BF16),
             c_w_in=c_w_in, c_conv_w=c_conv_w, c_w_out_bf16=c_w_out.astype(BF16))
    bp, t_len = x_prompt.shape[:2]
    bs, ts = x_sample.shape[:2]
    mp, ms = bp * t_len, bs * ts
    past_len = page_table.shape[1] * PAGE_SIZE
    h = jnp.concatenate([x_prompt.reshape(mp, D_MODEL), x_sample.reshape(ms, D_MODEL)], axis=0)
    p_ab, s_ab, p_c, s_c = [], [], [], []
    for layer in range(DEPTH):
        hn = _rmsnorm(h, norm_mix[layer], BF16)
        i = layer // 2
        if layer % 2 == 0:
            h, op, os_ = _ab_layer(h, hn, i, mp, bp, t_len, bs, ts, past_len, w, cache_kv_cmp, cache_kv_sel,
                                   state_win_kv, state_lru_h, state_lru_conv, page_table)
            p_ab.append(op)
            s_ab.append(os_)
        else:
            h, nbp, nbs = _c_layer(h, hn, i, mp, bp, t_len, bs, ts, w, state_sconv)
            p_c.append(nbp)
            s_c.append(nbs)
        hf = _rmsnorm(h, norm_ffn[layer], BF16)
        hid, wd_bf16 = _ffn_up(hf, ffn_w_gate, ffn_w_up, ffn_w_down, layer)
        h = _matmul_residual(hid, wd_bf16[None], 0, h, name="ffn_down", col_cap=FFN_DOWN_COL_CAP)
    y_prompt = _rmsnorm(h, norm_final, F32, 0, mp).reshape(bp, t_len, D_MODEL)
    y_sample = _rmsnorm(h, norm_final, F32, mp, ms).reshape(bs, ts, D_MODEL)
    stack = lambda rows, j: jnp.stack([r[j] for r in rows])
    return (y_prompt, y_sample,
            stack(p_ab, 0), stack(p_ab, 1), stack(p_ab, 2), stack(p_ab, 3), stack(p_ab, 4), jnp.stack(p_c),
            stack(s_ab, 0), stack(s_ab, 1), stack(s_ab, 2), stack(s_ab, 3), stack(s_ab, 4), jnp.stack(s_c))
```

```python
import functools

import jax
import jax.numpy as jnp
import numpy as np
from jax import lax
from jax.experimental import pallas as pl
from jax.experimental.pallas import tpu as pltpu

D_MODEL = 4096
DEPTH = 4
PAGE_SIZE = 128
D_A = D_MODEL // 2
A_HEADS = 16
A_BLK = D_A // A_HEADS
A_CONV = 4
LRU_C = 8.0
B_HEADS = 16
HEAD_DIM = 128
KV_HEADS = 4
HPG = B_HEADS // KV_HEADS
D_B = B_HEADS * HEAD_DIM
D_KV = KV_HEADS * HEAD_DIM
CMP_LEN = 32
CMP_STRIDE = 16
SUB_PER_CMP = CMP_LEN // CMP_STRIDE
SEL_BLOCK = 64
SUB_PER_SEL = SEL_BLOCK // CMP_STRIDE
SEL_TOP = 16
WINDOW = 512
D_C = D_MODEL
C_CONV = 3
D_IN_MAIN = 2 * D_A + D_B + 6 * D_KV
RMS_EPS = 1e-6
NEG_INF = -1e30
FORCE_SCORE = 1e4
QK_SCALE = HEAD_DIM ** -0.5

COL_XA = 0
COL_GA = D_A // 128
COL_Q = 2 * D_A // 128
COL_KVC = (2 * D_A + D_B) // 128
COL_KVS = COL_KVC + 2 * D_KV // 128
COL_KVW = COL_KVS + 2 * D_KV // 128

V7X_VMEM_LIMIT_BYTES = 56 * 1024 * 1024
LANE = 128
ATT_TQ = 256
ATT_CK = 512
F32 = jnp.float32
BF16 = jnp.bfloat16


def _row_tile(m, cap=1024):
    for t in range(cap - cap % 16, 15, -16):
        if m % t == 0:
            return t
    raise ValueError(m)


def _col_tile(n, cap=512):
    for t in range(cap, LANE - 1, -LANE):
        if n % t == 0:
            return t
    raise ValueError(n)


def _round_up(x, m):
    return -(-x // m) * m


def _params(sem):
    return pltpu.CompilerParams(dimension_semantics=sem, vmem_limit_bytes=V7X_VMEM_LIMIT_BYTES)


def _dot(a, b):
    return jnp.dot(a, b, preferred_element_type=F32)


def _dot_nt(a, b):
    return lax.dot_general(a, b, (((1,), (1,)), ((), ())), preferred_element_type=F32)


def _dot_f32_exact(x, w_bf16):
    hi = x.astype(BF16)
    r1 = x - hi.astype(F32)
    mid = r1.astype(BF16)
    lo = (r1 - mid.astype(F32)).astype(BF16)
    return _dot(hi, w_bf16) + (_dot(mid, w_bf16) + _dot(lo, w_bf16))


def _gelu_tanh(x):
    return 0.5 * x * (1.0 + jnp.tanh(0.7978845608028654 * (x + 0.044715 * (x * x * x))))


def _lanes(x128, w):
    return x128 if w == LANE else jnp.concatenate([x128] * (w // LANE), axis=1)


def _rmsnorm_kernel(x_ref, g_ref, o_ref):
    x = x_ref[...]
    y = x * lax.rsqrt(jnp.mean(x * x, axis=-1, keepdims=True) + RMS_EPS)
    o_ref[...] = (y * g_ref[...]).astype(o_ref.dtype)


def _rmsnorm(x, g, out_dtype, row0=0, n_rows=None):
    d = x.shape[1]
    m = n_rows or x.shape[0]
    tm = _row_tile(m, 768)
    assert row0 % tm == 0
    rb0 = row0 // tm
    return pl.pallas_call(
        _rmsnorm_kernel,
        out_shape=jax.ShapeDtypeStruct((m, d), out_dtype),
        grid=(m // tm,),
        in_specs=[pl.BlockSpec((tm, d), lambda i: (rb0 + i, 0)), pl.BlockSpec((1, d), lambda i: (0, 0))],
        out_specs=pl.BlockSpec((tm, d), lambda i: (i, 0)),
        compiler_params=_params(("parallel",)),
        name="rmsnorm",
    )(x, g.reshape(1, d))


def _wspec(layer, k, tn):
    return pl.BlockSpec((None, k, tn), lambda i, j: (layer, 0, j))


def _mm_kernel(x_ref, w_ref, o_ref):
    o_ref[...] = _dot(x_ref[...], w_ref[...].astype(BF16)).astype(o_ref.dtype)


def _matmul(x, w, layer, out_dtype=F32, name="matmul", row_cap=1024, col_cap=512):
    m, k = x.shape
    n = w.shape[2]
    tm, tn = _row_tile(m, row_cap), _col_tile(n, col_cap)
    return pl.pallas_call(
        _mm_kernel,
        out_shape=jax.ShapeDtypeStruct((m, n), out_dtype),
        grid=(m // tm, n // tn),
        in_specs=[pl.BlockSpec((tm, k), lambda i, j: (i, 0)), _wspec(layer, k, tn)],
        out_specs=pl.BlockSpec((tm, tn), lambda i, j: (i, j)),
        compiler_params=_params(("parallel", "arbitrary")),
        name=name,
    )(x, w)


def _in_proj_kernel(x_ref, w_ref, z_ref, kv_ref, *, j_kv0, tiles_per_seg):
    j = pl.program_id(1)
    tm, tn = z_ref.shape
    heads_per_tile = tn // HEAD_DIM
    rows_per_token = 2 * KV_HEADS
    acc = _dot(x_ref[...], w_ref[...].astype(BF16))
    z_ref[...] = acc
    for part in range(tiles_per_seg):
        @pl.when((j >= j_kv0) & ((j - j_kv0) % tiles_per_seg == part))
        def _():
            for c in range(heads_per_tile):
                kv_ref[pl.ds(part * heads_per_tile + c, tm, stride=rows_per_token), :] = (
                    acc[:, c * HEAD_DIM:(c + 1) * HEAD_DIM])


def _in_proj(x, w, layer):
    m, k = x.shape
    n = D_IN_MAIN
    tm, tn = _row_tile(m, WIDE_ROW_CAP), _col_tile(n)
    seg = 2 * D_KV
    assert seg % tn == 0 and (COL_KVC * LANE) % tn == 0 and n == COL_KVC * LANE + 3 * seg and w.shape[2] >= n
    tiles_per_seg = seg // tn
    j_kv0 = COL_KVC * LANE // tn
    rpt = 2 * KV_HEADS
    return pl.pallas_call(
        functools.partial(_in_proj_kernel, j_kv0=j_kv0, tiles_per_seg=tiles_per_seg),
        out_shape=(jax.ShapeDtypeStruct((m, n), F32), jax.ShapeDtypeStruct((3, m * rpt, HEAD_DIM), F32)),
        grid=(m // tm, n // tn),
        in_specs=[pl.BlockSpec((tm, k), lambda i, j: (i, 0)), _wspec(layer, k, tn)],
        out_specs=(pl.BlockSpec((tm, tn), lambda i, j: (i, j)),
                   pl.BlockSpec((None, tm * rpt, HEAD_DIM),
                                lambda i, j: (jnp.maximum(j - j_kv0, 0) // tiles_per_seg, i, 0))),
        compiler_params=_params(("parallel", "arbitrary")),
        name="ab_in_proj",
    )(x, w)


def _mm_res_kernel(x_ref, w_ref, r_ref, o_ref):
    o_ref[...] = r_ref[...] + _dot(x_ref[...], w_ref[...].astype(BF16))


def _matmul_residual(x, w, layer, r, name="matmul_res", row_cap=1024, col_cap=512):
    m, k = x.shape
    n = w.shape[2]
    tm, tn = _row_tile(m, row_cap), _col_tile(n, col_cap)
    return pl.pallas_call(
        _mm_res_kernel,
        out_shape=jax.ShapeDtypeStruct((m, n), F32),
        grid=(m // tm, n // tn),
        in_specs=[pl.BlockSpec((tm, k), lambda i, j: (i, 0)), _wspec(layer, k, tn),
                  pl.BlockSpec((tm, tn), lambda i, j: (i, j))],
        out_specs=pl.BlockSpec((tm, tn), lambda i, j: (i, j)),
        compiler_params=_params(("parallel", "arbitrary")),
        name=name,
    )(x, w, r)


def _mm2_res_kernel(x1_ref, x2_ref, w1_ref, w2_ref, r_ref, o_ref):
    o_ref[...] = r_ref[...] + (_dot(x1_ref[...], w1_ref[...].astype(BF16))
                               + _dot(x2_ref[...], w2_ref[...].astype(BF16)))


def _matmul2_residual(x1, x2, w, layer, r, name="matmul2_res"):
    m, k1 = x1.shape
    k2 = x2.shape[1]
    n = w.shape[2]
    tm, tn = _row_tile(m, WIDE_ROW_CAP), _col_tile(n)
    assert k1 % 8 == 0 and k1 == k2
    return pl.pallas_call(
        _mm2_res_kernel,
        out_shape=jax.ShapeDtypeStruct((m, n), F32),
        grid=(m // tm, n // tn),
        in_specs=[pl.BlockSpec((tm, k1), lambda i, j: (i, 0)), pl.BlockSpec((tm, k2), lambda i, j: (i, 0)),
                  pl.BlockSpec((None, k1, tn), lambda i, j: (layer, 0, j)),
                  pl.BlockSpec((None, k2, tn), lambda i, j: (layer, 1, j)),
                  pl.BlockSpec((tm, tn), lambda i, j: (i, j))],
        out_specs=pl.BlockSpec((tm, tn), lambda i, j: (i, j)),
        compiler_params=_params(("parallel", "arbitrary")),
        name=name,
    )(x1, x2, w, w, r)


def _ffn_up_kernel(x_ref, wg_ref, wu_ref, wd_ref, o_ref, wd_out_ref, *, cast_row_tiles):
    x = x_ref[...]
    g = _dot(x, wg_ref[...].astype(BF16))
    u = _dot(x, wu_ref[...].astype(BF16))
    o_ref[...] = (g * jax.nn.sigmoid(g) * u).astype(o_ref.dtype)

    @pl.when(pl.program_id(0) < cast_row_tiles)
    def _():
        wd_out_ref[...] = wd_ref[...].astype(BF16)


WIDE_ROW_CAP = 1536
F32_WEIGHT_COL_CAP = 256


def _ffn_up(x, wg, wu, wd, layer):
    m, k = x.shape
    n = wg.shape[2]
    tm, tn = _row_tile(m, WIDE_ROW_CAP), _col_tile(n, F32_WEIGHT_COL_CAP)
    ni, nj = m // tm, n // tn
    kd, nd = wd.shape[1:]
    cast_row_tiles = min(2, ni)
    n_blocks = cast_row_tiles * nj
    rb = kd // n_blocks
    assert rb * n_blocks == kd and rb % 16 == 0
    blk = lambda i, j: jnp.where(i < cast_row_tiles, i * nj + j, n_blocks - 1)
    return pl.pallas_call(
        functools.partial(_ffn_up_kernel, cast_row_tiles=cast_row_tiles),
        out_shape=(jax.ShapeDtypeStruct((m, n), BF16), jax.ShapeDtypeStruct((kd, nd), BF16)),
        grid=(ni, nj),
        in_specs=[pl.BlockSpec((tm, k), lambda i, j: (i, 0)), _wspec(layer, k, tn), _wspec(layer, k, tn),
                  pl.BlockSpec((None, rb, nd), lambda i, j: (layer, blk(i, j), 0))],
        out_specs=(pl.BlockSpec((tm, tn), lambda i, j: (i, j)),
                   pl.BlockSpec((rb, nd), lambda i, j: (blk(i, j), 0))),
        compiler_params=_params(("arbitrary", "arbitrary")),
        name="ffn_up",
    )(x, wg, wu, wd)


FFN_DOWN_COL_CAP = 256


def _lru_kernel(xa_ref, ga_ref, h0_ref, buf_ref, cw_ref, cb_ref, wa_ref, ba_ref, wx_ref, bx_ref, lam_ref,
                y_ref, ht_ref, nbuf_ref):
    t_len = xa_ref.shape[0]
    x = xa_ref[...]
    buf = buf_ref[...]
    cw = cw_ref[...]
    row = lax.broadcasted_iota(jnp.int32, x.shape, 0)
    nb = A_CONV - 1
    xc = x * cw[nb:nb + 1]
    for s in range(1, A_CONV):
        xs = pltpu.roll(x, s, 0)
        for r in range(s):
            xs = jnp.where(row == r, buf[nb + r - s:nb + r - s + 1], xs)
        xc = xc + xs * cw[nb - s:nb - s + 1]
    xc = xc + cb_ref[...]
    xcb = xc.astype(BF16)
    r_g = jax.nn.sigmoid(_dot(xcb, wa_ref[...].astype(BF16)) + ba_ref[...])
    i_g = jax.nn.sigmoid(_dot(xcb, wx_ref[...].astype(BF16)) + bx_ref[...])
    nl = -lam_ref[...]
    softplus = jnp.maximum(nl, 0.0) + jnp.log1p(jnp.exp(-jnp.abs(nl)))
    log_a = -LRU_C * r_g * softplus
    a = jnp.exp(log_a)
    u = jnp.sqrt(-jnp.tanh(log_a) * (a * a + 1.0)) * i_g * xc
    d = 1
    while d < t_len:
        if d % 8:
            keep = row >= d
            a_prev = pltpu.roll(a, d, 0)
            u_prev = pltpu.roll(u, d, 0)
            u = jnp.where(keep, a * u_prev + u, u)
            a = jnp.where(keep, a * a_prev, a)
        else:
            a_hi = a[d:]
            u = jnp.concatenate([u[:d], a_hi * u[:t_len - d] + u[d:]], axis=0)
            a = jnp.concatenate([a[:d], a_hi * a[:t_len - d]], axis=0)
        d *= 2
    h = a * h0_ref[...] + u
    y_ref[...] = (h * _gelu_tanh(ga_ref[...])).astype(y_ref.dtype)
    ht_ref[...] = h[t_len - 1:t_len]
    nbuf_ref[...] = x[t_len - nb:t_len]


def _into_zeros(kernel_fn, n_in, out_rows, width, dtype):
    def wrapped(*refs):
        return kernel_fn(*refs[:n_in], *refs[n_in + 1:])
    return wrapped, pl.BlockSpec(memory_space=pl.ANY), jnp.zeros((out_rows, width), dtype), {n_in: 0}


def _lru_mixer(z, row0, n_seq, t_len, h0, buf, conv_w, conv_b, wa, ba, wx, bx, lam, out_dtype, out_rows=None):
    assert row0 % t_len == 0 and t_len >= A_CONV - 1 and t_len % 8 == 0
    rb0 = row0 // t_len
    nb = A_CONV - 1
    out_rows = out_rows or n_seq * t_len
    vec = lambda v: v.reshape(1, D_A)
    cspec = pl.BlockSpec((1, A_BLK), lambda s, h: (0, h))
    wspec = pl.BlockSpec((None, A_BLK, A_BLK), lambda s, h: (h, 0, 0))
    body, base_spec, base, alias = _into_zeros(_lru_kernel, 11, out_rows, D_A, out_dtype)
    return pl.pallas_call(
        body,
        out_shape=(jax.ShapeDtypeStruct((out_rows, D_A), out_dtype),
                   jax.ShapeDtypeStruct((n_seq, 1, D_A), F32),
                   jax.ShapeDtypeStruct((n_seq, nb, D_A), F32)),
        grid=(n_seq, A_HEADS),
        in_specs=[pl.BlockSpec((t_len, A_BLK), lambda s, h: (rb0 + s, COL_XA + h)),
                  pl.BlockSpec((t_len, A_BLK), lambda s, h: (rb0 + s, COL_GA + h)),
                  pl.BlockSpec((None, 1, A_BLK), lambda s, h: (s, 0, h)),
                  pl.BlockSpec((None, nb, A_BLK), lambda s, h: (s, 0, h)),
                  pl.BlockSpec((A_CONV, A_BLK), lambda s, h: (0, h)),
                  cspec, wspec, cspec, wspec, cspec, cspec, base_spec],
        out_specs=(pl.BlockSpec((t_len, A_BLK), lambda s, h: (s, h)),
                   pl.BlockSpec((None, 1, A_BLK), lambda s, h: (s, 0, h)),
                   pl.BlockSpec((None, nb, A_BLK), lambda s, h: (s, 0, h))),
        input_output_aliases=alias,
        compiler_params=_params(("parallel", "parallel")),
        name="rglru_mixer",
    )(z, z, h0.reshape(n_seq, 1, D_A), buf, conv_w, vec(conv_b), wa, vec(ba), wx, vec(bx), vec(lam), base)


def _compress_kernel(x_ref, pe_ref, w1_ref, b1_ref, w2_ref, o_ref):
    n_sub = o_ref.shape[0]
    pe = pe_ref[...]
    y0 = jnp.zeros((n_sub, HEAD_DIM), F32)
    y1 = jnp.zeros((n_sub, HEAD_DIM), F32)
    for j in range(CMP_STRIDE):
        xj = x_ref[pl.ds(j, n_sub, stride=CMP_STRIDE), :]
        y0 = y0 + _dot((xj + pe[j:j + 1]).astype(BF16), w1_ref[j])
        y1 = y1 + _dot((xj + pe[CMP_STRIDE + j:CMP_STRIDE + j + 1]).astype(BF16), w1_ref[CMP_STRIDE + j])
    hid = y0 + pltpu.roll(y1, n_sub - 1, 0)
    hid = _gelu_tanh(hid + b1_ref[...])
    o_ref[...] = _dot(hid.astype(BF16), w2_ref[...]).astype(o_ref.dtype)


def _compress(x2d, x_index, n_seq, t_len, pe, w1, b1, w2):
    assert SUB_PER_CMP == 2 and t_len % (8 * CMP_STRIDE) == 0
    n_sub = t_len // CMP_STRIDE
    nt = 2 * KV_HEADS
    return pl.pallas_call(
        _compress_kernel,
        out_shape=jax.ShapeDtypeStruct((n_seq, nt, n_sub, HEAD_DIM), BF16),
        grid=(n_seq, nt),
        in_specs=[pl.BlockSpec((t_len, HEAD_DIM), x_index),
                  pl.BlockSpec((None, CMP_LEN, HEAD_DIM), lambda s, c: (c // KV_HEADS, 0, 0)),
                  pl.BlockSpec((None, CMP_LEN, HEAD_DIM, HEAD_DIM), lambda s, c: (c // KV_HEADS, 0, 0, 0)),
                  pl.BlockSpec((None, 1, HEAD_DIM), lambda s, c: (c // KV_HEADS, 0, 0)),
                  pl.BlockSpec((None, HEAD_DIM, HEAD_DIM), lambda s, c: (c // KV_HEADS, 0, 0))],
        out_specs=pl.BlockSpec((None, None, n_sub, HEAD_DIM), lambda s, c: (s, c, 0, 0)),
        compiler_params=_params(("parallel", "arbitrary")),
        name="nsa_compress",
    )(x2d, pe, w1.astype(BF16), b1.reshape(2, 1, HEAD_DIM), w2.astype(BF16))


def _head_rows(ref, c, n_tok):
    return ref[pl.ds(c, n_tok, stride=2 * KV_HEADS), :]


def _page_gather_kernel(pt_ref, *refs):
    page_refs, o_ref = refs[:-1], refs[-1]
    for k, c_ref in enumerate(page_refs):
        for c in range(2 * KV_HEADS):
            o_ref[c, k * PAGE_SIZE:(k + 1) * PAGE_SIZE, :] = _head_rows(c_ref, c, PAGE_SIZE)


def _pages_per_step(n_pages, want=4):
    pps = want
    while n_pages % pps:
        pps //= 2
    return pps


def _page_gather(cache_rows, layer, page_table):
    nb, n_pages = page_table.shape
    nt = 2 * KV_HEADS
    pps = _pages_per_step(n_pages, 16)

    def page_spec(k):
        return pl.BlockSpec((None, None, PAGE_SIZE * nt, HEAD_DIM), lambda b, p, pt: (layer, pt[b, p * pps + k], 0, 0))

    return pl.pallas_call(
        _page_gather_kernel,
        out_shape=jax.ShapeDtypeStruct((nb, nt, n_pages * PAGE_SIZE, HEAD_DIM), cache_rows.dtype),
        grid_spec=pltpu.PrefetchScalarGridSpec(
            num_scalar_prefetch=1, grid=(nb, n_pages // pps),
            in_specs=[page_spec(k) for k in range(pps)],
            out_specs=pl.BlockSpec((None, nt, pps * PAGE_SIZE, HEAD_DIM), lambda b, p, pt: (b, 0, p, 0))),
        compiler_params=_params(("parallel", "arbitrary")),
        name="page_gather",
    )(page_table, *([cache_rows] * pps))


def _stack_heads(q, g0):
    return jnp.concatenate([q[:, (g0 + p) * HEAD_DIM:(g0 + p + 1) * HEAD_DIM] for p in range(HPG)], axis=0).astype(BF16)


def _softmax(sb):
    m = jnp.max(sb, axis=-1, keepdims=True)
    e = jnp.exp(sb - m)
    return e / jnp.sum(e, axis=-1, keepdims=True)


def _cmp_branch(q4, kc, vc, t_rows, slope, n_cmp):
    ncp = kc.shape[0]
    s = _dot_nt(q4, kc) * QK_SCALE
    col = lax.broadcasted_iota(jnp.int32, s.shape, 1)
    dist_i = t_rows - (col * CMP_STRIDE + (CMP_LEN - 1))
    pick = lambda x, other: jnp.where(dist_i >= 0, jnp.where(col < n_cmp, x, other), other)
    p = _softmax(pick(s - _lanes(slope, ncp) * dist_i.astype(F32), NEG_INF))
    p = pick(p, 0.0)
    return _dot(p.astype(BF16), vc), p


def _select_blocks(imp, pool, t_q, n_sel):
    imp_sel = _dot_f32_exact(imp, pool)
    jcol = lax.broadcasted_iota(jnp.int32, imp_sel.shape, 1)
    cur = lax.shift_right_logical(t_q, int(np.log2(SEL_BLOCK)))
    score = jnp.where(jcol * SEL_BLOCK <= t_q, imp_sel, -1.0)
    for forced_blk in (cur - 1, cur, 0):
        score = jnp.where(jcol == forced_blk, FORCE_SCORE, score)
    score = jnp.where(jcol < n_sel, score, -2.0)
    rank = jnp.zeros(score.shape, F32)
    for i in range(n_sel):
        ci = score[:, i:i + 1]
        tie = jnp.where(jcol > i, 1.0, 0.0)
        rank = rank + jnp.where(ci > score, 1.0, jnp.where(ci == score, tie, 0.0))
    top = min(SEL_TOP, n_sel)
    return jnp.where(rank < float(top), 1.0, 0.0)


def _pool_matrix(n_cmp, ncp, n_sel, nsp):
    assert n_cmp + SUB_PER_CMP - 1 <= n_sel * SUB_PER_SEL
    m = np.zeros((ncp, nsp), np.float32)
    for c in range(n_cmp):
        for n in range(SUB_PER_CMP):
            m[c, (c + n) // SUB_PER_SEL] += 1.0 / SUB_PER_CMP
    return jnp.asarray(m, BF16)


def _slope_rows(t_rep):
    h = np.arange(1, B_HEADS + 1, dtype=np.float64)
    s = np.exp2(-8.0 * h / B_HEADS).astype(np.float32).reshape(KV_HEADS, HPG, 1, 1)
    return jnp.asarray(np.broadcast_to(s, (KV_HEADS, HPG, t_rep, LANE)).reshape(KV_HEADS, HPG * t_rep, LANE))


def _expand_matrix(shape, blk0):
    j = lax.broadcasted_iota(jnp.int32, shape, 0)
    k = lax.broadcasted_iota(jnp.int32, shape, 1)
    return jnp.where(j == blk0 + lax.shift_right_logical(k, int(np.log2(SEL_BLOCK))), 1.0, 0.0).astype(BF16)


def _gate_combine(gs, p, tq, o_c, o_s, o_w):
    rows = slice(p * tq, (p + 1) * tq)
    return (gs[:, p:p + 1] * o_c[rows] + gs[:, HPG + p:HPG + p + 1] * o_s[rows]
            + gs[:, 2 * HPG + p:2 * HPG + p + 1] * o_w[rows])


def _attn_prompt_kernel(q_ref, kc_ref, vc_ref, ks_ref, vs_ref, kw_ref, vw_ref, gt_ref, slope_ref, pool_ref,
                        o_ref, ksb, vsb, kwb, vwb, *, n_cmp, n_sel):
    qi = pl.program_id(2)
    tq = q_ref.shape[0]
    rr = HPG * tq

    @pl.when(qi == 0)
    def _():
        ksb[...] = ks_ref[...].astype(BF16)
        vsb[...] = vs_ref[...].astype(BF16)
        kwb[...] = kw_ref[...].astype(BF16)
        vwb[...] = vw_ref[...].astype(BF16)

    t0 = qi * tq
    q4 = _stack_heads(q_ref[...], 0)
    slope = slope_ref[...]

    def t_rows(width):
        r = lax.broadcasted_iota(jnp.int32, (rr, width), 0)
        return t0 + (r & (tq - 1))

    ncp = kc_ref.shape[0]
    o_c, p_c = _cmp_branch(q4, kc_ref[...], vc_ref[...], t_rows(ncp), slope, n_cmp)
    imp = p_c[0:tq]
    for p in range(1, HPG):
        imp = imp + p_c[p * tq:(p + 1) * tq]
    nsp = pool_ref.shape[1]
    t_q = t0 + lax.broadcasted_iota(jnp.int32, (tq, nsp), 0)
    sel = lax.cond(t0 + tq <= min(SEL_TOP, n_sel) * SEL_BLOCK,
                   lambda: jnp.ones((tq, nsp), F32),
                   lambda: _select_blocks(imp, pool_ref[...], t_q, n_sel)).astype(BF16)
    sel4 = jnp.concatenate([sel] * HPG, axis=0)

    ck = ATT_CK
    tr_ck = t_rows(ck)
    slope_ck = _lanes(slope, ck)
    col_ck = lax.broadcasted_iota(jnp.int32, (rr, ck), 1)

    def sel_chunk(c, carry):
        m, l, acc = carry
        k0 = pl.multiple_of(c * ck, ck)
        s = _dot_nt(q4, ksb[pl.ds(k0, ck), :]) * QK_SCALE
        dist_i = tr_ck - (k0 + col_ck)
        selx = _dot(sel4, _expand_matrix((nsp, ck), c * (ck // SEL_BLOCK)))
        sb = jnp.where(dist_i >= 0, jnp.where(selx > 0.5, s - slope_ck * dist_i.astype(F32), NEG_INF), NEG_INF)
        m_new = jnp.maximum(m, jnp.max(sb, axis=-1, keepdims=True))
        alpha = jnp.exp(m - m_new)
        e = jnp.exp(sb - m_new)
        l = alpha * l + jnp.sum(e, axis=-1, keepdims=True)
        acc = alpha * acc + _dot(e.astype(BF16), vsb[pl.ds(k0, ck), :])
        return m_new, l, acc

    n_chunks = (t0 + tq + ck - 1) // ck
    m0 = jnp.full((rr, 1), NEG_INF, F32)
    _, l_s, acc_s = lax.fori_loop(0, n_chunks, sel_chunk, (m0, jnp.zeros((rr, 1), F32), jnp.zeros((rr, HEAD_DIM), F32)))
    o_s = acc_s / l_s

    wk = WINDOW + tq
    w0 = pl.multiple_of(jnp.maximum(t0 - WINDOW, 0), tq)
    s = _dot_nt(q4, kwb[pl.ds(w0, wk), :]) * QK_SCALE
    dist_i = t_rows(wk) - (w0 + lax.broadcasted_iota(jnp.int32, (rr, wk), 1))
    logit = s - _lanes(slope, wk) * dist_i.astype(F32)
    p_w = _softmax(jnp.where(dist_i >= 0, jnp.where(dist_i < WINDOW, logit, NEG_INF), NEG_INF))
    o_w = _dot(p_w.astype(BF16), vwb[pl.ds(w0, wk), :])

    gs = jax.nn.sigmoid(gt_ref[...])
    for p in range(HPG):
        o_ref[:, p * HEAD_DIM:(p + 1) * HEAD_DIM] = _gate_combine(gs, p, tq, o_c, o_s, o_w).astype(o_ref.dtype)


def _attn_prompt(z, gt, kvc_cmp, n_seq, t_len, out_rows=None):
    tq = ATT_TQ
    assert t_len % ATT_CK == 0 and t_len >= WINDOW + tq and tq & (tq - 1) == 0
    n_cmp = (t_len - CMP_LEN) // CMP_STRIDE + 1
    n_sel = -(-t_len // SEL_BLOCK)
    ncp = kvc_cmp.shape[2]
    nsp = _round_up(n_sel, LANE)
    nq = t_len // tq
    rr = HPG * tq
    kvspec = lambda col0: pl.BlockSpec((t_len, HEAD_DIM), lambda b, g, i: (b, col0 + g))
    out_rows = out_rows or n_seq * t_len
    body, base_spec, base, alias = _into_zeros(functools.partial(_attn_prompt_kernel, n_cmp=n_cmp, n_sel=n_sel),
                                               10, out_rows, D_B, BF16)
    return pl.pallas_call(
        body,
        out_shape=jax.ShapeDtypeStruct((out_rows, D_B), BF16),
        grid=(n_seq, KV_HEADS, nq),
        in_specs=[pl.BlockSpec((tq, HPG * HEAD_DIM), lambda b, g, i: (b * nq + i, COL_Q // HPG + g)),
                  pl.BlockSpec((None, None, ncp, HEAD_DIM), lambda b, g, i: (b, g, 0, 0)),
                  pl.BlockSpec((None, None, ncp, HEAD_DIM), lambda b, g, i: (b, KV_HEADS + g, 0, 0)),
                  kvspec(COL_KVS), kvspec(COL_KVS + KV_HEADS), kvspec(COL_KVW), kvspec(COL_KVW + KV_HEADS),
                  pl.BlockSpec((tq, LANE), lambda b, g, i: (b * nq + i, g)),
                  pl.BlockSpec((None, rr, LANE), lambda b, g, i: (g, 0, 0)),
                  pl.BlockSpec((ncp, nsp), lambda b, g, i: (0, 0)), base_spec],
        out_specs=pl.BlockSpec((tq, HPG * HEAD_DIM), lambda b, g, i: (b * nq + i, g)),
        scratch_shapes=[pltpu.VMEM((t_len, HEAD_DIM), BF16)] * 4,
        input_output_aliases=alias,
        compiler_params=_params(("parallel", "parallel", "arbitrary")),
        name="nsa_prompt_attention",
    )(z, kvc_cmp, kvc_cmp, z, z, z, z, gt, _slope_rows(tq), _pool_matrix(n_cmp, ncp, n_sel, nsp), base)


def _attn_dec_select_kernel(q_ref, kc_ref, vc_ref, slope_ref, pool_ref, oc_ref, sel_ref, *, n_cmp, n_sel, past_len):
    ts = q_ref.shape[0]
    rr = HPG * ts
    ncp = kc_ref.shape[0]
    q4 = _stack_heads(q_ref[...], 0)
    t_rows = past_len + (lax.broadcasted_iota(jnp.int32, (rr, ncp), 0) & (ts - 1))
    o_c, p_c = _cmp_branch(q4, kc_ref[...], vc_ref[...], t_rows, slope_ref[...], n_cmp)
    imp = p_c[0:ts]
    for p in range(1, HPG):
        imp = imp + p_c[p * ts:(p + 1) * ts]
    nsp = pool_ref.shape[1]
    t_q = past_len + lax.broadcasted_iota(jnp.int32, (ts, nsp), 0)
    oc_ref[...] = o_c
    sel_ref[...] = _select_blocks(imp, pool_ref[...], t_q, n_sel)


def _attn_dec_select(z, row0, kvc_cmp, n_seq, ts, past_len, n_cmp, n_sel):
    assert row0 % ts == 0 and ts & (ts - 1) == 0 and ts % 8 == 0
    rb0 = row0 // ts
    ncp = kvc_cmp.shape[2]
    nsp = _round_up(n_sel, LANE)
    rr = HPG * ts
    return pl.pallas_call(
        functools.partial(_attn_dec_select_kernel, n_cmp=n_cmp, n_sel=n_sel, past_len=past_len),
        out_shape=(jax.ShapeDtypeStruct((n_seq, KV_HEADS, rr, HEAD_DIM), F32),
                   jax.ShapeDtypeStruct((n_seq, KV_HEADS, ts, nsp), F32)),
        grid=(n_seq, KV_HEADS),
        in_specs=[pl.BlockSpec((ts, HPG * HEAD_DIM), lambda b, g: (rb0 + b, COL_Q // HPG + g)),
                  pl.BlockSpec((None, None, ncp, HEAD_DIM), lambda b, g: (b, g, 0, 0)),
                  pl.BlockSpec((None, None, ncp, HEAD_DIM), lambda b, g: (b, KV_HEADS + g, 0, 0)),
                  pl.BlockSpec((None, rr, LANE), lambda b, g: (g, 0, 0)),
                  pl.BlockSpec((ncp, nsp), lambda b, g: (0, 0))],
        out_specs=(pl.BlockSpec((None, None, rr, HEAD_DIM), lambda b, g: (b, g, 0, 0)),
                   pl.BlockSpec((None, None, ts, nsp), lambda b, g: (b, g, 0, 0))),
        compiler_params=_params(("parallel", "parallel")),
        name="nsa_decode_select",
    )(z, kvc_cmp, kvc_cmp, _slope_rows(ts), _pool_matrix(n_cmp, ncp, n_sel, nsp))


def _attn_dec_kernel(pt_ref, q_ref, sel_ref, *rest, past_len, n_steps, pages_per_step):
    page_refs = rest[:pages_per_step]
    kvs_ref, kvw_ref, win_ref, oc_ref, gt_ref, slope_ref, o_ref, nwin_ref, m_sc, l_sc, acc_sc = rest[pages_per_step:]
    pg = pl.program_id(1)
    ts = q_ref.shape[0]
    rr = HPG * ts
    nsp = sel_ref.shape[-1]
    pad_rows = LANE - ts
    row = lax.broadcasted_iota(jnp.int32, (rr, LANE), 0)
    col = lax.broadcasted_iota(jnp.int32, (rr, LANE), 1)
    t_rows = past_len + (row & (ts - 1))

    @pl.when(pg == 0)
    def _():
        m_sc[...] = jnp.full(m_sc.shape, NEG_INF, F32)
        l_sc[...] = jnp.zeros(l_sc.shape, F32)
        acc_sc[...] = jnp.zeros(acc_sc.shape, F32)

    def sel_update(g, q4, kk, vv, dist_i, expand, real_key):
        width = kk.shape[0]
        s = _dot_nt(q4, kk) * QK_SCALE
        sel4 = jnp.concatenate([sel_ref[g].astype(BF16)] * HPG, axis=0)
        sb = s - _lanes(slope_ref[g], width) * dist_i.astype(F32)
        sb = jnp.where(dist_i >= 0, jnp.where(_dot(sel4, expand) > 0.5, sb, NEG_INF), NEG_INF)
        if real_key is not None:
            sb = jnp.where(real_key, sb, NEG_INF)
        m_old = m_sc[g]
        m_new = jnp.maximum(m_old, jnp.max(sb, axis=-1, keepdims=True))
        alpha = jnp.exp(m_old - m_new)
        e = jnp.exp(sb - m_new)
        l_sc[g] = alpha * l_sc[g] + jnp.sum(e, axis=-1, keepdims=True)
        acc_sc[g] = alpha * acc_sc[g] + _dot(e.astype(BF16), vv)
        m_sc[g] = m_new

    @pl.when(pg < n_steps)
    def _():
        width = pages_per_step * PAGE_SIZE
        k0 = pg * width
        expand = _expand_matrix((nsp, width), pg * (width // SEL_BLOCK))
        wrow_ = lax.broadcasted_iota(jnp.int32, (rr, width), 0)
        wcol_ = lax.broadcasted_iota(jnp.int32, (rr, width), 1)
        dist_i = (past_len + (wrow_ & (ts - 1))) - (k0 + wcol_)
        for g in range(KV_HEADS):
            q4 = _stack_heads(q_ref[...], g * HPG)
            kk = jnp.concatenate([_head_rows(r, g, PAGE_SIZE) for r in page_refs], axis=0).astype(BF16)
            vv = jnp.concatenate([_head_rows(r, KV_HEADS + g, PAGE_SIZE) for r in page_refs], axis=0).astype(BF16)
            sel_update(g, q4, kk, vv, dist_i, expand, None)

    @pl.when(pg == n_steps)
    def _():
        zpad = jnp.zeros((pad_rows, HEAD_DIM), F32)
        gs = jax.nn.sigmoid(gt_ref[...])
        jj = lax.broadcasted_iota(jnp.int32, (nsp, LANE), 0)
        kk_i = lax.broadcasted_iota(jnp.int32, (nsp, LANE), 1)
        new_blk = lax.shift_right_logical(past_len + kk_i, int(np.log2(SEL_BLOCK)))
        expand_new = jnp.where(jj == new_blk, 1.0, 0.0).astype(BF16)
        is_new = col < ts
        nt = 2 * KV_HEADS
        wlen = win_ref.shape[0] // nt
        wrow = lax.broadcasted_iota(jnp.int32, (rr, wlen), 0)
        wcol = lax.broadcasted_iota(jnp.int32, (rr, wlen), 1)
        dist_w = (past_len + (wrow & (ts - 1))) - (past_len - wlen + wcol)
        ok_w = (dist_w >= 0) & (dist_w < WINDOW)
        dist_n = t_rows - (past_len + col)
        ok_n = (dist_n >= 0) & (dist_n < WINDOW) & is_new
        for g in range(KV_HEADS):
            q4 = _stack_heads(q_ref[...], g * HPG)
            ksl = slice(g * HEAD_DIM, (g + 1) * HEAD_DIM)
            vsl = slice(D_KV + g * HEAD_DIM, D_KV + (g + 1) * HEAD_DIM)
            kn = jnp.concatenate([kvs_ref[:, ksl], zpad], axis=0).astype(BF16)
            vn = jnp.concatenate([kvs_ref[:, vsl], zpad], axis=0).astype(BF16)
            sel_update(g, q4, kn, vn, dist_n, expand_new, is_new)
            o_s = acc_sc[g] / l_sc[g]
            slope = slope_ref[g]
            s1 = _dot_nt(q4, _head_rows(win_ref, g, wlen).astype(BF16)) * QK_SCALE
            s1 = jnp.where(ok_w, s1 - _lanes(slope, wlen) * dist_w.astype(F32), NEG_INF)
            kwn = jnp.concatenate([kvw_ref[:, ksl], zpad], axis=0).astype(BF16)
            vwn = jnp.concatenate([kvw_ref[:, vsl], zpad], axis=0).astype(BF16)
            s2 = _dot_nt(q4, kwn) * QK_SCALE
            s2 = jnp.where(ok_n, s2 - slope * dist_n.astype(F32), NEG_INF)
            m = jnp.maximum(jnp.max(s1, axis=-1, keepdims=True), jnp.max(s2, axis=-1, keepdims=True))
            e1 = jnp.exp(s1 - m)
            e2 = jnp.exp(s2 - m)
            inv = 1.0 / (jnp.sum(e1, axis=-1, keepdims=True) + jnp.sum(e2, axis=-1, keepdims=True))
            o_w = (_dot((e1 * inv).astype(BF16), _head_rows(win_ref, KV_HEADS + g, wlen).astype(BF16))
                   + _dot((e2 * inv).astype(BF16), vwn))
            o_c = oc_ref[g]
            for p in range(HPG):
                c0 = (g * HPG + p) * HEAD_DIM
                o_ref[:, c0:c0 + HEAD_DIM] = _gate_combine(gs[:, g * LANE:(g + 1) * LANE], p, ts, o_c, o_s, o_w)
        keep = (wlen - ts) * nt
        nwin_ref[0:keep, :] = win_ref[ts * nt:wlen * nt, :]
        for c in range(nt):
            nwin_ref[pl.ds(keep + c, ts, stride=nt), :] = kvw_ref[:, c * HEAD_DIM:(c + 1) * HEAD_DIM]


def _attn_decode(z, gt, row0, n_seq, ts, page_table, cache_rows, layer, win_rows, o_c, sel, past_len):
    n_pages = page_table.shape[1]
    pps = _pages_per_step(n_pages, 16)
    n_steps = n_pages // pps
    rb0 = row0 // ts
    rr = HPG * ts
    nsp = sel.shape[-1]
    nt = 2 * KV_HEADS
    wrows = win_rows.shape[2]
    w = 2 * D_KV
    assert wrows == WINDOW * nt and past_len % SEL_BLOCK == 0 and PAGE_SIZE == LANE
    last = n_steps - 1

    def page_spec(k):
        return pl.BlockSpec((None, None, PAGE_SIZE * nt, HEAD_DIM),
                            lambda b, p, pt: (layer, pt[b, jnp.minimum(p, last) * pps + k], 0, 0))

    return pl.pallas_call(
        functools.partial(_attn_dec_kernel, past_len=past_len, n_steps=n_steps, pages_per_step=pps),
        out_shape=(jax.ShapeDtypeStruct((n_seq * ts, D_B), F32),
                   jax.ShapeDtypeStruct((n_seq, wrows, HEAD_DIM), F32)),
        grid_spec=pltpu.PrefetchScalarGridSpec(
            num_scalar_prefetch=1, grid=(n_seq, n_steps + 1),
            in_specs=[pl.BlockSpec((ts, D_B), lambda b, p, pt: (rb0 + b, COL_Q * LANE // D_B)),
                      pl.BlockSpec((None, KV_HEADS, ts, nsp), lambda b, p, pt: (b, 0, 0, 0))]
                     + [page_spec(k) for k in range(pps)]
                     + [pl.BlockSpec((ts, w), lambda b, p, pt: (rb0 + b, COL_KVS * LANE // w)),
                        pl.BlockSpec((ts, w), lambda b, p, pt: (rb0 + b, COL_KVW * LANE // w)),
                        pl.BlockSpec((None, None, wrows, HEAD_DIM), lambda b, p, pt: (layer, b, 0, 0)),
                        pl.BlockSpec((None, KV_HEADS, rr, HEAD_DIM), lambda b, p, pt: (b, 0, 0, 0)),
                        pl.BlockSpec((ts, KV_HEADS * LANE), lambda b, p, pt: (rb0 + b, 0)),
                        pl.BlockSpec((KV_HEADS, rr, LANE), lambda b, p, pt: (0, 0, 0))],
            out_specs=(pl.BlockSpec((ts, D_B), lambda b, p, pt: (b, 0)),
                       pl.BlockSpec((None, wrows, HEAD_DIM), lambda b, p, pt: (b, 0, 0))),
            scratch_shapes=[pltpu.VMEM((KV_HEADS, rr, 1), F32), pltpu.VMEM((KV_HEADS, rr, 1), F32),
                            pltpu.VMEM((KV_HEADS, rr, HEAD_DIM), F32)]),
        compiler_params=_params(("parallel", "arbitrary")),
        name="nsa_decode_attention",
    )(page_table, z, sel, *([cache_rows] * pps), z, z, win_rows, o_c, gt, _slope_rows(ts))


def _sconv_kernel(bg_ref, cg_ref, v_ref, buf_ref, cw_ref, y_ref, nbuf_ref):
    t_len = cg_ref.shape[0]
    nb = C_CONV - 1
    cv = cg_ref[...] * v_ref[...]
    buf = buf_ref[...]
    cw = cw_ref[...]
    row = lax.broadcasted_iota(jnp.int32, cv.shape, 0)
    y = cv * cw[nb:nb + 1]
    for s in range(1, C_CONV):
        xs = pltpu.roll(cv, s, 0)
        for r in range(s):
            xs = jnp.where(row == r, buf[nb + r - s:nb + r - s + 1], xs)
        y = y + xs * cw[nb - s:nb - s + 1]
    y_ref[...] = (bg_ref[...] * y).astype(y_ref.dtype)
    nbuf_ref[...] = cv[t_len - nb:t_len]


def _sconv_mixer(zc, row0, n_seq, t_len, buf, conv_w, out_dtype, out_rows=None):
    assert row0 % t_len == 0 and t_len >= C_CONV - 1 and t_len % 8 == 0
    rb0 = row0 // t_len
    nb = C_CONV - 1
    tc = 256 if t_len > 256 else 2048
    nc = D_C // tc
    zspec = lambda third: pl.BlockSpec((t_len, tc), lambda s, c: (rb0 + s, third * nc + c))
    out_rows = out_rows or n_seq * t_len
    body, base_spec, base, alias = _into_zeros(_sconv_kernel, 5, out_rows, D_C, out_dtype)
    return pl.pallas_call(
        body,
        out_shape=(jax.ShapeDtypeStruct((out_rows, D_C), out_dtype),
                   jax.ShapeDtypeStruct((n_seq, nb, D_C), F32)),
        grid=(n_seq, nc),
        in_specs=[zspec(0), zspec(1), zspec(2),
                  pl.BlockSpec((None, nb, tc), lambda s, c: (s, 0, c)),
                  pl.BlockSpec((C_CONV, tc), lambda s, c: (0, c)), base_spec],
        out_specs=(pl.BlockSpec((t_len, tc), lambda s, c: (s, c)),
                   pl.BlockSpec((None, nb, tc), lambda s, c: (s, 0, c))),
        input_output_aliases=alias,
        compiler_params=_params(("parallel", "parallel")),
        name="sconv_mixer",
    )(zc, zc, zc, buf, conv_w, base)


def _gate_weight(w_in):
    nl, k = w_in.shape[:2]
    wg = w_in[:, :, D_IN_MAIN:].reshape(nl, k, 3, KV_HEADS, HPG).transpose(0, 1, 3, 2, 4)
    wg = jnp.pad(wg.reshape(nl, k, KV_HEADS, 3 * HPG), ((0, 0), (0, 0), (0, 0), (0, LANE - 3 * HPG)))
    return wg.reshape(nl, k, KV_HEADS * LANE).astype(BF16)


def _ab_layer(h, hn, i, mp, bp, t_len, bs, ts, past_len, w, cache_kv_cmp, cache_kv_sel, state_win_kv, state_lru_h,
              state_lru_conv, page_table):
    z, kv_rows = _in_proj(hn, w["ab_w_in_bf16"], i)
    gt = _matmul(hn, w["ab_w_gate_bf16"], i, name="ab_gate_proj")
    nt = 2 * KV_HEADS
    rows_view = lambda a: a.reshape(a.shape[0], a.shape[1], -1, HEAD_DIM)
    lru_w = (w["ab_conv_w"][i], w["ab_conv_b"][i], w["ab_gate_a_w"][i], w["ab_gate_a_b"][i], w["ab_gate_x_w"][i],
             w["ab_gate_x_b"][i], w["ab_lru_lambda"][i])
    cmp_w = (w["ab_cmp_pe"][i], w["ab_cmp_w1"][i], w["ab_cmp_b1"][i], w["ab_cmp_w2"][i])
    tk = past_len + ts
    n_cmp_s = (tk - CMP_LEN) // CMP_STRIDE + 1
    n_sel_s = -(-tk // SEL_BLOCK)
    assert n_cmp_s == (past_len - CMP_LEN) // CMP_STRIDE + 1

    ya_p, hT_p, nbuf_p = _lru_mixer(z, 0, bp, t_len, jnp.zeros((bp, D_A), F32), jnp.zeros((bp, A_CONV - 1, D_A), F32),
                                    *lru_w, out_dtype=BF16, out_rows=mp + bs * ts)
    kvc_p = _compress(z, lambda s, c: (s, COL_KVC + c), bp, t_len, *cmp_w)
    ob_p = _attn_prompt(z, gt, kvc_p, bp, t_len, out_rows=mp + bs * ts)

    ya_s, hT_s, nbuf_s = _lru_mixer(z, mp, bs, ts, state_lru_h[i], state_lru_conv[i], *lru_w, out_dtype=F32)
    past_cmp = _page_gather(rows_view(cache_kv_cmp), i, page_table)
    kvc_s = _compress(past_cmp.reshape(bs * nt * past_len, HEAD_DIM), lambda s, c: (s * nt + c, 0), bs, past_len, *cmp_w)
    oc_s, sel_s = _attn_dec_select(z, mp, kvc_s, bs, ts, past_len, n_cmp_s, n_sel_s)
    ob_s, nwin_s = _attn_decode(z, gt, mp, bs, ts, page_table, rows_view(cache_kv_sel), i, rows_view(state_win_kv),
                                oc_s, sel_s, past_len)

    ya = lax.dynamic_update_slice(ya_p, ya_s.astype(BF16), (mp, 0))
    ob = lax.dynamic_update_slice(ob_p, ob_s.astype(BF16), (mp, 0))
    h = _matmul2_residual(ya, ob, w["ab_w_out_bf16"], i, h, name="ab_out_proj")

    kv5 = lambda a, nb_, tt: a.reshape(nb_, tt, 2, KV_HEADS, HEAD_DIM)
    wl = min(WINDOW, t_len)
    kvw_p = kv5(kv_rows[2, :mp * nt], bp, t_len)
    outs_p = (kv5(kv_rows[0, :mp * nt], bp, t_len), kv5(kv_rows[1, :mp * nt], bp, t_len),
              kvw_p[:, t_len - wl:], hT_p.reshape(bp, D_A), nbuf_p)
    outs_s = (kv5(kv_rows[0, mp * nt:], bs, ts), kv5(kv_rows[1, mp * nt:], bs, ts),
              kv5(nwin_s, bs, WINDOW), hT_s.reshape(bs, D_A), nbuf_s)
    return h, outs_p, outs_s


def _c_layer(h, hn, i, mp, bp, t_len, bs, ts, w, state_sconv):
    zc = _matmul(hn, w["c_w_in"], i, name="c_in_proj", row_cap=WIDE_ROW_CAP, col_cap=F32_WEIGHT_COL_CAP)
    y_p, nbuf_p = _sconv_mixer(zc, 0, bp, t_len, jnp.zeros((bp, C_CONV - 1, D_C), F32), w["c_conv_w"][i], BF16,
                               out_rows=mp + bs * ts)
    y_s, nbuf_s = _sconv_mixer(zc, mp, bs, ts, state_sconv[i], w["c_conv_w"][i], F32)
    y = lax.dynamic_update_slice(y_p, y_s.astype(BF16), (mp, 0))
    h = _matmul_residual(y, w["c_w_out_bf16"], i, h, name="c_out_proj", row_cap=WIDE_ROW_CAP)
    return h, nbuf_p, nbuf_s


def kernel(x_prompt, x_sample, cache_kv_cmp, cache_kv_sel, state_win_kv, state_lru_h, state_lru_conv, state_sconv,
           page_table, norm_mix, norm_ffn, norm_final, ab_w_in, ab_conv_w, ab_conv_b, ab_gate_a_w, ab_gate_a_b,
           ab_gate_x_w, ab_gate_x_b, ab_lru_lambda, ab_cmp_pe, ab_cmp_w1, ab_cmp_b1, ab_cmp_w2, ab_w_out,
           c_w_in, c_conv_w, c_w_out, ffn_w_gate, ffn_w_up, ffn_w_down):
    w = dict(ab_w_in_bf16=ab_w_in.astype(BF16), ab_w_gate_bf16=_gate_weight(ab_w_in), ab_conv_w=ab_conv_w,
             ab_conv_b=ab_conv_b, ab_gate_a_w=ab_gate_a_w, ab_gate_a_b=ab_gate_a_b, ab_gate_x_w=ab_gate_x_w,
             ab_gate_x_b=ab_gate_x_b, ab_lru_lambda=ab_lru_lambda, ab_cmp_pe=ab_cmp_pe, ab_cmp_w1=ab_cmp_w1,
             ab_cmp_b1=ab_cmp_b1, ab_cmp_w2=ab_cmp_w2, ab_w_out_bf16=ab_w_out.astype(BF16),
             c_w_in=c_w_in, c_conv_w=c_conv_w, c_w_out_bf16=c_w_out.astype(BF16))
    bp, t_len = x_prompt.shape[:2]
    bs, ts = x_sample.shape[:2]
    mp, ms = bp * t_len, bs * ts
    past_len = page_table.shape[1] * PAGE_SIZE
    h = jnp.concatenate([x_prompt.reshape(mp, D_MODEL), x_sample.reshape(ms, D_MODEL)], axis=0)
    p_ab, s_ab, p_c, s_c = [], [], [], []
    for layer in range(DEPTH):
        hn = _rmsnorm(h, norm_mix[layer], BF16)
        i = layer // 2
        if layer % 2 == 0:
            h, op, os_ = _ab_layer(h, hn, i, mp, bp, t_len, bs, ts, past_len, w, cache_kv_cmp, cache_kv_sel,
                                   state_win_kv, state_lru_h, state_lru_conv, page_table)
            p_ab.append(op)
            s_ab.append(os_)
        else:
            h, nbp, nbs = _c_layer(h, hn, i, mp, bp, t_len, bs, ts, w, state_sconv)
            p_c.append(nbp)
            s_c.append(nbs)
        hf = _rmsnorm(h, norm_ffn[layer], BF16)
        hid, wd_bf16 = _ffn_up(hf, ffn_w_gate, ffn_w_up, ffn_w_down, layer)
        h = _matmul_residual(hid, wd_bf16[None], 0, h, name="ffn_down", col_cap=FFN_DOWN_COL_CAP)
    y_prompt = _rmsnorm(h, norm_final, F32, 0, mp).reshape(bp, t_len, D_MODEL)
    y_sample = _rmsnorm(h, norm_final, F32, mp, ms).reshape(bs, ts, D_MODEL)
    stack = lambda rows, j: jnp.stack([r[j] for r in rows])
    return (y_prompt, y_sample,
            stack(p_ab, 0), stack(p_ab, 1), stack(p_ab, 2), stack(p_ab, 3), stack(p_ab, 4), jnp.stack(p_c),
            stack(s_ab, 0), stack(s_ab, 1), stack(s_ab, 2), stack(s_ab, 3), stack(s_ab, 4), jnp.stack(s_c))
```
